```python
import jax, jax.numpy as jnp
from jax import lax
import numpy as np

D_MODEL = 1024
BATCH = 2
SEQ = 8192
DEPTH = 2

GRID_W = 64
CTX_LEN = 256
N_MIXERS = 2
GLA_HEADS = 4
GLA_DK = D_MODEL // 2
GLA_DV = D_MODEL
GLA_DK_HEAD = GLA_DK // GLA_HEADS
GLA_DV_HEAD = GLA_DV // GLA_HEADS
GLA_LOW_RANK = 16
GLA_GATE_NORM = 16.0
GLA_CHUNK = 64
ROPE_BASE = 10000.0
NA_HEADS = 16
NA_HEAD_DIM = D_MODEL // NA_HEADS
NA_KH = 8
NA_KW = 16
D_FF = 2816
CONV_W = 3
EPS = 1e-6

kernel_name = "hybrid_gla_natten_convffn_prefix_dit"


def rms_norm(x, g):
    xf = x.astype(jnp.float32)
    y = xf * lax.rsqrt(jnp.mean(xf * xf, axis=-1, keepdims=True) + EPS)
    return (y * g.astype(jnp.float32)).astype(x.dtype)


def modulate(x, g, shift, scale):
    return rms_norm(x, g) * (1 + scale) + shift


def rope_2d(x, rows, cols):
    half = x.shape[-1] // 2
    inv = 1.0 / (ROPE_BASE ** (jnp.arange(0, half, 2, dtype=jnp.float32) / half))

    def rot(xa, pos):
        ang = pos.astype(jnp.float32)[:, None] * inv[None, :]
        cos = jnp.cos(ang)[:, None, :].astype(x.dtype)
        sin = jnp.sin(ang)[:, None, :].astype(x.dtype)
        x1, x2 = jnp.split(xa, 2, axis=-1)
        return jnp.concatenate([x1 * cos - x2 * sin, x1 * sin + x2 * cos], axis=-1)

    return jnp.concatenate([rot(x[..., :half], rows), rot(x[..., half:], cols)], axis=-1)


def gla_chunked(q, k, v, g, s0):
    B, H, T, dk = q.shape
    dv = v.shape[-1]
    n = T // GLA_CHUNK

    def to_chunks(t):
        return jnp.moveaxis(t.reshape(B, H, n, GLA_CHUNK, t.shape[-1]), 2, 0)

    mask = jnp.tril(jnp.ones((GLA_CHUNK, GLA_CHUNK), dtype=bool))[:, :, None]

    def step(s, xs):
        qc, kc, vc, gc = xs
        b = jnp.cumsum(gc, axis=2)
        b_last = b[:, :, -1:, :]
        o_inter = jnp.einsum('bhid,bhde->bhie', qc * jnp.exp(b), s)
        diff = jnp.where(mask, b[:, :, :, None, :] - b[:, :, None, :, :], -jnp.inf)
        att = jnp.einsum('bhid,bhjd,bhijd->bhij', qc, kc, jnp.exp(diff))
        o = o_inter + jnp.einsum('bhij,bhje->bhie', att, vc)
        s_new = jnp.exp(b_last)[:, :, 0, :, None] * s + jnp.einsum('bhjd,bhje->bhde', kc * jnp.exp(b_last - b), vc)
        return s_new, o

    s_final, o = lax.scan(step, s0, (to_chunks(q), to_chunks(k), to_chunks(v), to_chunks(g)))
    o = jnp.moveaxis(o, 0, 2).reshape(B, H, T, dv)
    return o, s_final


def gla_mixer(h_ctx, h_lat, w_in, a_w1, a_w2, a_b, norm_g, w_out, with_ctx_out):
    B, N, _ = h_lat.shape
    L = h_ctx.shape[1]
    T = L + N
    h = jnp.concatenate([h_ctx, h_lat], axis=1)
    proj = h @ w_in
    q, k, v, r = jnp.split(proj, [GLA_DK, 2 * GLA_DK, 2 * GLA_DK + GLA_DV], axis=-1)
    q = q.reshape(B, T, GLA_HEADS, GLA_DK_HEAD)
    k = k.reshape(B, T, GLA_HEADS, GLA_DK_HEAD)
    v = v.reshape(B, T, GLA_HEADS, GLA_DV_HEAD)
    pos = jnp.arange(N)
    rows, cols = pos // GRID_W, pos % GRID_W
    q = jnp.concatenate([q[:, :L], rope_2d(q[:, L:], rows, cols)], axis=1) * (GLA_DK_HEAD ** -0.5)
    k = jnp.concatenate([k[:, :L], rope_2d(k[:, L:], rows, cols)], axis=1)
    to_bhtd = lambda t: jnp.transpose(t, (0, 2, 1, 3))
    q, k, v = to_bhtd(q), to_bhtd(k), to_bhtd(v)
    s0 = jnp.zeros((B, GLA_HEADS, GLA_DK_HEAD, GLA_DV_HEAD), dtype=q.dtype)
    flip = lambda t: jnp.flip(t, axis=2)

    o_sum = None
    for d in range(2):
        g = jax.nn.log_sigmoid((h @ a_w1[d]) @ a_w2[d] + a_b[d]) / GLA_GATE_NORM
        g = to_bhtd(g.reshape(B, T, GLA_HEADS, GLA_DK_HEAD))
        qd, kd, vd, gd = q, k, v, g
        if d == 1:
            qd, kd, vd, gd = flip(q), flip(k), flip(v), flip(g)
            o_c, s_c = gla_chunked(qd[:, :, N:], kd[:, :, N:], vd[:, :, N:], gd[:, :, N:], s0)
            o_l, _ = gla_chunked(qd[:, :, :N], kd[:, :, :N], vd[:, :, :N], gd[:, :, :N], s_c)
            o_d = jnp.concatenate([flip(o_c), flip(o_l)], axis=2)
        else:
            o_c, s_c = gla_chunked(qd[:, :, :L], kd[:, :, :L], vd[:, :, :L], gd[:, :, :L], s0)
            o_l, _ = gla_chunked(qd[:, :, L:], kd[:, :, L:], vd[:, :, L:], gd[:, :, L:], s_c)
            o_d = jnp.concatenate([o_c, o_l], axis=2)
        o_sum = o_d if o_sum is None else o_sum + o_d

    o = jnp.transpose(o_sum, (0, 2, 1, 3))
    r = r.reshape(B, T, GLA_HEADS, GLA_DV_HEAD)
    if not with_ctx_out:
        o, r = o[:, L:], r[:, L:]
    y = (rms_norm(o, norm_g) * jax.nn.silu(r)).reshape(B, -1, GLA_DV) @ w_out
    if with_ctx_out:
        return y[:, :L], y[:, L:]
    return None, y


def na_mixer(h_ctx, h_lat, w_qkv, rpb, w_out, with_ctx_out):
    B, N, _ = h_lat.shape
    L = h_ctx.shape[1]
    rows_n = N // GRID_W
    kh = min(NA_KH, rows_n)
    q_l, k_l, v_l = jnp.split(h_lat @ w_qkv, 3, axis=-1)
    q_c, k_c, v_c = jnp.split(h_ctx @ w_qkv, 3, axis=-1)
    sc = NA_HEAD_DIM ** -0.5
    q_g = (q_l * sc).reshape(B, rows_n, GRID_W, NA_HEADS, NA_HEAD_DIM)
    k_g = k_l.reshape(B, rows_n, GRID_W, NA_HEADS, NA_HEAD_DIM)
    v_g = v_l.reshape(B, rows_n, GRID_W, NA_HEADS, NA_HEAD_DIM)
    k_c = k_c.reshape(B, L, NA_HEADS, NA_HEAD_DIM)
    v_c = v_c.reshape(B, L, NA_HEADS, NA_HEAD_DIM)

    cpos = np.arange(GRID_W)
    col_start = np.clip(cpos - NA_KW // 2, 0, GRID_W - NA_KW)
    col_idx = col_start[:, None] + np.arange(NA_KW)[None, :]
    dc_idx = col_idx - cpos[:, None] + (NA_KW - 1)
    rpb_c = rpb[:, :, dc_idx]

    def row_block(r):
        rs = jnp.clip(r - kh // 2, 0, rows_n - kh)
        q_r = lax.dynamic_index_in_dim(q_g, r, axis=1, keepdims=False)
        k_rows = lax.dynamic_slice_in_dim(k_g, rs, kh, axis=1)
        v_rows = lax.dynamic_slice_in_dim(v_g, rs, kh, axis=1)
        k_win = k_rows[:, :, col_idx]
        v_win = v_rows[:, :, col_idx]
        s_nb = jnp.einsum('bchd,bicjhd->bhcij', q_r, k_win).reshape(B, NA_HEADS, GRID_W, kh * NA_KW)
        dr_idx = rs + jnp.arange(kh) - r + (NA_KH - 1)
        bias = jnp.take(rpb_c, dr_idx, axis=1)
        bias = jnp.transpose(bias, (0, 2, 1, 3)).reshape(NA_HEADS, GRID_W, kh * NA_KW)
        s_nb = s_nb + bias[None]
        s_cx = jnp.einsum('bchd,blhd->bhcl', q_r, k_c)
        p = jax.nn.softmax(jnp.concatenate([s_nb, s_cx], axis=-1).astype(jnp.float32), axis=-1).astype(v_l.dtype)
        p_nb = p[..., :kh * NA_KW].reshape(B, NA_HEADS, GRID_W, kh, NA_KW)
        p_cx = p[..., kh * NA_KW:]
        return jnp.einsum('bhcij,bicjhd->bchd', p_nb, v_win) + jnp.einsum('bhcl,blhd->bchd', p_cx, v_c)

    o = lax.map(row_block, jnp.arange(rows_n))
    o_lat = jnp.transpose(o, (1, 0, 2, 3, 4)).reshape(B, N, D_MODEL) @ w_out
    if not with_ctx_out:
        return None, o_lat
    qc = (q_c * sc).reshape(B, L, NA_HEADS, NA_HEAD_DIM)
    s = jnp.einsum('blhd,bmhd->bhlm', qc, k_c)
    p = jax.nn.softmax(s.astype(jnp.float32), axis=-1).astype(v_c.dtype)
    o_ctx = jnp.einsum('bhlm,bmhd->blhd', p, v_c).reshape(B, L, D_MODEL) @ w_out
    return o_ctx, o_lat


def conv_ffn(h, w_in, conv_w, conv_b, w_out):
    a, val = jnp.split(h @ w_in, 2, axis=-1)
    ap = jnp.pad(a, ((0, 0), (1, 1), (0, 0)))
    a = ap[:, :-2] * conv_w[0] + ap[:, 1:-1] * conv_w[1] + ap[:, 2:] * conv_w[2] + conv_b
    return (jax.nn.silu(a) * val) @ w_out


def setup_inputs(seed: int = 0) -> dict:
    key = jax.random.key(seed)
    ks = jax.random.split(key, 24)
    n_gla = (DEPTH + N_MIXERS - 1) // N_MIXERS
    n_na = DEPTH // N_MIXERS
    D = D_MODEL

    def nrm(k, shape, scale):
        return jax.random.normal(k, shape, jnp.float32) * scale

    return {
        "x": nrm(ks[0], (BATCH, SEQ, D), 1.0),
        "c": nrm(ks[1], (BATCH, D), 1.0),
        "ctx": nrm(ks[2], (BATCH, CTX_LEN, D), 1.0),
        "c_ctx": nrm(ks[3], (D,), 1.0),
        "ada_w": nrm(ks[4], (DEPTH, D, 6 * D), 0.5 * D ** -0.5),
        "ada_b": nrm(ks[5], (DEPTH, 6 * D), 0.02),
        "norm1_g": 1.0 + nrm(ks[6], (DEPTH, D), 0.02),
        "norm2_g": 1.0 + nrm(ks[7], (DEPTH, D), 0.02),
        "ffn_w_in": nrm(ks[8], (DEPTH, D, 2 * D_FF), D ** -0.5),
        "ffn_conv_w": nrm(ks[9], (DEPTH, CONV_W, D_FF), CONV_W ** -0.5),
        "ffn_conv_b": nrm(ks[10], (DEPTH, D_FF), 0.02),
        "ffn_w_out": nrm(ks[11], (DEPTH, D_FF, D), D_FF ** -0.5),
        "gla_w_in": nrm(ks[12], (n_gla, D, 2 * GLA_DK + 2 * GLA_DV), D ** -0.5),
        "gla_a_w1": nrm(ks[13], (n_gla, 2, D, GLA_LOW_RANK), D ** -0.5),
        "gla_a_w2": nrm(ks[14], (n_gla, 2, GLA_LOW_RANK, GLA_DK), GLA_LOW_RANK ** -0.5),
        "gla_a_b": nrm(ks[15], (n_gla, 2, GLA_DK), 0.1),
        "gla_norm_g": 1.0 + nrm(ks[16], (n_gla, GLA_DV_HEAD), 0.02),
        "gla_w_out": nrm(ks[17], (n_gla, GLA_DV, D), GLA_DV ** -0.5),
        "na_w_qkv": nrm(ks[18], (n_na, D, 3 * D), D ** -0.5),
        "na_rpb": nrm(ks[19], (n_na, NA_HEADS, 2 * NA_KH - 1, 2 * NA_KW - 1), 0.1),
        "na_w_out": nrm(ks[20], (n_na, D, D), D ** -0.5),
        "final_g": 1.0 + nrm(ks[21], (D,), 0.02),
    }


def reference(x, c, ctx, c_ctx, ada_w, ada_b, norm1_g, norm2_g, ffn_w_in, ffn_conv_w, ffn_conv_b, ffn_w_out,
              gla_w_in, gla_a_w1, gla_a_w2, gla_a_b, gla_norm_g, gla_w_out, na_w_qkv, na_rpb, na_w_out, final_g):
    x_lat, x_ctx = x, ctx
    sc_lat = jax.nn.silu(c)
    sc_ctx = jax.nn.silu(c_ctx)
    for i in range(DEPTH):
        last = i == DEPTH - 1
        j = i // N_MIXERS
        mod_l = (sc_lat @ ada_w[i] + ada_b[i])[:, None, :]
        mod_c = sc_ctx @ ada_w[i] + ada_b[i]
        sh1_l, s1_l, g1_l, sh2_l, s2_l, g2_l = jnp.split(mod_l, 6, axis=-1)
        sh1_c, s1_c, g1_c, sh2_c, s2_c, g2_c = jnp.split(mod_c, 6, axis=-1)

        h_l = modulate(x_lat, norm1_g[i], sh1_l, s1_l)
        h_c = modulate(x_ctx, norm1_g[i], sh1_c, s1_c)
        if i % N_MIXERS == 0:
            o_c, o_l = gla_mixer(h_c, h_l, gla_w_in[j], gla_a_w1[j], gla_a_w2[j], gla_a_b[j],
                                 gla_norm_g[j], gla_w_out[j], not last)
        else:
            o_c, o_l = na_mixer(h_c, h_l, na_w_qkv[j], na_rpb[j], na_w_out[j], not last)
        x_lat = x_lat + g1_l * o_l

        h_l = modulate(x_lat, norm2_g[i], sh2_l, s2_l)
        x_lat = x_lat + g2_l * conv_ffn(h_l, ffn_w_in[i], ffn_conv_w[i], ffn_conv_b[i], ffn_w_out[i])

        if not last:
            x_ctx = x_ctx + g1_c * o_c
            h_c = modulate(x_ctx, norm2_g[i], sh2_c, s2_c)
            x_ctx = x_ctx + g2_c * conv_ffn(h_c, ffn_w_in[i], ffn_conv_w[i], ffn_conv_b[i], ffn_w_out[i])
    return rms_norm(x_lat, final_g)
```

```python
import functools

import numpy as np
import jax
import jax.numpy as jnp
from jax import lax
from jax.experimental import pallas as pl
from jax.experimental.pallas import tpu as pltpu

GRID_W = 64
GLA_HEADS = 4
GLA_GATE_NORM = 16.0
GLA_LOW_RANK = 16
ROPE_BASE = 10000.0
NA_HEADS = 16
NA_KH = 8
NA_KW = 16
EPS = 1e-6

LANES = 128
SUBLANES = 8
VMEM_LIMIT = 56 * 1024 * 1024

TM = 256
GLA_C = 128
GLA_DIAG = 8
NA_R = 4
NA_WIN = 12
NA_HG = 4
NEG = -1e30

F32 = jnp.float32
BF16 = jnp.bfloat16


def _cparams(sem):
    return pltpu.CompilerParams(dimension_semantics=sem, vmem_limit_bytes=VMEM_LIMIT)


def _resident(shape):
    nd = len(shape)
    return pl.BlockSpec(shape, lambda *_: (0,) * nd, pipeline_mode=pl.Buffered(1))


def _silu(x):
    return x * (1.0 / (1.0 + jnp.exp(-x)))


def _norm_mod(x, g, shift, scale):
    ms = jnp.mean(x * x, axis=-1, keepdims=True)
    y = x * lax.rsqrt(ms + EPS) * g
    return y * (1.0 + scale) + shift


def _ada_kernel(c_ref, w_ref, b_ref, o_ref):
    sc = _silu(c_ref[...])
    o_ref[0] = jnp.dot(sc, w_ref[0], preferred_element_type=F32,
                       precision=lax.Precision.HIGHEST) + b_ref[0]


def _ada(c_all, ada_w, ada_b):
    depth, d, d6 = ada_w.shape
    tn = 1536
    return pl.pallas_call(
        _ada_kernel,
        grid=(depth, d6 // tn),
        in_specs=[
            pl.BlockSpec((SUBLANES, d), lambda i, j: (0, 0)),
            pl.BlockSpec((1, d, tn), lambda i, j: (i, 0, j)),
            pl.BlockSpec((1, 1, tn), lambda i, j: (i, 0, j)),
        ],
        out_specs=pl.BlockSpec((1, SUBLANES, tn), lambda i, j: (i, 0, j)),
        out_shape=jax.ShapeDtypeStruct((depth, SUBLANES, d6), F32),
        compiler_params=_cparams(("parallel", "parallel")),
        name="ada_mod",
    )(c_all, ada_w, ada_b.reshape(depth, 1, d6))


def _mod_spec(d, seg, nb, tile_off):
    def imap(b, i):
        return (jnp.where(i + tile_off == 0, nb, b), 0, seg)
    return pl.BlockSpec((1, 1, d), imap)


def _gla_in_kernel(x_ref, sh_ref, sc_ref, g_ref, w_ref, aw1_ref, aw2_ref, ab_ref,
                   cos_ref, sin_ref, q_ref, k_ref, v_ref, r_ref, gf_ref, gb_ref, *, dk, dv):
    h = _norm_mod(x_ref[0], g_ref[...], sh_ref[0], sc_ref[0]).astype(BF16)
    proj = jnp.dot(h, w_ref[...], preferred_element_type=F32)
    cos = cos_ref[...]
    sin = sin_ref[...]
    lane = lax.broadcasted_iota(jnp.int32, cos.shape, 1)
    low = (lane & 32) == 0
    qscale = float(LANES) ** -0.5

    def rope(t):
        partner = jnp.where(low, pltpu.roll(t, LANES - 32, 1), pltpu.roll(t, 32, 1))
        return t * cos + partner * sin

    for hd in range(dk // LANES):
        sl = slice(hd * LANES, (hd + 1) * LANES)
        q_ref[0, :, sl] = rope(proj[:, sl]) * qscale
        k_ref[0, :, sl] = rope(proj[:, dk + hd * LANES: dk + (hd + 1) * LANES])
    v_ref[0] = proj[:, 2 * dk: 2 * dk + dv].astype(BF16)
    r_ref[0] = proj[:, 2 * dk + dv:].astype(BF16)

    z = jnp.dot(h, aw1_ref[...], preferred_element_type=F32)
    for d, o_ref in enumerate((gf_ref, gb_ref)):
        pre = jnp.dot(z, aw2_ref[d], preferred_element_type=F32,
                      precision=lax.Precision.HIGHEST) + ab_ref[d]
        ls = jnp.minimum(pre, 0.0) - jnp.log1p(jnp.exp(-jnp.abs(pre)))
        o_ref[0] = ls * (1.0 / GLA_GATE_NORM)


def _gla_in(xs, mod, norm_g, w_in, aw1, aw2, ab, cos_t, sin_t, dk, dv):
    nb, t, d = xs.shape
    nt = t // TM
    kern = functools.partial(_gla_in_kernel, dk=dk, dv=dv)
    row = lambda w: pl.BlockSpec((1, TM, w), lambda b, i: (b, i, 0))
    return pl.pallas_call(
        kern,
        grid=(nb, nt),
        in_specs=[
            row(d),
            _mod_spec(d, 0, nb, 0),
            _mod_spec(d, 1, nb, 0),
            _resident((1, d)),
            _resident(w_in.shape),
            _resident(aw1.shape),
            _resident(aw2.shape),
            _resident(ab.shape),
            pl.BlockSpec((TM, LANES), lambda b, i: (i, 0)),
            pl.BlockSpec((TM, LANES), lambda b, i: (i, 0)),
        ],
        out_specs=[row(dk), row(dk), row(dv), row(dv), row(dk), row(dk)],
        out_shape=[
            jax.ShapeDtypeStruct((nb, t, dk), F32),
            jax.ShapeDtypeStruct((nb, t, dk), F32),
            jax.ShapeDtypeStruct((nb, t, dv), BF16),
            jax.ShapeDtypeStruct((nb, t, dv), BF16),
            jax.ShapeDtypeStruct((nb, t, dk), F32),
            jax.ShapeDtypeStruct((nb, t, dk), F32),
        ],
        compiler_params=_cparams(("parallel", "parallel")),
        name="gla_in",
    )(xs, mod, mod, norm_g, w_in, aw1, aw2, ab, cos_t, sin_t)


def _row_bcast(b, period, offset):
    c, w = b.shape
    if period == c:
        return jnp.broadcast_to(b[offset:offset + 1, :], (c, w))
    b3 = b.reshape(c // period, period, w)
    return jnp.broadcast_to(b3[:, offset:offset + 1, :], b3.shape).reshape(c, w)


def _gla_chain(q, k, v, g, s, rev, row, same, causal):
    c = q.shape[0]
    b = g
    sh = 1
    while sh < c:
        if rev:
            b = b + jnp.where(row < c - sh, pltpu.roll(b, c - sh, 0), 0.0)
        else:
            b = b + jnp.where(row >= sh, pltpu.roll(b, sh, 0), 0.0)
        sh *= 2
    b_last = b[0:1, :] if rev else b[c - 1:c, :]
    q_in = (q * jnp.exp(b)).astype(BF16)
    k_out = (k * jnp.exp(b_last - b)).astype(BF16)

    nt = (((1,), (1,)), ((), ()))
    b_ref = _row_bcast(b, GLA_DIAG, GLA_DIAG - 1 if rev else 0)
    qd = (q * jnp.exp(b - b_ref)).astype(BF16)
    kd = (k * jnp.exp(b_ref - b)).astype(BF16)
    att = jnp.where(causal, lax.dot_general(qd, kd, nt, preferred_element_type=F32), 0.0)
    period = 2 * GLA_DIAG
    while period <= c:
        half = period // 2
        off = row & (period - 1)
        kside = (off >= half) if rev else (off < half)
        b_ref = _row_bcast(b, period, half if rev else half - 1)
        e = jnp.exp(-jnp.abs(b - b_ref))
        ql = jnp.where(kside, 0.0, q * e).astype(BF16)
        kl = jnp.where(kside, k * e, 0.0).astype(BF16)
        a_l = lax.dot_general(ql, kl, nt, preferred_element_type=F32)
        att = jnp.where(same[half], att, a_l)
        period *= 2
    lhs = jnp.concatenate([q_in, att.astype(BF16)], axis=1)
    rhs = jnp.concatenate([s.astype(BF16), v], axis=0)
    o = jnp.dot(lhs, rhs, preferred_element_type=F32)
    decay = jnp.transpose(jnp.broadcast_to(jnp.exp(b_last), (LANES, LANES)))
    decay = jnp.concatenate([decay] * (s.shape[1] // LANES), axis=1)
    kv = lax.dot_general(k_out, v, (((0,), (0,)), ((), ())), preferred_element_type=F32)
    return o, s * decay + kv


def _gla_core_kernel(qf_ref, kf_ref, vf_ref, gf_ref, qb_ref, kb_ref, vb_ref, gb_ref,
                     of_ref, ob_ref, s_ref, *, heads, dvh):
    @pl.when(pl.program_id(1) == 0)
    def _():
        s_ref[...] = jnp.zeros_like(s_ref)

    c = GLA_C
    row = lax.broadcasted_iota(jnp.int32, (c, LANES), 0)
    ti = lax.broadcasted_iota(jnp.int32, (c, c), 0)
    tj = lax.broadcasted_iota(jnp.int32, (c, c), 1)
    x = ti ^ tj
    same = {}
    half = GLA_DIAG
    while half < c:
        same[half] = x < half
        half *= 2
    causal_f = same[GLA_DIAG] & (tj <= ti)
    causal_b = same[GLA_DIAG] & (tj >= ti)

    for d, (q_ref, k_ref, v_ref, g_ref, o_ref) in enumerate((
            (qf_ref, kf_ref, vf_ref, gf_ref, of_ref),
            (qb_ref, kb_ref, vb_ref, gb_ref, ob_ref))):
        for hd in range(heads):
            sl = slice(hd * LANES, (hd + 1) * LANES)
            vsl = slice(hd * dvh, (hd + 1) * dvh)
            o, s_new = _gla_chain(q_ref[0, :, sl], k_ref[0, :, sl], v_ref[0, :, vsl],
                                  g_ref[0, :, sl], s_ref[d, hd], d == 1, row, same,
                                  causal_b if d == 1 else causal_f)
            o_ref[0, :, vsl] = o
            s_ref[d, hd] = s_new


def _gla_core(q, k, v, gf, gb, heads, n_ctx):
    nb, t, dk = q.shape
    dv = v.shape[-1]
    nc = t // GLA_C
    ncx = n_ctx // GLA_C
    fwd = lambda b, s: (b, s, 0)
    bwd = lambda b, s: (b, jnp.where(s < ncx, ncx - 1 - s, nc - 1 + ncx - s), 0)
    kern = functools.partial(_gla_core_kernel, heads=heads, dvh=dv // heads)
    blk = lambda w, m: pl.BlockSpec((1, GLA_C, w), m)
    return pl.pallas_call(
        kern,
        grid=(nb, nc),
        in_specs=[blk(dk, fwd), blk(dk, fwd), blk(dv, fwd), blk(dk, fwd),
                  blk(dk, bwd), blk(dk, bwd), blk(dv, bwd), blk(dk, bwd)],
        out_specs=[blk(dv, fwd), blk(dv, bwd)],
        out_shape=[jax.ShapeDtypeStruct((nb, t, dv), F32)] * 2,
        scratch_shapes=[pltpu.VMEM((2, heads, dk // heads, dv // heads), F32)],
        compiler_params=_cparams(("parallel", "arbitrary")),
        name="gla_core",
    )(q, k, v, gf, q, k, v, gb)


def _gla_out_kernel(of_ref, ob_ref, r_ref, x_ref, gate_ref, ng_ref, w_ref, o_ref, *, heads):
    o = of_ref[0] + ob_ref[0]
    r = r_ref[0].astype(F32)
    dvh = o.shape[1] // heads
    parts = []
    for hd in range(heads):
        oh = o[:, hd * dvh:(hd + 1) * dvh]
        ms = jnp.mean(oh * oh, axis=-1, keepdims=True)
        parts.append(oh * lax.rsqrt(ms + EPS) * ng_ref[...])
    y = jnp.concatenate(parts, axis=1) * _silu(r)
    out = jnp.dot(y.astype(BF16), w_ref[...], preferred_element_type=F32)
    o_ref[0] = x_ref[0] + gate_ref[0] * out


def _gla_out(o_f, o_b, r, xs, mod, norm_g, w_out, heads):
    nb, t, d = xs.shape
    dv = o_f.shape[-1]
    row = lambda w: pl.BlockSpec((1, TM, w), lambda b, i: (b, i, 0))
    return pl.pallas_call(
        functools.partial(_gla_out_kernel, heads=heads),
        grid=(nb, t // TM),
        in_specs=[row(dv), row(dv), row(dv), row(d), _mod_spec(d, 2, nb, 0),
                  _resident(norm_g.shape), _resident(w_out.shape)],
        out_specs=row(d),
        out_shape=jax.ShapeDtypeStruct((nb, t, d), F32),
        compiler_params=_cparams(("parallel", "parallel")),
        name="gla_out",
    )(o_f, o_b, r, xs, mod, norm_g, w_out)


def _ffn_kernel(x_ref, xp_ref, xn_ref, sh_ref, sc_ref, gate_ref, g_ref, wa_ref, wv_ref,
                cw_ref, cb_ref, wo_ref, fg_ref, o_ref, h_ref, acc_ref,
                *, tile_off, seq_starts, seq_ends, chunks, final_norm):
    i = pl.program_id(1) + tile_off
    has_prev = jnp.logical_not(functools.reduce(jnp.logical_or, [i == s for s in seq_starts]))
    has_next = jnp.logical_not(functools.reduce(jnp.logical_or, [i == e for e in seq_ends]))
    g = g_ref[...]
    sh = sh_ref[0]
    sc = sc_ref[0]
    hs = SUBLANES
    x = x_ref[0]
    h_ref[0:hs, :] = jnp.where(has_prev, _norm_mod(xp_ref[0], g, sh, sc), 0.0).astype(BF16)
    h_ref[hs:hs + TM, :] = _norm_mod(x, g, sh, sc).astype(BF16)
    h_ref[hs + TM:, :] = jnp.where(has_next, _norm_mod(xn_ref[0], g, sh, sc), 0.0).astype(BF16)

    first = True
    for (c0, cw) in chunks:
        a = jnp.dot(h_ref[...], wa_ref[:, c0:c0 + cw], preferred_element_type=F32)
        val = jnp.dot(h_ref[hs:hs + TM, :], wv_ref[:, c0:c0 + cw], preferred_element_type=F32)
        w3 = cw_ref[:, c0:c0 + cw]
        n = a.shape[0]
        conv = (pltpu.roll(a, 1, 0) * w3[0:1] + a * w3[1:2] + pltpu.roll(a, n - 1, 0) * w3[2:3]
                )[hs:hs + TM] + cb_ref[:, c0:c0 + cw]
        act = (_silu(conv) * val).astype(BF16)
        part = jnp.dot(act, wo_ref[c0:c0 + cw, :], preferred_element_type=F32)
        if first:
            acc_ref[...] = part
            first = False
        else:
            acc_ref[...] += part
    y = x + gate_ref[0] * acc_ref[...]
    if final_norm:
        ms = jnp.mean(y * y, axis=-1, keepdims=True)
        y = y * lax.rsqrt(ms + EPS) * fg_ref[...]
    o_ref[0] = y


def _ffn(xs, mod, norm_g, w_a, w_v, conv_w, conv_b, w_out, final_g, *, lat_only, final_norm):
    nb, t, d = xs.shape
    dff = w_a.shape[1]
    nt = t // TM
    tile_off = 1 if lat_only else 0
    n_steps = nt - tile_off
    bpt = TM // SUBLANES
    nblk = t // SUBLANES
    chunks = []
    c0 = 0
    while c0 < dff:
        cw = min(512, dff - c0)
        chunks.append((c0, cw))
        c0 += cw
    kern = functools.partial(
        _ffn_kernel, tile_off=tile_off, seq_starts=(0, 1), seq_ends=(0, nt - 1),
        chunks=tuple(chunks), final_norm=final_norm)
    out_rows = n_steps * TM
    return pl.pallas_call(
        kern,
        grid=(nb, n_steps),
        in_specs=[
            pl.BlockSpec((1, TM, d), lambda b, i: (b, i + tile_off, 0)),
            pl.BlockSpec((1, SUBLANES, d),
                         lambda b, i: (b, jnp.maximum((i + tile_off) * bpt - 1, 0), 0)),
            pl.BlockSpec((1, SUBLANES, d),
                         lambda b, i: (b, jnp.minimum((i + tile_off + 1) * bpt, nblk - 1), 0)),
            _mod_spec(d, 3, nb, tile_off),
            _mod_spec(d, 4, nb, tile_off),
            _mod_spec(d, 5, nb, tile_off),
            _resident((1, d)),
            _resident(w_a.shape),
            _resident(w_v.shape),
            _resident(conv_w.shape),
            _resident(conv_b.shape),
            _resident(w_out.shape),
            _resident((1, d)),
        ],
        out_specs=pl.BlockSpec((1, TM, d), lambda b, i: (b, i, 0)),
        out_shape=jax.ShapeDtypeStruct((nb, out_rows, d), F32),
        scratch_shapes=[pltpu.VMEM((TM + 2 * SUBLANES, d), BF16), pltpu.VMEM((TM, d), F32)],
        compiler_params=_cparams(("parallel", "parallel")),
        name="ffn_final" if final_norm else "ffn",
    )(xs, xs, xs, mod, mod, mod, norm_g, w_a, w_v, conv_w, conv_b, w_out, final_g)


def _na_qkv_kernel(x_ref, sh_ref, sc_ref, g_ref, w_ref, q_ref, k_ref, v_ref, *, d, qscale):
    h = _norm_mod(x_ref[0], g_ref[...], sh_ref[0], sc_ref[0]).astype(BF16)
    proj = jnp.dot(h, w_ref[...], preferred_element_type=F32)
    q_ref[0] = (proj[:, :d] * qscale).astype(BF16)
    k_ref[0] = proj[:, d:2 * d].astype(BF16)
    v_ref[0] = proj[:, 2 * d:].astype(BF16)


def _na_qkv(xs, mod, norm_g, w_qkv, head_dim):
    nb, t, d = xs.shape
    row = lambda w: pl.BlockSpec((1, TM, w), lambda b, i: (b, i, 0))
    return pl.pallas_call(
        functools.partial(_na_qkv_kernel, d=d, qscale=float(head_dim) ** -0.5),
        grid=(nb, t // TM),
        in_specs=[row(d), _mod_spec(d, 0, nb, 0), _mod_spec(d, 1, nb, 0),
                  _resident((1, d)), _resident(w_qkv.shape)],
        out_specs=[row(d)] * 3,
        out_shape=[jax.ShapeDtypeStruct((nb, t, d), BF16)] * 3,
        compiler_params=_cparams(("parallel", "parallel")),
        name="na_qkv",
    )(xs, mod, mod, norm_g, w_qkv)


def _na_attn_kernel(q_ref, k_ref, v_ref, bias_ref, o_ref, *, n_ctx, rows_n, head_dim):
    rb = pl.program_id(2)
    nq = NA_R * GRID_W
    nwin = NA_WIN * GRID_W
    win0 = jnp.clip(rb * NA_R - NA_KH // 2, 0, rows_n - NA_WIN)
    start = pl.multiple_of(n_ctx + win0 * GRID_W, GRID_W)
    q = q_ref[0]
    lane = lax.broadcasted_iota(jnp.int32, q.shape, 1)
    qs = jnp.concatenate(
        [jnp.where((lane // head_dim) == hd, q, jnp.zeros_like(q)) for hd in range(NA_HG)], axis=0)
    kk = jnp.concatenate([k_ref[0, pl.ds(start, nwin), :], k_ref[0, 0:n_ctx, :]], axis=0)
    vv = jnp.concatenate([v_ref[0, pl.ds(start, nwin), :], v_ref[0, 0:n_ctx, :]], axis=0)
    s = lax.dot_general(qs, kk, (((1,), (1,)), ((), ())), preferred_element_type=F32)
    s = jnp.concatenate([s[:, :nwin] + bias_ref[0, 0], s[:, nwin:]], axis=1)
    m = jnp.max(s, axis=-1, keepdims=True)
    p = jnp.exp(s - m)
    l = jnp.sum(p, axis=-1, keepdims=True)
    p = (p * (1.0 / l)).astype(BF16)
    o = jnp.dot(p, vv, preferred_element_type=F32)
    out = jnp.zeros((nq, q.shape[1]), F32)
    for hd in range(NA_HG):
        out = jnp.where((lane // head_dim) == hd, o[hd * nq:(hd + 1) * nq], out)
    o_ref[0] = out.astype(BF16)


def _na_bias(rpb, rows_n):
    heads = rpb.shape[0]
    nrb = rows_n // NA_R
    variants = []
    for rb in (0, 1, nrb - 1):
        win0 = int(np.clip(rb * NA_R - NA_KH // 2, 0, rows_n - NA_WIN))
        r = rb * NA_R + np.arange(NA_R)
        rs = np.clip(r - NA_KH // 2, 0, rows_n - NA_KH)
        kr = win0 + np.arange(NA_WIN)
        row_ok = (kr[None, :] >= rs[:, None]) & (kr[None, :] < rs[:, None] + NA_KH)
        dr = kr[None, :] - r[:, None] + (NA_KH - 1)
        c = np.arange(GRID_W)
        cs = np.clip(c - NA_KW // 2, 0, GRID_W - NA_KW)
        col_ok = (c[None, :] >= cs[:, None]) & (c[None, :] < cs[:, None] + NA_KW)
        dc = c[None, :] - c[:, None] + (NA_KW - 1)
        ok = row_ok[:, None, :, None] & col_ok[None, :, None, :]
        dr_i = np.broadcast_to(np.clip(dr, 0, 2 * NA_KH - 2)[:, None, :, None], ok.shape)
        dc_i = np.broadcast_to(np.clip(dc, 0, 2 * NA_KW - 2)[None, :, None, :], ok.shape)
        variants.append((ok.reshape(NA_R * GRID_W, NA_WIN * GRID_W),
                         dr_i.reshape(NA_R * GRID_W, NA_WIN * GRID_W),
                         dc_i.reshape(NA_R * GRID_W, NA_WIN * GRID_W)))
    ok = np.stack([v[0] for v in variants])
    dr_i = np.stack([v[1] for v in variants])
    dc_i = np.stack([v[2] for v in variants])
    bias = jnp.where(ok[None], rpb[:, dr_i, dc_i], NEG)
    nq, nk = ok.shape[1:]
    bias = bias.reshape(heads // NA_HG, NA_HG, 3, nq, nk)
    return jnp.transpose(bias, (2, 0, 1, 3, 4)).reshape(3, heads // NA_HG, NA_HG * nq, nk)


def _na_attn(q, k, v, bias, n_ctx, head_dim):
    nb, t, d = q.shape
    rows_n = (t - n_ctx) // GRID_W
    nrb = rows_n // NA_R
    nq = NA_R * GRID_W
    hw = NA_HG * head_dim
    ngrp = d // hw
    ctx_tiles = n_ctx // nq
    kern = functools.partial(_na_attn_kernel, n_ctx=n_ctx, rows_n=rows_n, head_dim=head_dim)
    return pl.pallas_call(
        kern,
        grid=(nb, ngrp, nrb),
        in_specs=[
            pl.BlockSpec((1, nq, hw), lambda b, g, r: (b, r + ctx_tiles, g)),
            pl.BlockSpec((1, t, hw), lambda b, g, r: (b, 0, g)),
            pl.BlockSpec((1, t, hw), lambda b, g, r: (b, 0, g)),
            pl.BlockSpec((1, 1, NA_HG * nq, NA_WIN * GRID_W),
                         lambda b, g, r: (jnp.where(r == 0, 0, jnp.where(r == nrb - 1, 2, 1)), g, 0, 0)),
        ],
        out_specs=pl.BlockSpec((1, nq, hw), lambda b, g, r: (b, r, g)),
        out_shape=jax.ShapeDtypeStruct((nb, t - n_ctx, d), BF16),
        compiler_params=_cparams(("parallel", "parallel", "arbitrary")),
        name="na_attn",
    )(q, k, v, bias)


def _na_out_kernel(a_ref, x_ref, gate_ref, w_ref, o_ref):
    out = jnp.dot(a_ref[0], w_ref[...], preferred_element_type=F32)
    o_ref[0] = x_ref[0] + gate_ref[0] * out


def _na_out(attn, xs, mod, w_out):
    nb, t, d = xs.shape
    n = attn.shape[1]
    off = (t - n) // TM
    return pl.pallas_call(
        _na_out_kernel,
        grid=(nb, n // TM),
        in_specs=[pl.BlockSpec((1, TM, d), lambda b, i: (b, i, 0)),
                  pl.BlockSpec((1, TM, d), lambda b, i: (b, i + off, 0)),
                  _mod_spec(d, 2, nb, 1),
                  _resident(w_out.shape)],
        out_specs=pl.BlockSpec((1, TM, d), lambda b, i: (b, i + off, 0)),
        out_shape=jax.ShapeDtypeStruct((nb, t, d), F32),
        input_output_aliases={1: 0},
        compiler_params=_cparams(("parallel", "parallel")),
        name="na_out",
    )(attn, xs, mod, w_out)


def _rope_tables(n_lat, n_ctx):
    half = LANES // 2
    inv = 1.0 / (ROPE_BASE ** (jnp.arange(0, half, 2, dtype=F32) / half))
    pos = jnp.arange(n_lat)
    rows = (pos // GRID_W).astype(F32)[:, None] * inv[None, :]
    cols = (pos % GRID_W).astype(F32)[:, None] * inv[None, :]
    cos = jnp.concatenate([jnp.cos(rows)] * 2 + [jnp.cos(cols)] * 2, axis=1)
    sin = jnp.concatenate([-jnp.sin(rows), jnp.sin(rows), -jnp.sin(cols), jnp.sin(cols)], axis=1)
    cos = jnp.concatenate([jnp.ones((n_ctx, LANES), F32), cos], axis=0)
    sin = jnp.concatenate([jnp.zeros((n_ctx, LANES), F32), sin], axis=0)
    return cos, sin


def kernel(x, c, ctx, c_ctx, ada_w, ada_b, norm1_g, norm2_g, ffn_w_in, ffn_conv_w, ffn_conv_b, ffn_w_out,
           gla_w_in, gla_a_w1, gla_a_w2, gla_a_b, gla_norm_g, gla_w_out, na_w_qkv, na_rpb, na_w_out, final_g):
    nb, n, d = x.shape
    n_ctx = ctx.shape[1]
    depth = ada_w.shape[0]
    dff = ffn_conv_b.shape[-1]
    assert n_ctx == TM and n % TM == 0 and depth == 2 and nb + 1 <= SUBLANES

    c_all = jnp.zeros((SUBLANES, d), F32).at[:nb].set(c).at[nb].set(c_ctx)
    mod = _ada(c_all, ada_w, ada_b)
    xs = jnp.concatenate([ctx, x], axis=1)

    m0 = mod[0].reshape(SUBLANES, 1, 6 * d)
    dk = gla_a_w2.shape[-1]
    dv = gla_w_out.shape[1]
    aw1 = jnp.zeros((d, LANES), F32).at[:, :2 * GLA_LOW_RANK].set(
        jnp.concatenate([gla_a_w1[0, 0], gla_a_w1[0, 1]], axis=1)).astype(BF16)
    aw2 = jnp.zeros((2, LANES, dk), F32)
    aw2 = aw2.at[0, :GLA_LOW_RANK].set(gla_a_w2[0, 0]).at[1, GLA_LOW_RANK:2 * GLA_LOW_RANK].set(gla_a_w2[0, 1])
    cos_t, sin_t = _rope_tables(n, n_ctx)
    q, k, v, r, gf, gb = _gla_in(xs, m0, norm1_g[0].reshape(1, d), gla_w_in[0].astype(BF16), aw1, aw2,
                                 gla_a_b[0].reshape(2, 1, dk), cos_t, sin_t, dk, dv)
    o_f, o_b = _gla_core(q, k, v, gf, gb, GLA_HEADS, n_ctx)
    xs = _gla_out(o_f, o_b, r, xs, m0, gla_norm_g[0].reshape(1, dv // GLA_HEADS),
                  gla_w_out[0].astype(BF16), GLA_HEADS)
    w_in0 = ffn_w_in[0].astype(BF16)
    xs = _ffn(xs, m0, norm2_g[0].reshape(1, d), w_in0[:, :dff], w_in0[:, dff:], ffn_conv_w[0],
              ffn_conv_b[0].reshape(1, dff), ffn_w_out[0].astype(BF16), final_g.reshape(1, d),
              lat_only=False, final_norm=False)

    m1 = mod[1].reshape(SUBLANES, 1, 6 * d)
    head_dim = d // NA_HEADS
    qn, kn, vn = _na_qkv(xs, m1, norm1_g[1].reshape(1, d), na_w_qkv[0].astype(BF16), head_dim)
    bias = _na_bias(na_rpb[0], n // GRID_W)
    attn = _na_attn(qn, kn, vn, bias, n_ctx, head_dim)
    xs = _na_out(attn, xs, m1, na_w_out[0].astype(BF16))
    w_in1 = ffn_w_in[1].astype(BF16)
    return _ffn(xs, m1, norm2_g[1].reshape(1, d), w_in1[:, :dff], w_in1[:, dff:], ffn_conv_w[1],
                ffn_conv_b[1].reshape(1, dff), ffn_w_out[1].astype(BF16), final_g.reshape(1, d),
                lat_only=True, final_norm=True)
```

```python
import functools

import numpy as np
import jax
import jax.numpy as jnp
from jax import lax
from jax.experimental import pallas as pl
from jax.experimental.pallas import tpu as pltpu

GRID_W = 64
GLA_HEADS = 4
GLA_GATE_NORM = 16.0
GLA_LOW_RANK = 16
ROPE_BASE = 10000.0
NA_HEADS = 16
NA_KH = 8
NA_KW = 16
EPS = 1e-6

LANES = 128
SUBLANES = 8
VMEM_LIMIT = 56 * 1024 * 1024

TM = 256
GLA_C = 128
GLA_DIAG = 8
NA_RB = 16
NA_U = 4
NA_HG = 4
NEG = -1e30

F32 = jnp.float32
BF16 = jnp.bfloat16


def _cparams(sem):
    return pltpu.CompilerParams(dimension_semantics=sem, vmem_limit_bytes=VMEM_LIMIT)


def _resident(shape):
    nd = len(shape)
    return pl.BlockSpec(shape, lambda *_: (0,) * nd, pipeline_mode=pl.Buffered(1))


def _silu(x):
    return x * (1.0 / (1.0 + jnp.exp(-x)))


def _norm_mod(x, g, shift, scale):
    ms = jnp.mean(x * x, axis=-1, keepdims=True)
    y = x * lax.rsqrt(ms + EPS) * g
    return y * (1.0 + scale) + shift


def _ada_kernel(c_ref, w_ref, b_ref, o_ref):
    sc = _silu(c_ref[...])
    o_ref[0] = jnp.dot(sc, w_ref[0], preferred_element_type=F32,
                       precision=lax.Precision.HIGHEST) + b_ref[0]


def _ada(c_all, ada_w, ada_b):
    depth, d, d6 = ada_w.shape
    tn = 1536
    return pl.pallas_call(
        _ada_kernel,
        grid=(depth, d6 // tn),
        in_specs=[
            pl.BlockSpec((SUBLANES, d), lambda i, j: (0, 0)),
            pl.BlockSpec((1, d, tn), lambda i, j: (i, 0, j)),
            pl.BlockSpec((1, 1, tn), lambda i, j: (i, 0, j)),
        ],
        out_specs=pl.BlockSpec((1, SUBLANES, tn), lambda i, j: (i, 0, j)),
        out_shape=jax.ShapeDtypeStruct((depth, SUBLANES, d6), F32),
        compiler_params=_cparams(("parallel", "parallel")),
        name="ada_mod",
    )(c_all, ada_w, ada_b.reshape(depth, 1, d6))


def _mod_spec(d, seg, nb, tile_off):
    def imap(b, i):
        return (jnp.where(i + tile_off == 0, nb, b), 0, seg)
    return pl.BlockSpec((1, 1, d), imap)


def _gla_in_kernel(x_ref, sh_ref, sc_ref, g_ref, w_ref, aw1_ref, aw2_ref, ab_ref,
                   cos_ref, sin_ref, q_ref, k_ref, v_ref, r_ref, gf_ref, gb_ref, *, dk, dv):
    h = _norm_mod(x_ref[0], g_ref[...], sh_ref[0], sc_ref[0]).astype(BF16)
    proj = jnp.dot(h, w_ref[...], preferred_element_type=F32)
    cos = cos_ref[...]
    sin = sin_ref[...]
    lane = lax.broadcasted_iota(jnp.int32, cos.shape, 1)
    low = (lane & 32) == 0
    qscale = float(LANES) ** -0.5

    def rope(t):
        partner = jnp.where(low, pltpu.roll(t, LANES - 32, 1), pltpu.roll(t, 32, 1))
        return t * cos + partner * sin

    for hd in range(dk // LANES):
        sl = slice(hd * LANES, (hd + 1) * LANES)
        q_ref[0, :, sl] = rope(proj[:, sl]) * qscale
        k_ref[0, :, sl] = rope(proj[:, dk + hd * LANES: dk + (hd + 1) * LANES])
    v_ref[0] = proj[:, 2 * dk: 2 * dk + dv].astype(BF16)
    r_ref[0] = proj[:, 2 * dk + dv:].astype(BF16)

    z = jnp.dot(h, aw1_ref[...], preferred_element_type=F32)
    for d, o_ref in enumerate((gf_ref, gb_ref)):
        pre = jnp.dot(z, aw2_ref[d], preferred_element_type=F32,
                      precision=lax.Precision.HIGHEST) + ab_ref[d]
        ls = jnp.minimum(pre, 0.0) - jnp.log1p(jnp.exp(-jnp.abs(pre)))
        o_ref[0] = ls * (1.0 / GLA_GATE_NORM)


def _gla_in(xs, mod, norm_g, w_in, aw1, aw2, ab, cos_t, sin_t, dk, dv):
    nb, t, d = xs.shape
    nt = t // TM
    kern = functools.partial(_gla_in_kernel, dk=dk, dv=dv)
    row = lambda w: pl.BlockSpec((1, TM, w), lambda b, i: (b, i, 0))
    return pl.pallas_call(
        kern,
        grid=(nb, nt),
        in_specs=[
            row(d),
            _mod_spec(d, 0, nb, 0),
            _mod_spec(d, 1, nb, 0),
            _resident((1, d)),
            _resident(w_in.shape),
            _resident(aw1.shape),
            _resident(aw2.shape),
            _resident(ab.shape),
            pl.BlockSpec((TM, LANES), lambda b, i: (i, 0)),
            pl.BlockSpec((TM, LANES), lambda b, i: (i, 0)),
        ],
        out_specs=[row(dk), row(dk), row(dv), row(dv), row(dk), row(dk)],
        out_shape=[
            jax.ShapeDtypeStruct((nb, t, dk), F32),
            jax.ShapeDtypeStruct((nb, t, dk), F32),
            jax.ShapeDtypeStruct((nb, t, dv), BF16),
            jax.ShapeDtypeStruct((nb, t, dv), BF16),
            jax.ShapeDtypeStruct((nb, t, dk), F32),
            jax.ShapeDtypeStruct((nb, t, dk), F32),
        ],
        compiler_params=_cparams(("parallel", "parallel")),
        name="gla_in",
    )(xs, mod, mod, norm_g, w_in, aw1, aw2, ab, cos_t, sin_t)


def _row_bcast(b, period, offset):
    c, w = b.shape
    if period == c:
        return jnp.broadcast_to(b[offset:offset + 1, :], (c, w))
    b3 = b.reshape(c // period, period, w)
    return jnp.broadcast_to(b3[:, offset:offset + 1, :], b3.shape).reshape(c, w)


def _gla_chain(q, k, v, g, s, rev, row, same, causal):
    c = q.shape[0]
    b = g
    sh = 1
    while sh < c:
        if rev:
            b = b + jnp.where(row < c - sh, pltpu.roll(b, c - sh, 0), 0.0)
        else:
            b = b + jnp.where(row >= sh, pltpu.roll(b, sh, 0), 0.0)
        sh *= 2
    b_last = b[0:1, :] if rev else b[c - 1:c, :]
    q_in = (q * jnp.exp(b)).astype(BF16)
    k_out = (k * jnp.exp(b_last - b)).astype(BF16)

    nt = (((1,), (1,)), ((), ()))
    b_ref = _row_bcast(b, GLA_DIAG, GLA_DIAG - 1 if rev else 0)
    qd = (q * jnp.exp(b - b_ref)).astype(BF16)
    kd = (k * jnp.exp(b_ref - b)).astype(BF16)
    att = jnp.where(causal, lax.dot_general(qd, kd, nt, preferred_element_type=F32), 0.0)
    period = 2 * GLA_DIAG
    while period <= c:
        half = period // 2
        off = row & (period - 1)
        kside = (off >= half) if rev else (off < half)
        b_ref = _row_bcast(b, period, half if rev else half - 1)
        e = jnp.exp(-jnp.abs(b - b_ref))
        ql = jnp.where(kside, 0.0, q * e).astype(BF16)
        kl = jnp.where(kside, k * e, 0.0).astype(BF16)
        a_l = lax.dot_general(ql, kl, nt, preferred_element_type=F32)
        att = jnp.where(same[half], att, a_l)
        period *= 2
    lhs = jnp.concatenate([q_in, att.astype(BF16)], axis=1)
    rhs = jnp.concatenate([s.astype(BF16), v], axis=0)
    o = jnp.dot(lhs, rhs, preferred_element_type=F32)
    decay = jnp.transpose(jnp.broadcast_to(jnp.exp(b_last), (LANES, LANES)))
    decay = jnp.concatenate([decay] * (s.shape[1] // LANES), axis=1)
    kv = lax.dot_general(k_out, v, (((0,), (0,)), ((), ())), preferred_element_type=F32)
    return o, s * decay + kv


def _gla_core_kernel(qf_ref, kf_ref, vf_ref, gf_ref, qb_ref, kb_ref, vb_ref, gb_ref,
                     of_ref, ob_ref, s_ref, *, heads, dvh):
    @pl.when(pl.program_id(1) == 0)
    def _():
        s_ref[...] = jnp.zeros_like(s_ref)

    c = GLA_C
    row = lax.broadcasted_iota(jnp.int32, (c, LANES), 0)
    ti = lax.broadcasted_iota(jnp.int32, (c, c), 0)
    tj = lax.broadcasted_iota(jnp.int32, (c, c), 1)
    x = ti ^ tj
    same = {}
    half = GLA_DIAG
    while half < c:
        same[half] = x < half
        half *= 2
    causal_f = same[GLA_DIAG] & (tj <= ti)
    causal_b = same[GLA_DIAG] & (tj >= ti)

    for d, (q_ref, k_ref, v_ref, g_ref, o_ref) in enumerate((
            (qf_ref, kf_ref, vf_ref, gf_ref, of_ref),
            (qb_ref, kb_ref, vb_ref, gb_ref, ob_ref))):
        for hd in range(heads):
            sl = slice(hd * LANES, (hd + 1) * LANES)
            vsl = slice(hd * dvh, (hd + 1) * dvh)
            o, s_new = _gla_chain(q_ref[0, :, sl], k_ref[0, :, sl], v_ref[0, :, vsl],
                                  g_ref[0, :, sl], s_ref[d, hd], d == 1, row, same,
                                  causal_b if d == 1 else causal_f)
            o_ref[0, :, vsl] = o
            s_ref[d, hd] = s_new


def _gla_core(q, k, v, gf, gb, heads, n_ctx):
    nb, t, dk = q.shape
    dv = v.shape[-1]
    nc = t // GLA_C
    ncx = n_ctx // GLA_C
    fwd = lambda b, s: (b, s, 0)
    bwd = lambda b, s: (b, jnp.where(s < ncx, ncx - 1 - s, nc - 1 + ncx - s), 0)
    kern = functools.partial(_gla_core_kernel, heads=heads, dvh=dv // heads)
    blk = lambda w, m: pl.BlockSpec((1, GLA_C, w), m)
    return pl.pallas_call(
        kern,
        grid=(nb, nc),
        in_specs=[blk(dk, fwd), blk(dk, fwd), blk(dv, fwd), blk(dk, fwd),
                  blk(dk, bwd), blk(dk, bwd), blk(dv, bwd), blk(dk, bwd)],
        out_specs=[blk(dv, fwd), blk(dv, bwd)],
        out_shape=[jax.ShapeDtypeStruct((nb, t, dv), F32)] * 2,
        scratch_shapes=[pltpu.VMEM((2, heads, dk // heads, dv // heads), F32)],
        compiler_params=_cparams(("parallel", "arbitrary")),
        name="gla_core",
    )(q, k, v, gf, q, k, v, gb)


def _gla_out_kernel(of_ref, ob_ref, r_ref, x_ref, gate_ref, ng_ref, w_ref, o_ref, *, heads):
    o = of_ref[0] + ob_ref[0]
    r = r_ref[0].astype(F32)
    dvh = o.shape[1] // heads
    parts = []
    for hd in range(heads):
        oh = o[:, hd * dvh:(hd + 1) * dvh]
        ms = jnp.mean(oh * oh, axis=-1, keepdims=True)
        parts.append(oh * lax.rsqrt(ms + EPS) * ng_ref[...])
    y = jnp.concatenate(parts, axis=1) * _silu(r)
    out = jnp.dot(y.astype(BF16), w_ref[...], preferred_element_type=F32)
    o_ref[0] = x_ref[0] + gate_ref[0] * out


def _gla_out(o_f, o_b, r, xs, mod, norm_g, w_out, heads):
    nb, t, d = xs.shape
    dv = o_f.shape[-1]
    row = lambda w: pl.BlockSpec((1, TM, w), lambda b, i: (b, i, 0))
    return pl.pallas_call(
        functools.partial(_gla_out_kernel, heads=heads),
        grid=(nb, t // TM),
        in_specs=[row(dv), row(dv), row(dv), row(d), _mod_spec(d, 2, nb, 0),
                  _resident(norm_g.shape), _resident(w_out.shape)],
        out_specs=row(d),
        out_shape=jax.ShapeDtypeStruct((nb, t, d), F32),
        compiler_params=_cparams(("parallel", "parallel")),
        name="gla_out",
    )(o_f, o_b, r, xs, mod, norm_g, w_out)


def _ffn_kernel(x_ref, xp_ref, xn_ref, sh_ref, sc_ref, gate_ref, g_ref, wa_ref, wv_ref,
                cw_ref, cb_ref, wo_ref, fg_ref, o_ref, h_ref, acc_ref,
                *, tile_off, seq_starts, seq_ends, chunks, final_norm):
    i = pl.program_id(1) + tile_off
    has_prev = jnp.logical_not(functools.reduce(jnp.logical_or, [i == s for s in seq_starts]))
    has_next = jnp.logical_not(functools.reduce(jnp.logical_or, [i == e for e in seq_ends]))
    g = g_ref[...]
    sh = sh_ref[0]
    sc = sc_ref[0]
    hs = SUBLANES
    x = x_ref[0]
    h_ref[0:hs, :] = jnp.where(has_prev, _norm_mod(xp_ref[0], g, sh, sc), 0.0).astype(BF16)
    h_ref[hs:hs + TM, :] = _norm_mod(x, g, sh, sc).astype(BF16)
    h_ref[hs + TM:, :] = jnp.where(has_next, _norm_mod(xn_ref[0], g, sh, sc), 0.0).astype(BF16)

    first = True
    for (c0, cw) in chunks:
        a = jnp.dot(h_ref[...], wa_ref[:, c0:c0 + cw], preferred_element_type=F32)
        val = jnp.dot(h_ref[hs:hs + TM, :], wv_ref[:, c0:c0 + cw], preferred_element_type=F32)
        w3 = cw_ref[:, c0:c0 + cw]
        n = a.shape[0]
        conv = (pltpu.roll(a, 1, 0) * w3[0:1] + a * w3[1:2] + pltpu.roll(a, n - 1, 0) * w3[2:3]
                )[hs:hs + TM] + cb_ref[:, c0:c0 + cw]
        act = (_silu(conv) * val).astype(BF16)
        part = jnp.dot(act, wo_ref[c0:c0 + cw, :], preferred_element_type=F32)
        if first:
            acc_ref[...] = part
            first = False
        else:
            acc_ref[...] += part
    y = x + gate_ref[0] * acc_ref[...]
    if final_norm:
        ms = jnp.mean(y * y, axis=-1, keepdims=True)
        y = y * lax.rsqrt(ms + EPS) * fg_ref[...]
    o_ref[0] = y


def _ffn(xs, mod, norm_g, w_a, w_v, conv_w, conv_b, w_out, final_g, *, lat_only, final_norm):
    nb, t, d = xs.shape
    dff = w_a.shape[1]
    nt = t // TM
    tile_off = 1 if lat_only else 0
    n_steps = nt - tile_off
    bpt = TM // SUBLANES
    nblk = t // SUBLANES
    chunks = []
    c0 = 0
    while c0 < dff:
        cw = min(512, dff - c0)
        chunks.append((c0, cw))
        c0 += cw
    kern = functools.partial(
        _ffn_kernel, tile_off=tile_off, seq_starts=(0, 1), seq_ends=(0, nt - 1),
        chunks=tuple(chunks), final_norm=final_norm)
    out_rows = n_steps * TM
    return pl.pallas_call(
        kern,
        grid=(nb, n_steps),
        in_specs=[
            pl.BlockSpec((1, TM, d), lambda b, i: (b, i + tile_off, 0)),
            pl.BlockSpec((1, SUBLANES, d),
                         lambda b, i: (b, jnp.maximum((i + tile_off) * bpt - 1, 0), 0)),
            pl.BlockSpec((1, SUBLANES, d),
                         lambda b, i: (b, jnp.minimum((i + tile_off + 1) * bpt, nblk - 1), 0)),
            _mod_spec(d, 3, nb, tile_off),
            _mod_spec(d, 4, nb, tile_off),
            _mod_spec(d, 5, nb, tile_off),
            _resident((1, d)),
            _resident(w_a.shape),
            _resident(w_v.shape),
            _resident(conv_w.shape),
            _resident(conv_b.shape),
            _resident(w_out.shape),
            _resident((1, d)),
        ],
        out_specs=pl.BlockSpec((1, TM, d), lambda b, i: (b, i, 0)),
        out_shape=jax.ShapeDtypeStruct((nb, out_rows, d), F32),
        scratch_shapes=[pltpu.VMEM((TM + 2 * SUBLANES, d), BF16), pltpu.VMEM((TM, d), F32)],
        compiler_params=_cparams(("parallel", "parallel")),
        name="ffn_final" if final_norm else "ffn",
    )(xs, xs, xs, mod, mod, mod, norm_g, w_a, w_v, conv_w, conv_b, w_out, final_g)


def _na_qkv_kernel(x_ref, sh_ref, sc_ref, g_ref, w_ref, q_ref, k_ref, v_ref, *, d, qscale):
    h = _norm_mod(x_ref[0], g_ref[...], sh_ref[0], sc_ref[0]).astype(BF16)
    proj = jnp.dot(h, w_ref[...], preferred_element_type=F32)
    q_ref[0] = (proj[:, :d] * qscale).astype(BF16)
    k_ref[0] = proj[:, d:2 * d].astype(BF16)
    v_ref[0] = proj[:, 2 * d:].astype(BF16)


def _na_qkv(xs, mod, norm_g, w_qkv, head_dim):
    nb, t, d = xs.shape
    row = lambda w: pl.BlockSpec((1, TM, w), lambda b, i: (b, i, 0))
    return pl.pallas_call(
        functools.partial(_na_qkv_kernel, d=d, qscale=float(head_dim) ** -0.5),
        grid=(nb, t // TM),
        in_specs=[row(d), _mod_spec(d, 0, nb, 0), _mod_spec(d, 1, nb, 0),
                  _resident((1, d)), _resident(w_qkv.shape)],
        out_specs=[row(d)] * 3,
        out_shape=[jax.ShapeDtypeStruct((nb, t, d), BF16)] * 3,
        compiler_params=_cparams(("parallel", "parallel")),
        name="na_qkv",
    )(xs, mod, mod, norm_g, w_qkv)


def _na_attn_kernel(q_ref, k_ref, v_ref, bias_ref, o_ref, *, n_ctx, rows_n, head_dim):
    rb = pl.program_id(2)
    hw = q_ref.shape[-1]
    nwin = NA_KH * GRID_W
    nt = (((1,), (1,)), ((), ()))
    lane = lax.broadcasted_iota(jnp.int32, (GRID_W, hw), 1)
    hmask = [(lane // head_dim) == hd for hd in range(NA_HG)]

    def one_row(i):
        r = rb * NA_RB + i
        rs = jnp.clip(r - NA_KH // 2, 0, rows_n - NA_KH)
        lo = rs - r + (NA_KH - 1)
        start = pl.multiple_of(n_ctx + rs * GRID_W, GRID_W)
        qrow = pl.ds(pl.multiple_of(i * GRID_W, GRID_W), GRID_W)
        q = q_ref[0, pl.ds(pl.multiple_of(n_ctx + r * GRID_W, GRID_W), GRID_W), :]
        qs = jnp.concatenate([jnp.where(hmask[hd], q, jnp.zeros_like(q)) for hd in range(NA_HG)], axis=0)
        s_nb = lax.dot_general(qs, k_ref[0, pl.ds(start, nwin), :], nt, preferred_element_type=F32)
        s_cx = lax.dot_general(qs, k_ref[0, 0:n_ctx, :], nt, preferred_element_type=F32)
        bias = jnp.concatenate(
            [jnp.concatenate([bias_ref[0, hd, lo + 2 * m] for m in range(NA_KH // 2)], axis=1)
             for hd in range(NA_HG)], axis=0)
        s_nb = s_nb + bias
        m = jnp.maximum(jnp.max(s_nb, axis=-1, keepdims=True), jnp.max(s_cx, axis=-1, keepdims=True))
        p_nb = jnp.exp(s_nb - m)
        p_cx = jnp.exp(s_cx - m)
        l = jnp.sum(p_nb, axis=-1, keepdims=True) + jnp.sum(p_cx, axis=-1, keepdims=True)
        o = (jnp.dot(p_nb.astype(BF16), v_ref[0, pl.ds(start, nwin), :], preferred_element_type=F32)
             + jnp.dot(p_cx.astype(BF16), v_ref[0, 0:n_ctx, :], preferred_element_type=F32))
        o = o * (1.0 / l)
        out = o[0:GRID_W]
        for hd in range(1, NA_HG):
            out = jnp.where(hmask[hd], o[hd * GRID_W:(hd + 1) * GRID_W], out)
        o_ref[0, qrow, :] = out.astype(BF16)

    def body(it, carry):
        for u in range(NA_U):
            one_row(it * NA_U + u)
        return carry

    lax.fori_loop(0, NA_RB // NA_U, body, 0)


def _na_bias(rpb):
    heads, ndr, ndc = rpb.shape
    c = np.arange(GRID_W)
    cs = np.clip(c - NA_KW // 2, 0, GRID_W - NA_KW)
    onehot = np.zeros((2 * ndc, GRID_W, 2 * GRID_W), np.float32)
    mask = np.full((GRID_W, 2 * GRID_W), NEG, np.float32)
    for half in range(2):
        for cq in range(GRID_W):
            for kc in range(cs[cq], cs[cq] + NA_KW):
                onehot[half * ndc + kc - cq + NA_KW - 1, cq, half * GRID_W + kc] = 1.0
                mask[cq, half * GRID_W + kc] = 0.0
    pair = jnp.concatenate([rpb[:, :ndr - 1, :], rpb[:, 1:, :]], axis=-1)
    tiles = jnp.einsum('hdk,kcl->hdcl', pair, onehot, precision=lax.Precision.HIGHEST) + mask
    return tiles.reshape(heads // NA_HG, NA_HG, ndr - 1, GRID_W, 2 * GRID_W)


def _na_attn(q, k, v, bias, n_ctx, head_dim):
    nb, t, d = q.shape
    rows_n = (t - n_ctx) // GRID_W
    nrb = rows_n // NA_RB
    nq = NA_RB * GRID_W
    hw = NA_HG * head_dim
    ngrp = d // hw
    assert n_ctx % nq == 0 or nq % n_ctx == 0
    kern = functools.partial(_na_attn_kernel, n_ctx=n_ctx, rows_n=rows_n, head_dim=head_dim)
    return pl.pallas_call(
        kern,
        grid=(nb, ngrp, nrb),
        in_specs=[
            pl.BlockSpec((1, t, hw), lambda b, g, r: (b, 0, g)),
            pl.BlockSpec((1, t, hw), lambda b, g, r: (b, 0, g)),
            pl.BlockSpec((1, t, hw), lambda b, g, r: (b, 0, g)),
            pl.BlockSpec((1,) + bias.shape[1:], lambda b, g, r: (g, 0, 0, 0, 0)),
        ],
        out_specs=pl.BlockSpec((1, nq, hw), lambda b, g, r: (b, r, g)),
        out_shape=jax.ShapeDtypeStruct((nb, t - n_ctx, d), BF16),
        compiler_params=_cparams(("parallel", "parallel", "arbitrary")),
        name="na_attn",
    )(q, k, v, bias)


def _na_out_kernel(a_ref, x_ref, gate_ref, w_ref, o_ref):
    out = jnp.dot(a_ref[0], w_ref[...], preferred_element_type=F32)
    o_ref[0] = x_ref[0] + gate_ref[0] * out


def _na_out(attn, xs, mod, w_out):
    nb, t, d = xs.shape
    n = attn.shape[1]
    off = (t - n) // TM
    return pl.pallas_call(
        _na_out_kernel,
        grid=(nb, n // TM),
        in_specs=[pl.BlockSpec((1, TM, d), lambda b, i: (b, i, 0)),
                  pl.BlockSpec((1, TM, d), lambda b, i: (b, i + off, 0)),
                  _mod_spec(d, 2, nb, 1),
                  _resident(w_out.shape)],
        out_specs=pl.BlockSpec((1, TM, d), lambda b, i: (b, i + off, 0)),
        out_shape=jax.ShapeDtypeStruct((nb, t, d), F32),
        input_output_aliases={1: 0},
        compiler_params=_cparams(("parallel", "parallel")),
        name="na_out",
    )(attn, xs, mod, w_out)


def _rope_tables(n_lat, n_ctx):
    half = LANES // 2
    inv = 1.0 / (ROPE_BASE ** (jnp.arange(0, half, 2, dtype=F32) / half))
    pos = jnp.arange(n_lat)
    rows = (pos // GRID_W).astype(F32)[:, None] * inv[None, :]
    cols = (pos % GRID_W).astype(F32)[:, None] * inv[None, :]
    cos = jnp.concatenate([jnp.cos(rows)] * 2 + [jnp.cos(cols)] * 2, axis=1)
    sin = jnp.concatenate([-jnp.sin(rows), jnp.sin(rows), -jnp.sin(cols), jnp.sin(cols)], axis=1)
    cos = jnp.concatenate([jnp.ones((n_ctx, LANES), F32), cos], axis=0)
    sin = jnp.concatenate([jnp.zeros((n_ctx, LANES), F32), sin], axis=0)
    return cos, sin


def kernel(x, c, ctx, c_ctx, ada_w, ada_b, norm1_g, norm2_g, ffn_w_in, ffn_conv_w, ffn_conv_b, ffn_w_out,
           gla_w_in, gla_a_w1, gla_a_w2, gla_a_b, gla_norm_g, gla_w_out, na_w_qkv, na_rpb, na_w_out, final_g):
    nb, n, d = x.shape
    n_ctx = ctx.shape[1]
    depth = ada_w.shape[0]
    dff = ffn_conv_b.shape[-1]
    assert n_ctx == TM and n % TM == 0 and depth == 2 and nb + 1 <= SUBLANES

    c_all = jnp.zeros((SUBLANES, d), F32).at[:nb].set(c).at[nb].set(c_ctx)
    mod = _ada(c_all, ada_w, ada_b)
    xs = jnp.concatenate([ctx, x], axis=1)

    m0 = mod[0].reshape(SUBLANES, 1, 6 * d)
    dk = gla_a_w2.shape[-1]
    dv = gla_w_out.shape[1]
    aw1 = jnp.zeros((d, LANES), F32).at[:, :2 * GLA_LOW_RANK].set(
        jnp.concatenate([gla_a_w1[0, 0], gla_a_w1[0, 1]], axis=1)).astype(BF16)
    aw2 = jnp.zeros((2, LANES, dk), F32)
    aw2 = aw2.at[0, :GLA_LOW_RANK].set(gla_a_w2[0, 0]).at[1, GLA_LOW_RANK:2 * GLA_LOW_RANK].set(gla_a_w2[0, 1])
    cos_t, sin_t = _rope_tables(n, n_ctx)
    q, k, v, r, gf, gb = _gla_in(xs, m0, norm1_g[0].reshape(1, d), gla_w_in[0].astype(BF16), aw1, aw2,
                                 gla_a_b[0].reshape(2, 1, dk), cos_t, sin_t, dk, dv)
    o_f, o_b = _gla_core(q, k, v, gf, gb, GLA_HEADS, n_ctx)
    xs = _gla_out(o_f, o_b, r, xs, m0, gla_norm_g[0].reshape(1, dv // GLA_HEADS),
                  gla_w_out[0].astype(BF16), GLA_HEADS)
    w_in0 = ffn_w_in[0].astype(BF16)
    xs = _ffn(xs, m0, norm2_g[0].reshape(1, d), w_in0[:, :dff], w_in0[:, dff:], ffn_conv_w[0],
              ffn_conv_b[0].reshape(1, dff), ffn_w_out[0].astype(BF16), final_g.reshape(1, d),
              lat_only=False, final_norm=False)

    m1 = mod[1].reshape(SUBLANES, 1, 6 * d)
    head_dim = d // NA_HEADS
    qn, kn, vn = _na_qkv(xs, m1, norm1_g[1].reshape(1, d), na_w_qkv[0].astype(BF16), head_dim)
    bias = _na_bias(na_rpb[0])
    attn = _na_attn(qn, kn, vn, bias, n_ctx, head_dim)
    xs = _na_out(attn, xs, m1, na_w_out[0].astype(BF16))
    w_in1 = ffn_w_in[1].astype(BF16)
    return _ffn(xs, m1, norm2_g[1].reshape(1, d), w_in1[:, :dff], w_in1[:, dff:], ffn_conv_w[1],
                ffn_conv_b[1].reshape(1, dff), ffn_w_out[1].astype(BF16), final_g.reshape(1, d),
                lat_only=True, final_norm=True)
```

```python
import functools

import numpy as np
import jax
import jax.numpy as jnp
from jax import lax
from jax.experimental import pallas as pl
from jax.experimental.pallas import tpu as pltpu

GRID_W = 64
GLA_HEADS = 4
GLA_GATE_NORM = 16.0
GLA_LOW_RANK = 16
ROPE_BASE = 10000.0
NA_HEADS = 16
NA_KH = 8
NA_KW = 16
EPS = 1e-6

LANES = 128
SUBLANES = 8
VMEM_LIMIT = 56 * 1024 * 1024

TM = 256
GLA_C = 128
GLA_DIAG = 16
NA_RB = 16
NA_U = 4
NA_HG = 4
NEG = -1e30

F32 = jnp.float32
BF16 = jnp.bfloat16


def _cparams(sem):
    return pltpu.CompilerParams(dimension_semantics=sem, vmem_limit_bytes=VMEM_LIMIT)


def _resident(shape):
    nd = len(shape)
    return pl.BlockSpec(shape, lambda *_: (0,) * nd, pipeline_mode=pl.Buffered(1))


def _silu(x):
    return x * (1.0 / (1.0 + jnp.exp(-x)))


def _norm_mod(x, g, shift, scale):
    ms = jnp.mean(x * x, axis=-1, keepdims=True)
    y = x * lax.rsqrt(ms + EPS) * g
    return y * (1.0 + scale) + shift


def _ada_kernel(c_ref, w_ref, b_ref, o_ref):
    sc = _silu(c_ref[...])
    o_ref[0] = jnp.dot(sc, w_ref[0], preferred_element_type=F32,
                       precision=lax.Precision.HIGHEST) + b_ref[0]


def _ada(c_all, ada_w, ada_b):
    depth, d, d6 = ada_w.shape
    tn = 1536
    return pl.pallas_call(
        _ada_kernel,
        grid=(depth, d6 // tn),
        in_specs=[
            pl.BlockSpec((SUBLANES, d), lambda i, j: (0, 0)),
            pl.BlockSpec((1, d, tn), lambda i, j: (i, 0, j)),
            pl.BlockSpec((1, 1, tn), lambda i, j: (i, 0, j)),
        ],
        out_specs=pl.BlockSpec((1, SUBLANES, tn), lambda i, j: (i, 0, j)),
        out_shape=jax.ShapeDtypeStruct((depth, SUBLANES, d6), F32),
        compiler_params=_cparams(("parallel", "parallel")),
        name="ada_mod",
    )(c_all, ada_w, ada_b.reshape(depth, 1, d6))


def _mod_spec(d, seg, nb, tile_off):
    def imap(b, i):
        return (jnp.where(i + tile_off == 0, nb, b), 0, seg)
    return pl.BlockSpec((1, 1, d), imap)


def _gla_in_kernel(x_ref, sh_ref, sc_ref, g_ref, w_ref, aw1_ref, aw2_ref, ab_ref,
                   cos_ref, sin_ref, q_ref, k_ref, v_ref, r_ref, gf_ref, gb_ref, *, dk, dv):
    h = _norm_mod(x_ref[0], g_ref[...], sh_ref[0], sc_ref[0]).astype(BF16)
    proj = jnp.dot(h, w_ref[...], preferred_element_type=F32)
    cos = cos_ref[...]
    sin = sin_ref[...]
    lane = lax.broadcasted_iota(jnp.int32, cos.shape, 1)
    low = (lane & 32) == 0
    qscale = float(LANES) ** -0.5

    def rope(t):
        partner = jnp.where(low, pltpu.roll(t, LANES - 32, 1), pltpu.roll(t, 32, 1))
        return t * cos + partner * sin

    for hd in range(dk // LANES):
        sl = slice(hd * LANES, (hd + 1) * LANES)
        q_ref[0, :, sl] = rope(proj[:, sl]) * qscale
        k_ref[0, :, sl] = rope(proj[:, dk + hd * LANES: dk + (hd + 1) * LANES])
    v_ref[0] = proj[:, 2 * dk: 2 * dk + dv].astype(BF16)
    r_ref[0] = proj[:, 2 * dk + dv:].astype(BF16)

    z = jnp.dot(h, aw1_ref[...], preferred_element_type=F32)
    z_hi = z.astype(BF16)
    z_lo = (z - z_hi.astype(F32)).astype(BF16)
    grp = lax.broadcasted_iota(jnp.int32, z.shape, 1) // (2 * GLA_LOW_RANK)
    zc = jnp.where((grp & 1) == 0, z_hi, z_lo)
    pre = jnp.dot(zc, aw2_ref[...], preferred_element_type=F32) + ab_ref[...]
    g = (jnp.minimum(pre, 0.0) - jnp.log1p(jnp.exp(-jnp.abs(pre)))) * (1.0 / GLA_GATE_NORM)
    g_hi = g.astype(BF16)
    g_lo = (g - g_hi.astype(F32)).astype(BF16)
    ti = lax.broadcasted_iota(jnp.int32, (TM, TM), 0)
    tj = lax.broadcasted_iota(jnp.int32, (TM, TM), 1)
    same_chunk = (ti // GLA_C) == (tj // GLA_C)
    for o_ref, tri, sl in ((gf_ref, same_chunk & (tj <= ti), slice(0, dk)),
                           (gb_ref, same_chunk & (tj >= ti), slice(dk, 2 * dk))):
        tri = jnp.where(tri, 1.0, 0.0).astype(BF16)
        o_ref[0] = jnp.dot(jnp.concatenate([tri, tri], axis=1),
                           jnp.concatenate([g_hi[:, sl], g_lo[:, sl]], axis=0),
                           preferred_element_type=F32)


def _gla_in(xs, mod, norm_g, w_in, aw1, aw2, ab, cos_t, sin_t, dk, dv):
    nb, t, d = xs.shape
    nt = t // TM
    kern = functools.partial(_gla_in_kernel, dk=dk, dv=dv)
    row = lambda w: pl.BlockSpec((1, TM, w), lambda b, i: (b, i, 0))
    return pl.pallas_call(
        kern,
        grid=(nb, nt),
        in_specs=[
            row(d),
            _mod_spec(d, 0, nb, 0),
            _mod_spec(d, 1, nb, 0),
            _resident((1, d)),
            _resident(w_in.shape),
            _resident(aw1.shape),
            _resident(aw2.shape),
            _resident(ab.shape),
            pl.BlockSpec((TM, LANES), lambda b, i: (i, 0)),
            pl.BlockSpec((TM, LANES), lambda b, i: (i, 0)),
        ],
        out_specs=[row(dk), row(dk), row(dv), row(dv), row(dk), row(dk)],
        out_shape=[
            jax.ShapeDtypeStruct((nb, t, dk), F32),
            jax.ShapeDtypeStruct((nb, t, dk), F32),
            jax.ShapeDtypeStruct((nb, t, dv), BF16),
            jax.ShapeDtypeStruct((nb, t, dv), BF16),
            jax.ShapeDtypeStruct((nb, t, dk), F32),
            jax.ShapeDtypeStruct((nb, t, dk), F32),
        ],
        compiler_params=_cparams(("parallel", "parallel")),
        name="gla_in",
    )(xs, mod, mod, norm_g, w_in, aw1, aw2, ab, cos_t, sin_t)


def _row_bcast(b, period, offset):
    c, w = b.shape
    if period == c:
        return jnp.broadcast_to(b[offset:offset + 1, :], (c, w))
    b3 = b.reshape(c // period, period, w)
    return jnp.broadcast_to(b3[:, offset:offset + 1, :], b3.shape).reshape(c, w)


def _pair_scores(lhs, rhs, first):
    zero = jnp.zeros_like(rhs)
    blockdiag = jnp.concatenate([jnp.where(first, rhs, zero), jnp.where(first, zero, rhs)], axis=0)
    return lax.dot_general(lhs, blockdiag, (((1,), (1,)), ((), ())), preferred_element_type=F32)


def _gla_pair(q, k, b, rev, kside, same, causal, first):
    c = q.shape[0]
    b_last = b[0:1, :] if rev else b[c - 1:c, :]
    q_in = (q * jnp.exp(b)).astype(BF16)
    k_out = (k * jnp.exp(b_last - b)).astype(BF16)
    b_ref = _row_bcast(b, GLA_DIAG, GLA_DIAG - 1 if rev else 0)
    qd = (q * jnp.exp(b - b_ref)).astype(BF16)
    kd = (k * jnp.exp(b_ref - b)).astype(BF16)
    att = _pair_scores(qd, kd, first)
    period = 2 * GLA_DIAG
    while period <= c:
        half = period // 2
        b_ref = _row_bcast(b, period, half if rev else half - 1)
        u = (jnp.where(kside[period], k, q) * jnp.exp(-jnp.abs(b - b_ref))).astype(BF16)
        att = jnp.where(same[half], att, _pair_scores(u, u, first))
        period *= 2
    att = jnp.where(causal, att, 0.0).astype(BF16)
    return q_in, k_out, att, jnp.exp(b_last)


def _gla_state_step(q_in, att, k_out, v, decay_row, s):
    lhs = jnp.concatenate([q_in, att], axis=1)
    rhs = jnp.concatenate([s.astype(BF16), v], axis=0)
    o = jnp.dot(lhs, rhs, preferred_element_type=F32)
    decay = jnp.transpose(jnp.broadcast_to(decay_row, (LANES, LANES)))
    decay = jnp.concatenate([decay] * (s.shape[1] // LANES), axis=1)
    kv = lax.dot_general(k_out, v, (((0,), (0,)), ((), ())), preferred_element_type=F32)
    return o, s * decay + kv


def _gla_core_kernel(qf_ref, kf_ref, vf_ref, gf_ref, qb_ref, kb_ref, vb_ref, gb_ref,
                     of_ref, ob_ref, s_ref, *, heads, dvh):
    @pl.when(pl.program_id(1) == 0)
    def _():
        s_ref[...] = jnp.zeros_like(s_ref)

    c = GLA_C
    assert c == LANES and heads % 2 == 0
    row = lax.broadcasted_iota(jnp.int32, (c, 2 * LANES), 0)
    first = lax.broadcasted_iota(jnp.int32, (c, 2 * LANES), 1) < LANES
    ti = lax.broadcasted_iota(jnp.int32, (c, 2 * c), 0)
    tj = lax.broadcasted_iota(jnp.int32, (c, 2 * c), 1) & (c - 1)
    x = ti ^ tj
    same = {}
    half = GLA_DIAG
    while half < c:
        same[half] = x < half
        half *= 2

    for d, (q_ref, k_ref, v_ref, b_ref, o_ref) in enumerate((
            (qf_ref, kf_ref, vf_ref, gf_ref, of_ref),
            (qb_ref, kb_ref, vb_ref, gb_ref, ob_ref))):
        rev = d == 1
        causal = (tj >= ti) if rev else (tj <= ti)
        kside = {}
        period = 2 * GLA_DIAG
        while period <= c:
            off = row & (period - 1)
            kside[period] = (off >= period // 2) if rev else (off < period // 2)
            period *= 2
        for pr in range(heads // 2):
            psl = slice(2 * pr * LANES, (2 * pr + 2) * LANES)
            q_in, k_out, att, decay = _gla_pair(q_ref[0, :, psl], k_ref[0, :, psl], b_ref[0, :, psl],
                                                rev, kside, same, causal, first)
            for j in range(2):
                hd = 2 * pr + j
                hsl = slice(j * LANES, (j + 1) * LANES)
                vsl = slice(hd * dvh, (hd + 1) * dvh)
                o, s_new = _gla_state_step(q_in[:, hsl], att[:, j * c:(j + 1) * c], k_out[:, hsl],
                                           v_ref[0, :, vsl], decay[:, hsl], s_ref[d, hd])
                o_ref[0, :, vsl] = o
                s_ref[d, hd] = s_new


def _gla_core(q, k, v, gf, gb, heads, n_ctx):
    nb, t, dk = q.shape
    dv = v.shape[-1]
    nc = t // GLA_C
    ncx = n_ctx // GLA_C
    fwd = lambda b, s: (b, s, 0)
    bwd = lambda b, s: (b, jnp.where(s < ncx, ncx - 1 - s, nc - 1 + ncx - s), 0)
    kern = functools.partial(_gla_core_kernel, heads=heads, dvh=dv // heads)
    blk = lambda w, m: pl.BlockSpec((1, GLA_C, w), m)
    return pl.pallas_call(
        kern,
        grid=(nb, nc),
        in_specs=[blk(dk, fwd), blk(dk, fwd), blk(dv, fwd), blk(dk, fwd),
                  blk(dk, bwd), blk(dk, bwd), blk(dv, bwd), blk(dk, bwd)],
        out_specs=[blk(dv, fwd), blk(dv, bwd)],
        out_shape=[jax.ShapeDtypeStruct((nb, t, dv), F32)] * 2,
        scratch_shapes=[pltpu.VMEM((2, heads, dk // heads, dv // heads), F32)],
        compiler_params=_cparams(("parallel", "arbitrary")),
        name="gla_core",
    )(q, k, v, gf, q, k, v, gb)


def _gla_out_kernel(of_ref, ob_ref, r_ref, x_ref, gate_ref, ng_ref, w_ref, o_ref, *, heads):
    o = of_ref[0] + ob_ref[0]
    r = r_ref[0].astype(F32)
    dvh = o.shape[1] // heads
    parts = []
    for hd in range(heads):
        oh = o[:, hd * dvh:(hd + 1) * dvh]
        ms = jnp.mean(oh * oh, axis=-1, keepdims=True)
        parts.append(oh * lax.rsqrt(ms + EPS) * ng_ref[...])
    y = jnp.concatenate(parts, axis=1) * _silu(r)
    out = jnp.dot(y.astype(BF16), w_ref[...], preferred_element_type=F32)
    o_ref[0] = x_ref[0] + gate_ref[0] * out


def _gla_out(o_f, o_b, r, xs, mod, norm_g, w_out, heads):
    nb, t, d = xs.shape
    dv = o_f.shape[-1]
    row = lambda w: pl.BlockSpec((1, TM, w), lambda b, i: (b, i, 0))
    return pl.pallas_call(
        functools.partial(_gla_out_kernel, heads=heads),
        grid=(nb, t // TM),
        in_specs=[row(dv), row(dv), row(dv), row(d), _mod_spec(d, 2, nb, 0),
                  _resident(norm_g.shape), _resident(w_out.shape)],
        out_specs=row(d),
        out_shape=jax.ShapeDtypeStruct((nb, t, d), F32),
        compiler_params=_cparams(("parallel", "parallel")),
        name="gla_out",
    )(o_f, o_b, r, xs, mod, norm_g, w_out)


def _ffn_kernel(x_ref, xp_ref, xn_ref, sh_ref, sc_ref, gate_ref, g_ref, wa_ref, wv_ref,
                cw_ref, cb_ref, wo_ref, fg_ref, o_ref, h_ref, act_ref,
                *, tile_off, seq_starts, seq_ends, chunks, final_norm):
    i = pl.program_id(1) + tile_off
    has_prev = jnp.logical_not(functools.reduce(jnp.logical_or, [i == s for s in seq_starts]))
    has_next = jnp.logical_not(functools.reduce(jnp.logical_or, [i == e for e in seq_ends]))
    g = g_ref[...]
    sh = sh_ref[0]
    sc = sc_ref[0]
    x = x_ref[0]
    h_ref[0:TM, :] = _norm_mod(x, g, sh, sc).astype(BF16)
    h_ref[TM:, :] = jnp.concatenate(
        [jnp.where(has_next, _norm_mod(xn_ref[0], g, sh, sc), 0.0),
         jnp.where(has_prev, _norm_mod(xp_ref[0], g, sh, sc), 0.0)], axis=0).astype(BF16)
    n = TM + 2 * SUBLANES

    for (c0, cw) in chunks:
        a = jnp.dot(h_ref[...], wa_ref[:, c0:c0 + cw], preferred_element_type=F32)
        val = jnp.dot(h_ref[0:TM, :], wv_ref[:, c0:c0 + cw], preferred_element_type=F32)
        w3 = cw_ref[:, c0:c0 + cw]
        conv = (pltpu.roll(a, 1, 0)[0:TM] * w3[0:1] + a[0:TM] * w3[1:2]
                + pltpu.roll(a, n - 1, 0)[0:TM] * w3[2:3] + cb_ref[:, c0:c0 + cw])
        act_ref[:, c0:c0 + cw] = (_silu(conv) * val).astype(BF16)
    y = x + gate_ref[0] * jnp.dot(act_ref[...], wo_ref[...], preferred_element_type=F32)
    if final_norm:
        ms = jnp.mean(y * y, axis=-1, keepdims=True)
        y = y * lax.rsqrt(ms + EPS) * fg_ref[...]
    o_ref[0] = y


def _ffn(xs, mod, norm_g, w_a, w_v, conv_w, conv_b, w_out, final_g, *, lat_only, final_norm):
    nb, t, d = xs.shape
    dff = w_a.shape[1]
    nt = t // TM
    tile_off = 1 if lat_only else 0
    n_steps = nt - tile_off
    bpt = TM // SUBLANES
    nblk = t // SUBLANES
    chunks = []
    c0 = 0
    while c0 < dff:
        cw = min(512, dff - c0)
        chunks.append((c0, cw))
        c0 += cw
    kern = functools.partial(
        _ffn_kernel, tile_off=tile_off, seq_starts=(0, 1), seq_ends=(0, nt - 1),
        chunks=tuple(chunks), final_norm=final_norm)
    out_rows = n_steps * TM
    return pl.pallas_call(
        kern,
        grid=(nb, n_steps),
        in_specs=[
            pl.BlockSpec((1, TM, d), lambda b, i: (b, i + tile_off, 0)),
            pl.BlockSpec((1, SUBLANES, d),
                         lambda b, i: (b, jnp.maximum((i + tile_off) * bpt - 1, 0), 0)),
            pl.BlockSpec((1, SUBLANES, d),
                         lambda b, i: (b, jnp.minimum((i + tile_off + 1) * bpt, nblk - 1), 0)),
            _mod_spec(d, 3, nb, tile_off),
            _mod_spec(d, 4, nb, tile_off),
            _mod_spec(d, 5, nb, tile_off),
            _resident((1, d)),
            _resident(w_a.shape),
            _resident(w_v.shape),
            _resident(conv_w.shape),
            _resident(conv_b.shape),
            _resident(w_out.shape),
            _resident((1, d)),
        ],
        out_specs=pl.BlockSpec((1, TM, d), lambda b, i: (b, i, 0)),
        out_shape=jax.ShapeDtypeStruct((nb, out_rows, d), F32),
        scratch_shapes=[pltpu.VMEM((TM + 2 * SUBLANES, d), BF16), pltpu.VMEM((TM, dff), BF16)],
        compiler_params=_cparams(("parallel", "parallel")),
        name="ffn_final" if final_norm else "ffn",
    )(xs, xs, xs, mod, mod, mod, norm_g, w_a, w_v, conv_w, conv_b, w_out, final_g)


def _na_qkv_kernel(x_ref, sh_ref, sc_ref, g_ref, w_ref, q_ref, k_ref, v_ref, *, d, qscale):
    h = _norm_mod(x_ref[0], g_ref[...], sh_ref[0], sc_ref[0]).astype(BF16)
    proj = jnp.dot(h, w_ref[...], preferred_element_type=F32)
    q_ref[0] = (proj[:, :d] * qscale).astype(BF16)
    k_ref[0] = proj[:, d:2 * d].astype(BF16)
    v_ref[0] = proj[:, 2 * d:].astype(BF16)


def _na_qkv(xs, mod, norm_g, w_qkv, head_dim):
    nb, t, d = xs.shape
    row = lambda w: pl.BlockSpec((1, TM, w), lambda b, i: (b, i, 0))
    return pl.pallas_call(
        functools.partial(_na_qkv_kernel, d=d, qscale=float(head_dim) ** -0.5),
        grid=(nb, t // TM),
        in_specs=[row(d), _mod_spec(d, 0, nb, 0), _mod_spec(d, 1, nb, 0),
                  _resident((1, d)), _resident(w_qkv.shape)],
        out_specs=[row(d)] * 3,
        out_shape=[jax.ShapeDtypeStruct((nb, t, d), BF16)] * 3,
        compiler_params=_cparams(("parallel", "parallel")),
        name="na_qkv",
    )(xs, mod, mod, norm_g, w_qkv)


def _na_attn_kernel(q_ref, k_ref, v_ref, bias_ref, o_ref, *, n_ctx, rows_n, head_dim):
    rb = pl.program_id(2)
    hw = q_ref.shape[-1]
    nwin = NA_KH * GRID_W
    nt = (((1,), (1,)), ((), ()))
    lane = lax.broadcasted_iota(jnp.int32, (GRID_W, hw), 1)
    hmask = [(lane // head_dim) == hd for hd in range(NA_HG)]

    def one_row(i):
        r = rb * NA_RB + i
        rs = jnp.clip(r - NA_KH // 2, 0, rows_n - NA_KH)
        lo = rs - r + (NA_KH - 1)
        start = pl.multiple_of(n_ctx + rs * GRID_W, GRID_W)
        qrow = pl.ds(pl.multiple_of(i * GRID_W, GRID_W), GRID_W)
        q = q_ref[0, pl.ds(pl.multiple_of(n_ctx + r * GRID_W, GRID_W), GRID_W), :]
        qs = jnp.concatenate([jnp.where(hmask[hd], q, jnp.zeros_like(q)) for hd in range(NA_HG)], axis=0)
        s_nb = lax.dot_general(qs, k_ref[0, pl.ds(start, nwin), :], nt, preferred_element_type=F32)
        s_cx = lax.dot_general(qs, k_ref[0, 0:n_ctx, :], nt, preferred_element_type=F32)
        bias = jnp.concatenate(
            [jnp.concatenate([bias_ref[0, hd, lo + 2 * m] for m in range(NA_KH // 2)], axis=1)
             for hd in range(NA_HG)], axis=0)
        s_nb = s_nb + bias
        m = jnp.maximum(jnp.max(s_nb, axis=-1, keepdims=True), jnp.max(s_cx, axis=-1, keepdims=True))
        p_nb = jnp.exp(s_nb - m)
        p_cx = jnp.exp(s_cx - m)
        l = jnp.sum(p_nb, axis=-1, keepdims=True) + jnp.sum(p_cx, axis=-1, keepdims=True)
        o = (jnp.dot(p_nb.astype(BF16), v_ref[0, pl.ds(start, nwin), :], preferred_element_type=F32)
             + jnp.dot(p_cx.astype(BF16), v_ref[0, 0:n_ctx, :], preferred_element_type=F32))
        o = o * (1.0 / l)
        out = o[0:GRID_W]
        for hd in range(1, NA_HG):
            out = jnp.where(hmask[hd], o[hd * GRID_W:(hd + 1) * GRID_W], out)
        o_ref[0, qrow, :] = out.astype(BF16)

    def body(it, carry):
        for u in range(NA_U):
            one_row(it * NA_U + u)
        return carry

    lax.fori_loop(0, NA_RB // NA_U, body, 0)


def _na_bias(rpb):
    heads, ndr, ndc = rpb.shape
    c = np.arange(GRID_W)
    cs = np.clip(c - NA_KW // 2, 0, GRID_W - NA_KW)
    onehot = np.zeros((2 * ndc, GRID_W, 2 * GRID_W), np.float32)
    mask = np.full((GRID_W, 2 * GRID_W), NEG, np.float32)
    for half in range(2):
        for cq in range(GRID_W):
            for kc in range(cs[cq], cs[cq] + NA_KW):
                onehot[half * ndc + kc - cq + NA_KW - 1, cq, half * GRID_W + kc] = 1.0
                mask[cq, half * GRID_W + kc] = 0.0
    pair = jnp.concatenate([rpb[:, :ndr - 1, :], rpb[:, 1:, :]], axis=-1)
    tiles = jnp.einsum('hdk,kcl->hdcl', pair, onehot, precision=lax.Precision.HIGHEST) + mask
    return tiles.reshape(heads // NA_HG, NA_HG, ndr - 1, GRID_W, 2 * GRID_W)


def _na_attn(q, k, v, bias, n_ctx, head_dim):
    nb, t, d = q.shape
    rows_n = (t - n_ctx) // GRID_W
    nrb = rows_n // NA_RB
    nq = NA_RB * GRID_W
    hw = NA_HG * head_dim
    ngrp = d // hw
    assert n_ctx % nq == 0 or nq % n_ctx == 0
    kern = functools.partial(_na_attn_kernel, n_ctx=n_ctx, rows_n=rows_n, head_dim=head_dim)
    return pl.pallas_call(
        kern,
        grid=(nb, ngrp, nrb),
        in_specs=[
            pl.BlockSpec((1, t, hw), lambda b, g, r: (b, 0, g)),
            pl.BlockSpec((1, t, hw), lambda b, g, r: (b, 0, g)),
            pl.BlockSpec((1, t, hw), lambda b, g, r: (b, 0, g)),
            pl.BlockSpec((1,) + bias.shape[1:], lambda b, g, r: (g, 0, 0, 0, 0)),
        ],
        out_specs=pl.BlockSpec((1, nq, hw), lambda b, g, r: (b, r, g)),
        out_shape=jax.ShapeDtypeStruct((nb, t - n_ctx, d), BF16),
        compiler_params=_cparams(("parallel", "parallel", "arbitrary")),
        name="na_attn",
    )(q, k, v, bias)


def _na_out_kernel(a_ref, x_ref, gate_ref, w_ref, o_ref):
    out = jnp.dot(a_ref[0], w_ref[...], preferred_element_type=F32)
    o_ref[0] = x_ref[0] + gate_ref[0] * out


def _na_out(attn, xs, mod, w_out):
    nb, t, d = xs.shape
    n = attn.shape[1]
    off = (t - n) // TM
    return pl.pallas_call(
        _na_out_kernel,
        grid=(nb, n // TM),
        in_specs=[pl.BlockSpec((1, TM, d), lambda b, i: (b, i, 0)),
                  pl.BlockSpec((1, TM, d), lambda b, i: (b, i + off, 0)),
                  _mod_spec(d, 2, nb, 1),
                  _resident(w_out.shape)],
        out_specs=pl.BlockSpec((1, TM, d), lambda b, i: (b, i + off, 0)),
        out_shape=jax.ShapeDtypeStruct((nb, t, d), F32),
        input_output_aliases={1: 0},
        compiler_params=_cparams(("parallel", "parallel")),
        name="na_out",
    )(attn, xs, mod, w_out)


def _rope_tables(n_lat, n_ctx):
    half = LANES // 2
    inv = 1.0 / (ROPE_BASE ** (jnp.arange(0, half, 2, dtype=F32) / half))
    pos = jnp.arange(n_lat)
    rows = (pos // GRID_W).astype(F32)[:, None] * inv[None, :]
    cols = (pos % GRID_W).astype(F32)[:, None] * inv[None, :]
    cos = jnp.concatenate([jnp.cos(rows)] * 2 + [jnp.cos(cols)] * 2, axis=1)
    sin = jnp.concatenate([-jnp.sin(rows), jnp.sin(rows), -jnp.sin(cols), jnp.sin(cols)], axis=1)
    cos = jnp.concatenate([jnp.ones((n_ctx, LANES), F32), cos], axis=0)
    sin = jnp.concatenate([jnp.zeros((n_ctx, LANES), F32), sin], axis=0)
    return cos, sin


def kernel(x, c, ctx, c_ctx, ada_w, ada_b, norm1_g, norm2_g, ffn_w_in, ffn_conv_w, ffn_conv_b, ffn_w_out,
           gla_w_in, gla_a_w1, gla_a_w2, gla_a_b, gla_norm_g, gla_w_out, na_w_qkv, na_rpb, na_w_out, final_g):
    nb, n, d = x.shape
    n_ctx = ctx.shape[1]
    depth = ada_w.shape[0]
    dff = ffn_conv_b.shape[-1]
    assert n_ctx == TM and n % TM == 0 and depth == 2 and nb + 1 <= SUBLANES

    c_all = jnp.zeros((SUBLANES, d), F32).at[:nb].set(c).at[nb].set(c_ctx)
    mod = _ada(c_all, ada_w, ada_b)
    xs = jnp.concatenate([ctx, x], axis=1)

    m0 = mod[0].reshape(SUBLANES, 1, 6 * d)
    dk = gla_a_w2.shape[-1]
    dv = gla_w_out.shape[1]
    assert 4 * 2 * GLA_LOW_RANK == LANES
    aw1 = jnp.concatenate([gla_a_w1[0, 0], gla_a_w1[0, 1]] * 4, axis=1).astype(BF16)
    zeros = jnp.zeros((GLA_LOW_RANK, dk), F32)
    w2 = jnp.concatenate([jnp.concatenate([gla_a_w2[0, 0], zeros], axis=1),
                          jnp.concatenate([zeros, gla_a_w2[0, 1]], axis=1)], axis=0)
    w2_hi = w2.astype(BF16)
    w2_lo = (w2 - w2_hi.astype(F32)).astype(BF16)
    aw2 = jnp.concatenate([w2_hi, w2_hi, w2_lo, w2_lo], axis=0)
    cos_t, sin_t = _rope_tables(n, n_ctx)
    q, k, v, r, gf, gb = _gla_in(xs, m0, norm1_g[0].reshape(1, d), gla_w_in[0].astype(BF16), aw1, aw2,
                                 gla_a_b[0].reshape(1, 2 * dk), cos_t, sin_t, dk, dv)
    o_f, o_b = _gla_core(q, k, v, gf, gb, GLA_HEADS, n_ctx)
    xs = _gla_out(o_f, o_b, r, xs, m0, gla_norm_g[0].reshape(1, dv // GLA_HEADS),
                  gla_w_out[0].astype(BF16), GLA_HEADS)
    w_in0 = ffn_w_in[0].astype(BF16)
    xs = _ffn(xs, m0, norm2_g[0].reshape(1, d), w_in0[:, :dff], w_in0[:, dff:], ffn_conv_w[0],
              ffn_conv_b[0].reshape(1, dff), ffn_w_out[0].astype(BF16), final_g.reshape(1, d),
              lat_only=False, final_norm=False)

    m1 = mod[1].reshape(SUBLANES, 1, 6 * d)
    head_dim = d // NA_HEADS
    qn, kn, vn = _na_qkv(xs, m1, norm1_g[1].reshape(1, d), na_w_qkv[0].astype(BF16), head_dim)
    bias = _na_bias(na_rpb[0])
    attn = _na_attn(qn, kn, vn, bias, n_ctx, head_dim)
    xs = _na_out(attn, xs, m1, na_w_out[0].astype(BF16))
    w_in1 = ffn_w_in[1].astype(BF16)
    return _ffn(xs, m1, norm2_g[1].reshape(1, d), w_in1[:, :dff], w_in1[:, dff:], ffn_conv_w[1],
                ffn_conv_b[1].reshape(1, dff), ffn_w_out[1].astype(BF16), final_g.reshape(1, d),
                lat_only=True, final_norm=True)
```

```python
import functools

import numpy as np
import jax
import jax.numpy as jnp
from jax import lax
from jax.experimental import pallas as pl
from jax.experimental.pallas import tpu as pltpu

GRID_W = 64
GLA_HEADS = 4
GLA_GATE_NORM = 16.0
GLA_LOW_RANK = 16
ROPE_BASE = 10000.0
NA_HEADS = 16
NA_KH = 8
NA_KW = 16
EPS = 1e-6

LANES = 128
SUBLANES = 8
VMEM_LIMIT = 56 * 1024 * 1024

TM = 256
GLA_C = 128
GLA_DIAG = 16
NA_RB = 16
NA_U = 8
NA_HG = 4
NEG = -1e30

F32 = jnp.float32
BF16 = jnp.bfloat16


def _cparams(sem):
    return pltpu.CompilerParams(dimension_semantics=sem, vmem_limit_bytes=VMEM_LIMIT)


def _resident(shape):
    nd = len(shape)
    return pl.BlockSpec(shape, lambda *_: (0,) * nd, pipeline_mode=pl.Buffered(1))


def _silu(x):
    return x * (1.0 / (1.0 + jnp.exp(-x)))


def _norm_mod(x, g, shift, scale):
    ms = jnp.mean(x * x, axis=-1, keepdims=True)
    y = x * lax.rsqrt(ms + EPS) * g
    return y * (1.0 + scale) + shift


def _ada_kernel(c_ref, w_ref, b_ref, o_ref):
    sc = _silu(c_ref[...])
    o_ref[0] = jnp.dot(sc, w_ref[0], preferred_element_type=F32,
                       precision=lax.Precision.HIGHEST) + b_ref[0]


def _ada(c_all, ada_w, ada_b):
    depth, d, d6 = ada_w.shape
    tn = 1536
    return pl.pallas_call(
        _ada_kernel,
        grid=(depth, d6 // tn),
        in_specs=[
            pl.BlockSpec((SUBLANES, d), lambda i, j: (0, 0)),
            pl.BlockSpec((1, d, tn), lambda i, j: (i, 0, j)),
            pl.BlockSpec((1, 1, tn), lambda i, j: (i, 0, j)),
        ],
        out_specs=pl.BlockSpec((1, SUBLANES, tn), lambda i, j: (i, 0, j)),
        out_shape=jax.ShapeDtypeStruct((depth, SUBLANES, d6), F32),
        compiler_params=_cparams(("parallel", "parallel")),
        name="ada_mod",
    )(c_all, ada_w, ada_b.reshape(depth, 1, d6))


def _mod_spec(d, seg, nb, tile_off):
    def imap(b, i):
        return (jnp.where(i + tile_off == 0, nb, b), 0, seg)
    return pl.BlockSpec((1, 1, d), imap)


def _gla_in_kernel(x_ref, sh_ref, sc_ref, g_ref, w_ref, aw1_ref, aw2_ref, ab_ref,
                   cos_ref, sin_ref, q_ref, k_ref, v_ref, r_ref, gf_ref, gb_ref, *, dk, dv):
    h = _norm_mod(x_ref[0], g_ref[...], sh_ref[0], sc_ref[0]).astype(BF16)

    z = jnp.dot(h, aw1_ref[...], preferred_element_type=F32)
    z_hi = z.astype(BF16)
    z_lo = (z - z_hi.astype(F32)).astype(BF16)
    grp = lax.broadcasted_iota(jnp.int32, z.shape, 1) // (2 * GLA_LOW_RANK)
    zc = jnp.where((grp & 1) == 0, z_hi, z_lo)
    pre = jnp.dot(zc, aw2_ref[...], preferred_element_type=F32) + ab_ref[...]

    qk = jnp.dot(h, w_ref[:, :2 * dk], preferred_element_type=F32)
    g = (jnp.minimum(pre, 0.0) - jnp.log1p(jnp.exp(-jnp.abs(pre)))) * (1.0 / GLA_GATE_NORM)
    g_hi = g.astype(BF16)
    g_lo = (g - g_hi.astype(F32)).astype(BF16)
    v_ref[0] = jnp.dot(h, w_ref[:, 2 * dk: 2 * dk + dv], preferred_element_type=F32).astype(BF16)

    cos = cos_ref[...]
    sin = sin_ref[...]
    lane = lax.broadcasted_iota(jnp.int32, cos.shape, 1)
    low = (lane & 32) == 0
    qscale = float(LANES) ** -0.5

    def rope(t):
        partner = jnp.where(low, pltpu.roll(t, LANES - 32, 1), pltpu.roll(t, 32, 1))
        return t * cos + partner * sin

    for hd in range(dk // LANES):
        sl = slice(hd * LANES, (hd + 1) * LANES)
        q_ref[0, :, sl] = rope(qk[:, sl]) * qscale
        k_ref[0, :, sl] = rope(qk[:, dk + hd * LANES: dk + (hd + 1) * LANES])

    ti = lax.broadcasted_iota(jnp.int32, (GLA_C, GLA_C), 0)
    tj = lax.broadcasted_iota(jnp.int32, (GLA_C, GLA_C), 1)
    for o_ref, tri, sl in ((gf_ref, tj <= ti, slice(0, dk)), (gb_ref, tj >= ti, slice(dk, 2 * dk))):
        tri = jnp.where(tri, 1.0, 0.0).astype(BF16)
        tri2 = jnp.concatenate([tri, tri], axis=1)
        for ch in range(TM // GLA_C):
            rows = slice(ch * GLA_C, (ch + 1) * GLA_C)
            o_ref[0, rows, :] = jnp.dot(tri2, jnp.concatenate([g_hi[rows, sl], g_lo[rows, sl]], axis=0),
                                        preferred_element_type=F32)
    r_ref[0] = jnp.dot(h, w_ref[:, 2 * dk + dv:], preferred_element_type=F32).astype(BF16)


def _gla_in(xs, mod, norm_g, w_in, aw1, aw2, ab, cos_t, sin_t, dk, dv):
    nb, t, d = xs.shape
    nt = t // TM
    kern = functools.partial(_gla_in_kernel, dk=dk, dv=dv)
    row = lambda w: pl.BlockSpec((1, TM, w), lambda b, i: (b, i, 0))
    return pl.pallas_call(
        kern,
        grid=(nb, nt),
        in_specs=[
            row(d),
            _mod_spec(d, 0, nb, 0),
            _mod_spec(d, 1, nb, 0),
            _resident((1, d)),
            _resident(w_in.shape),
            _resident(aw1.shape),
            _resident(aw2.shape),
            _resident(ab.shape),
            pl.BlockSpec((TM, LANES), lambda b, i: (i, 0)),
            pl.BlockSpec((TM, LANES), lambda b, i: (i, 0)),
        ],
        out_specs=[row(dk), row(dk), row(dv), row(dv), row(dk), row(dk)],
        out_shape=[
            jax.ShapeDtypeStruct((nb, t, dk), F32),
            jax.ShapeDtypeStruct((nb, t, dk), F32),
            jax.ShapeDtypeStruct((nb, t, dv), BF16),
            jax.ShapeDtypeStruct((nb, t, dv), BF16),
            jax.ShapeDtypeStruct((nb, t, dk), F32),
            jax.ShapeDtypeStruct((nb, t, dk), F32),
        ],
        compiler_params=_cparams(("parallel", "parallel")),
        name="gla_in",
    )(xs, mod, mod, norm_g, w_in, aw1, aw2, ab, cos_t, sin_t)


def _row_bcast(b, period, offset):
    c, w = b.shape
    if period == c:
        return jnp.broadcast_to(b[offset:offset + 1, :], (c, w))
    b3 = b.reshape(c // period, period, w)
    return jnp.broadcast_to(b3[:, offset:offset + 1, :], b3.shape).reshape(c, w)


def _pair_scores(lhs, rhs, first):
    zero = jnp.zeros_like(rhs)
    blockdiag = jnp.concatenate([jnp.where(first, rhs, zero), jnp.where(first, zero, rhs)], axis=0)
    return lax.dot_general(lhs, blockdiag, (((1,), (1,)), ((), ())), preferred_element_type=F32)


def _gla_pair(q, k, b, rev, kside, same, causal, first):
    c = q.shape[0]
    b_last = b[0:1, :] if rev else b[c - 1:c, :]
    q_in = (q * jnp.exp(b)).astype(BF16)
    k_out = (k * jnp.exp(b_last - b)).astype(BF16)
    b_ref = _row_bcast(b, GLA_DIAG, GLA_DIAG - 1 if rev else 0)
    qd = (q * jnp.exp(b - b_ref)).astype(BF16)
    kd = (k * jnp.exp(b_ref - b)).astype(BF16)
    att = _pair_scores(qd, kd, first)
    period = 2 * GLA_DIAG
    while period <= c:
        half = period // 2
        b_ref = _row_bcast(b, period, half if rev else half - 1)
        u = (jnp.where(kside[period], k, q) * jnp.exp(-jnp.abs(b - b_ref))).astype(BF16)
        att = jnp.where(same[half], att, _pair_scores(u, u, first))
        period *= 2
    att = jnp.where(causal, att, 0.0).astype(BF16)
    return q_in, k_out, att, jnp.exp(b_last)


def _gla_state_step(q_in, att, k_out, v, decay_row, s):
    lhs = jnp.concatenate([q_in, att], axis=1)
    rhs = jnp.concatenate([s.astype(BF16), v], axis=0)
    o = jnp.dot(lhs, rhs, preferred_element_type=F32)
    decay = jnp.transpose(jnp.broadcast_to(decay_row, (LANES, LANES)))
    decay = jnp.concatenate([decay] * (s.shape[1] // LANES), axis=1)
    kv = lax.dot_general(k_out, v, (((0,), (0,)), ((), ())), preferred_element_type=F32)
    return o, s * decay + kv


def _gla_core_kernel(qf_ref, kf_ref, vf_ref, gf_ref, qb_ref, kb_ref, vb_ref, gb_ref,
                     of_ref, ob_ref, s_ref, *, heads, dvh):
    @pl.when(pl.program_id(1) == 0)
    def _():
        s_ref[...] = jnp.zeros_like(s_ref)

    c = GLA_C
    assert c == LANES and heads % 2 == 0
    row = lax.broadcasted_iota(jnp.int32, (c, 2 * LANES), 0)
    first = lax.broadcasted_iota(jnp.int32, (c, 2 * LANES), 1) < LANES
    ti = lax.broadcasted_iota(jnp.int32, (c, 2 * c), 0)
    tj = lax.broadcasted_iota(jnp.int32, (c, 2 * c), 1) & (c - 1)
    x = ti ^ tj
    same = {}
    half = GLA_DIAG
    while half < c:
        same[half] = x < half
        half *= 2

    for d, (q_ref, k_ref, v_ref, b_ref, o_ref) in enumerate((
            (qf_ref, kf_ref, vf_ref, gf_ref, of_ref),
            (qb_ref, kb_ref, vb_ref, gb_ref, ob_ref))):
        rev = d == 1
        causal = (tj >= ti) if rev else (tj <= ti)
        kside = {}
        period = 2 * GLA_DIAG
        while period <= c:
            off = row & (period - 1)
            kside[period] = (off >= period // 2) if rev else (off < period // 2)
            period *= 2
        for pr in range(heads // 2):
            psl = slice(2 * pr * LANES, (2 * pr + 2) * LANES)
            q_in, k_out, att, decay = _gla_pair(q_ref[0, :, psl], k_ref[0, :, psl], b_ref[0, :, psl],
                                                rev, kside, same, causal, first)
            for j in range(2):
                hd = 2 * pr + j
                hsl = slice(j * LANES, (j + 1) * LANES)
                vsl = slice(hd * dvh, (hd + 1) * dvh)
                o, s_new = _gla_state_step(q_in[:, hsl], att[:, j * c:(j + 1) * c], k_out[:, hsl],
                                           v_ref[0, :, vsl], decay[:, hsl], s_ref[d, hd])
                o_ref[0, :, vsl] = o.astype(o_ref.dtype)
                s_ref[d, hd] = s_new


def _gla_core(q, k, v, gf, gb, heads, n_ctx):
    nb, t, dk = q.shape
    dv = v.shape[-1]
    nc = t // GLA_C
    ncx = n_ctx // GLA_C
    fwd = lambda b, s: (b, s, 0)
    bwd = lambda b, s: (b, jnp.where(s < ncx, ncx - 1 - s, nc - 1 + ncx - s), 0)
    kern = functools.partial(_gla_core_kernel, heads=heads, dvh=dv // heads)
    blk = lambda w, m: pl.BlockSpec((1, GLA_C, w), m)
    return pl.pallas_call(
        kern,
        grid=(nb, nc),
        in_specs=[blk(dk, fwd), blk(dk, fwd), blk(dv, fwd), blk(dk, fwd),
                  blk(dk, bwd), blk(dk, bwd), blk(dv, bwd), blk(dk, bwd)],
        out_specs=[blk(dv, fwd), blk(dv, bwd)],
        out_shape=[jax.ShapeDtypeStruct((nb, t, dv), BF16)] * 2,
        scratch_shapes=[pltpu.VMEM((2, heads, dk // heads, dv // heads), F32)],
        compiler_params=_cparams(("parallel", "arbitrary")),
        name="gla_core",
    )(q, k, v, gf, q, k, v, gb)


def _gla_out_kernel(of_ref, ob_ref, r_ref, x_ref, gate_ref, ng_ref, w_ref, o_ref, *, heads):
    o = of_ref[0].astype(F32) + ob_ref[0].astype(F32)
    r = r_ref[0].astype(F32)
    dvh = o.shape[1] // heads
    parts = []
    for hd in range(heads):
        oh = o[:, hd * dvh:(hd + 1) * dvh]
        ms = jnp.mean(oh * oh, axis=-1, keepdims=True)
        parts.append(oh * lax.rsqrt(ms + EPS) * ng_ref[...])
    y = jnp.concatenate(parts, axis=1) * _silu(r)
    out = jnp.dot(y.astype(BF16), w_ref[...], preferred_element_type=F32)
    o_ref[0] = x_ref[0] + gate_ref[0] * out


def _gla_out(o_f, o_b, r, xs, mod, norm_g, w_out, heads):
    nb, t, d = xs.shape
    dv = o_f.shape[-1]
    row = lambda w: pl.BlockSpec((1, TM, w), lambda b, i: (b, i, 0))
    return pl.pallas_call(
        functools.partial(_gla_out_kernel, heads=heads),
        grid=(nb, t // TM),
        in_specs=[row(dv), row(dv), row(dv), row(d), _mod_spec(d, 2, nb, 0),
                  _resident(norm_g.shape), _resident(w_out.shape)],
        out_specs=row(d),
        out_shape=jax.ShapeDtypeStruct((nb, t, d), F32),
        compiler_params=_cparams(("parallel", "parallel")),
        name="gla_out",
    )(o_f, o_b, r, xs, mod, norm_g, w_out)


def _ffn_kernel(x_ref, xp_ref, xn_ref, sh_ref, sc_ref, gate_ref, g_ref, wi_ref,
                cw_ref, cb_ref, wo_ref, fg_ref, o_ref, h_ref, act_ref,
                *, tile_off, seq_starts, seq_ends, chunks, final_norm):
    i = pl.program_id(1) + tile_off
    dff = act_ref.shape[1]
    has_prev = jnp.logical_not(functools.reduce(jnp.logical_or, [i == s for s in seq_starts]))
    has_next = jnp.logical_not(functools.reduce(jnp.logical_or, [i == e for e in seq_ends]))
    g = g_ref[...]
    sh = sh_ref[0]
    sc = sc_ref[0]
    x = x_ref[0]
    h_ref[0:TM, :] = _norm_mod(x, g, sh, sc).astype(BF16)
    h_ref[TM:, :] = jnp.concatenate(
        [jnp.where(has_next, _norm_mod(xn_ref[0], g, sh, sc), 0.0),
         jnp.where(has_prev, _norm_mod(xp_ref[0], g, sh, sc), 0.0)], axis=0).astype(BF16)
    n = TM + 2 * SUBLANES

    def branches(c0, cw):
        return (jnp.dot(h_ref[...], wi_ref[:, c0:c0 + cw], preferred_element_type=F32),
                jnp.dot(h_ref[0:TM, :], wi_ref[:, dff + c0:dff + c0 + cw], preferred_element_type=F32))

    nxt = branches(*chunks[0])
    for ci, (c0, cw) in enumerate(chunks):
        a, val = nxt
        if ci + 1 < len(chunks):
            nxt = branches(*chunks[ci + 1])
        w3 = cw_ref[:, c0:c0 + cw]
        conv = (pltpu.roll(a, 1, 0)[0:TM] * w3[0:1] + a[0:TM] * w3[1:2]
                + pltpu.roll(a, n - 1, 0)[0:TM] * w3[2:3] + cb_ref[:, c0:c0 + cw])
        act_ref[:, c0:c0 + cw] = (_silu(conv) * val).astype(BF16)
    y = x + gate_ref[0] * jnp.dot(act_ref[...], wo_ref[...], preferred_element_type=F32)
    if final_norm:
        ms = jnp.mean(y * y, axis=-1, keepdims=True)
        y = y * lax.rsqrt(ms + EPS) * fg_ref[...]
    o_ref[0] = y


def _ffn(xs, mod, norm_g, w_in, conv_w, conv_b, w_out, final_g, *, lat_only, final_norm):
    nb, t, d = xs.shape
    dff = w_out.shape[0]
    nt = t // TM
    tile_off = 1 if lat_only else 0
    n_steps = nt - tile_off
    bpt = TM // SUBLANES
    nblk = t // SUBLANES
    chunks = []
    c0 = 0
    while c0 < dff:
        cw = min(512, dff - c0)
        chunks.append((c0, cw))
        c0 += cw
    kern = functools.partial(
        _ffn_kernel, tile_off=tile_off, seq_starts=(0, 1), seq_ends=(0, nt - 1),
        chunks=tuple(chunks), final_norm=final_norm)
    out_rows = n_steps * TM
    return pl.pallas_call(
        kern,
        grid=(nb, n_steps),
        in_specs=[
            pl.BlockSpec((1, TM, d), lambda b, i: (b, i + tile_off, 0)),
            pl.BlockSpec((1, SUBLANES, d),
                         lambda b, i: (b, jnp.maximum((i + tile_off) * bpt - 1, 0), 0)),
            pl.BlockSpec((1, SUBLANES, d),
                         lambda b, i: (b, jnp.minimum((i + tile_off + 1) * bpt, nblk - 1), 0)),
            _mod_spec(d, 3, nb, tile_off),
            _mod_spec(d, 4, nb, tile_off),
            _mod_spec(d, 5, nb, tile_off),
            _resident((1, d)),
            _resident(w_in.shape),
            _resident(conv_w.shape),
            _resident(conv_b.shape),
            _resident(w_out.shape),
            _resident((1, d)),
        ],
        out_specs=pl.BlockSpec((1, TM, d), lambda b, i: (b, i, 0)),
        out_shape=jax.ShapeDtypeStruct((nb, out_rows, d), F32),
        scratch_shapes=[pltpu.VMEM((TM + 2 * SUBLANES, d), BF16), pltpu.VMEM((TM, dff), BF16)],
        compiler_params=_cparams(("parallel", "parallel")),
        name="ffn_final" if final_norm else "ffn",
    )(xs, xs, xs, mod, mod, mod, norm_g, w_in, conv_w, conv_b, w_out, final_g)


def _na_qkv_kernel(x_ref, sh_ref, sc_ref, g_ref, w_ref, q_ref, k_ref, v_ref, *, d, qscale):
    h = _norm_mod(x_ref[0], g_ref[...], sh_ref[0], sc_ref[0]).astype(BF16)
    proj = jnp.dot(h, w_ref[...], preferred_element_type=F32)
    q_ref[0] = (proj[:, :d] * qscale).astype(BF16)
    k_ref[0] = proj[:, d:2 * d].astype(BF16)
    v_ref[0] = proj[:, 2 * d:].astype(BF16)


def _na_qkv(xs, mod, norm_g, w_qkv, head_dim):
    nb, t, d = xs.shape
    row = lambda w: pl.BlockSpec((1, TM, w), lambda b, i: (b, i, 0))
    return pl.pallas_call(
        functools.partial(_na_qkv_kernel, d=d, qscale=float(head_dim) ** -0.5),
        grid=(nb, t // TM),
        in_specs=[row(d), _mod_spec(d, 0, nb, 0), _mod_spec(d, 1, nb, 0),
                  _resident((1, d)), _resident(w_qkv.shape)],
        out_specs=[row(d)] * 3,
        out_shape=[jax.ShapeDtypeStruct((nb, t, d), BF16)] * 3,
        compiler_params=_cparams(("parallel", "parallel")),
        name="na_qkv",
    )(xs, mod, mod, norm_g, w_qkv)


def _na_attn_kernel(q_ref, k_ref, v_ref, bias_ref, o_ref, *, n_ctx, rows_n, head_dim):
    rb = pl.program_id(2)
    hw = q_ref.shape[-1]
    nwin = NA_KH * GRID_W
    nt = (((1,), (1,)), ((), ()))
    lane = lax.broadcasted_iota(jnp.int32, (GRID_W, hw), 1)
    hmask = [(lane // head_dim) == hd for hd in range(NA_HG)]

    def window(i):
        r = rb * NA_RB + i
        rs = jnp.clip(r - NA_KH // 2, 0, rows_n - NA_KH)
        lo = rs - r + (NA_KH - 1)
        return r, lo, pl.multiple_of(n_ctx + rs * GRID_W, GRID_W)

    def scores(i):
        r, lo, start = window(i)
        q = q_ref[0, pl.ds(pl.multiple_of(n_ctx + r * GRID_W, GRID_W), GRID_W), :]
        qs = jnp.concatenate([jnp.where(hmask[hd], q, jnp.zeros_like(q)) for hd in range(NA_HG)], axis=0)
        s_nb = lax.dot_general(qs, k_ref[0, pl.ds(start, nwin), :], nt, preferred_element_type=F32)
        s_cx = lax.dot_general(qs, k_ref[0, 0:n_ctx, :], nt, preferred_element_type=F32)
        return s_nb, s_cx

    def finish(i, s_nb, s_cx):
        r, lo, start = window(i)
        bias = jnp.concatenate(
            [jnp.concatenate([bias_ref[0, hd, lo + 2 * m] for m in range(NA_KH // 2)], axis=1)
             for hd in range(NA_HG)], axis=0)
        s_nb = s_nb + bias
        m = jnp.maximum(jnp.max(s_nb, axis=-1, keepdims=True), jnp.max(s_cx, axis=-1, keepdims=True))
        p_nb = jnp.exp(s_nb - m)
        p_cx = jnp.exp(s_cx - m)
        l = jnp.sum(p_nb, axis=-1, keepdims=True) + jnp.sum(p_cx, axis=-1, keepdims=True)
        o = (jnp.dot(p_nb.astype(BF16), v_ref[0, pl.ds(start, nwin), :], preferred_element_type=F32)
             + jnp.dot(p_cx.astype(BF16), v_ref[0, 0:n_ctx, :], preferred_element_type=F32))
        o = o * (1.0 / l)
        out = o[0:GRID_W]
        for hd in range(1, NA_HG):
            out = jnp.where(hmask[hd], o[hd * GRID_W:(hd + 1) * GRID_W], out)
        o_ref[0, pl.ds(pl.multiple_of(i * GRID_W, GRID_W), GRID_W), :] = out.astype(BF16)

    def body(it, carry):
        nxt = scores(it * NA_U)
        for u in range(NA_U):
            cur = nxt
            if u + 1 < NA_U:
                nxt = scores(it * NA_U + u + 1)
            finish(it * NA_U + u, *cur)
        return carry

    lax.fori_loop(0, NA_RB // NA_U, body, 0)


def _na_bias(rpb):
    heads, ndr, ndc = rpb.shape
    c = np.arange(GRID_W)
    cs = np.clip(c - NA_KW // 2, 0, GRID_W - NA_KW)
    onehot = np.zeros((2 * ndc, GRID_W, 2 * GRID_W), np.float32)
    mask = np.full((GRID_W, 2 * GRID_W), NEG, np.float32)
    for half in range(2):
        for cq in range(GRID_W):
            for kc in range(cs[cq], cs[cq] + NA_KW):
                onehot[half * ndc + kc - cq + NA_KW - 1, cq, half * GRID_W + kc] = 1.0
                mask[cq, half * GRID_W + kc] = 0.0
    pair = jnp.concatenate([rpb[:, :ndr - 1, :], rpb[:, 1:, :]], axis=-1)
    tiles = jnp.einsum('hdk,kcl->hdcl', pair, onehot, precision=lax.Precision.HIGHEST) + mask
    return tiles.reshape(heads // NA_HG, NA_HG, ndr - 1, GRID_W, 2 * GRID_W)


def _na_attn(q, k, v, bias, n_ctx, head_dim):
    nb, t, d = q.shape
    rows_n = (t - n_ctx) // GRID_W
    nrb = rows_n // NA_RB
    nq = NA_RB * GRID_W
    hw = NA_HG * head_dim
    ngrp = d // hw
    assert n_ctx % nq == 0 or nq % n_ctx == 0
    kern = functools.partial(_na_attn_kernel, n_ctx=n_ctx, rows_n=rows_n, head_dim=head_dim)
    return pl.pallas_call(
        kern,
        grid=(nb, ngrp, nrb),
        in_specs=[
            pl.BlockSpec((1, t, hw), lambda b, g, r: (b, 0, g)),
            pl.BlockSpec((1, t, hw), lambda b, g, r: (b, 0, g)),
            pl.BlockSpec((1, t, hw), lambda b, g, r: (b, 0, g)),
            pl.BlockSpec((1,) + bias.shape[1:], lambda b, g, r: (g, 0, 0, 0, 0)),
        ],
        out_specs=pl.BlockSpec((1, nq, hw), lambda b, g, r: (b, r, g)),
        out_shape=jax.ShapeDtypeStruct((nb, t - n_ctx, d), BF16),
        compiler_params=_cparams(("parallel", "parallel", "arbitrary")),
        name="na_attn",
    )(q, k, v, bias)


def _na_out_kernel(a_ref, x_ref, gate_ref, w_ref, o_ref):
    out = jnp.dot(a_ref[0], w_ref[...], preferred_element_type=F32)
    o_ref[0] = x_ref[0] + gate_ref[0] * out


def _na_out(attn, xs, mod, w_out):
    nb, t, d = xs.shape
    n = attn.shape[1]
    off = (t - n) // TM
    return pl.pallas_call(
        _na_out_kernel,
        grid=(nb, n // TM),
        in_specs=[pl.BlockSpec((1, TM, d), lambda b, i: (b, i, 0)),
                  pl.BlockSpec((1, TM, d), lambda b, i: (b, i + off, 0)),
                  _mod_spec(d, 2, nb, 1),
                  _resident(w_out.shape)],
        out_specs=pl.BlockSpec((1, TM, d), lambda b, i: (b, i + off, 0)),
        out_shape=jax.ShapeDtypeStruct((nb, t, d), F32),
        input_output_aliases={1: 0},
        compiler_params=_cparams(("parallel", "parallel")),
        name="na_out",
    )(attn, xs, mod, w_out)


def _rope_tables(n_lat, n_ctx):
    half = LANES // 2
    inv = 1.0 / (ROPE_BASE ** (jnp.arange(0, half, 2, dtype=F32) / half))
    pos = jnp.arange(n_lat)
    rows = (pos // GRID_W).astype(F32)[:, None] * inv[None, :]
    cols = (pos % GRID_W).astype(F32)[:, None] * inv[None, :]
    cos = jnp.concatenate([jnp.cos(rows)] * 2 + [jnp.cos(cols)] * 2, axis=1)
    sin = jnp.concatenate([-jnp.sin(rows), jnp.sin(rows), -jnp.sin(cols), jnp.sin(cols)], axis=1)
    cos = jnp.concatenate([jnp.ones((n_ctx, LANES), F32), cos], axis=0)
    sin = jnp.concatenate([jnp.zeros((n_ctx, LANES), F32), sin], axis=0)
    return cos, sin


def kernel(x, c, ctx, c_ctx, ada_w, ada_b, norm1_g, norm2_g, ffn_w_in, ffn_conv_w, ffn_conv_b, ffn_w_out,
           gla_w_in, gla_a_w1, gla_a_w2, gla_a_b, gla_norm_g, gla_w_out, na_w_qkv, na_rpb, na_w_out, final_g):
    nb, n, d = x.shape
    n_ctx = ctx.shape[1]
    depth = ada_w.shape[0]
    dff = ffn_conv_b.shape[-1]
    assert n_ctx == TM and n % TM == 0 and depth == 2 and nb + 1 <= SUBLANES

    c_all = jnp.zeros((SUBLANES, d), F32).at[:nb].set(c).at[nb].set(c_ctx)
    mod = _ada(c_all, ada_w, ada_b)
    xs = jnp.concatenate([ctx, x], axis=1)

    m0 = mod[0].reshape(SUBLANES, 1, 6 * d)
    dk = gla_a_w2.shape[-1]
    dv = gla_w_out.shape[1]
    assert 4 * 2 * GLA_LOW_RANK == LANES
    aw1 = jnp.concatenate([gla_a_w1[0, 0], gla_a_w1[0, 1]] * 4, axis=1).astype(BF16)
    zeros = jnp.zeros((GLA_LOW_RANK, dk), F32)
    w2 = jnp.concatenate([jnp.concatenate([gla_a_w2[0, 0], zeros], axis=1),
                          jnp.concatenate([zeros, gla_a_w2[0, 1]], axis=1)], axis=0)
    w2_hi = w2.astype(BF16)
    w2_lo = (w2 - w2_hi.astype(F32)).astype(BF16)
    aw2 = jnp.concatenate([w2_hi, w2_hi, w2_lo, w2_lo], axis=0)
    cos_t, sin_t = _rope_tables(n, n_ctx)
    q, k, v, r, gf, gb = _gla_in(xs, m0, norm1_g[0].reshape(1, d), gla_w_in[0].astype(BF16), aw1, aw2,
                                 gla_a_b[0].reshape(1, 2 * dk), cos_t, sin_t, dk, dv)
    o_f, o_b = _gla_core(q, k, v, gf, gb, GLA_HEADS, n_ctx)
    xs = _gla_out(o_f, o_b, r, xs, m0, gla_norm_g[0].reshape(1, dv // GLA_HEADS),
                  gla_w_out[0].astype(BF16), GLA_HEADS)
    xs = _ffn(xs, m0, norm2_g[0].reshape(1, d), ffn_w_in[0].astype(BF16), ffn_conv_w[0],
              ffn_conv_b[0].reshape(1, dff), ffn_w_out[0].astype(BF16), final_g.reshape(1, d),
              lat_only=False, final_norm=False)

    m1 = mod[1].reshape(SUBLANES, 1, 6 * d)
    head_dim = d // NA_HEADS
    qn, kn, vn = _na_qkv(xs, m1, norm1_g[1].reshape(1, d), na_w_qkv[0].astype(BF16), head_dim)
    bias = _na_bias(na_rpb[0])
    attn = _na_attn(qn, kn, vn, bias, n_ctx, head_dim)
    xs = _na_out(attn, xs, m1, na_w_out[0].astype(BF16))
    return _ffn(xs, m1, norm2_g[1].reshape(1, d), ffn_w_in[1].astype(BF16), ffn_conv_w[1],
                ffn_conv_b[1].reshape(1, dff), ffn_w_out[1].astype(BF16), final_g.reshape(1, d),
                lat_only=True, final_norm=True)
```

```python
import functools

import numpy as np
import jax
import jax.numpy as jnp
from jax import lax
from jax.experimental import pallas as pl
from jax.experimental.pallas import tpu as pltpu

GRID_W = 64
GLA_HEADS = 4
GLA_GATE_NORM = 16.0
GLA_LOW_RANK = 16
ROPE_BASE = 10000.0
NA_HEADS = 16
NA_KH = 8
NA_KW = 16
EPS = 1e-6

LANES = 128
SUBLANES = 8
VMEM_LIMIT = 56 * 1024 * 1024

TM = 256
GLA_C = 128
GLA_BLK = 256
GLA_DIAG = 16
NA_RB = 16
NA_U = 8
NA_HG = 4
NEG = -1e30

F32 = jnp.float32
BF16 = jnp.bfloat16


def _cparams(sem):
    return pltpu.CompilerParams(dimension_semantics=sem, vmem_limit_bytes=VMEM_LIMIT)


def _resident(shape):
    nd = len(shape)
    return pl.BlockSpec(shape, lambda *_: (0,) * nd, pipeline_mode=pl.Buffered(1))


def _silu(x):
    return x * (1.0 / (1.0 + jnp.exp(-x)))


def _norm_mod(x, g, shift, scale):
    ms = jnp.mean(x * x, axis=-1, keepdims=True)
    y = x * lax.rsqrt(ms + EPS) * g
    return y * (1.0 + scale) + shift


def _ada_kernel(c_ref, w_ref, b_ref, o_ref):
    sc = _silu(c_ref[...])
    o_ref[0] = jnp.dot(sc, w_ref[0], preferred_element_type=F32,
                       precision=lax.Precision.HIGHEST) + b_ref[0]


def _ada(c_all, ada_w, ada_b):
    depth, d, d6 = ada_w.shape
    tn = 1536
    return pl.pallas_call(
        _ada_kernel,
        grid=(depth, d6 // tn),
        in_specs=[
            pl.BlockSpec((SUBLANES, d), lambda i, j: (0, 0)),
            pl.BlockSpec((1, d, tn), lambda i, j: (i, 0, j)),
            pl.BlockSpec((1, 1, tn), lambda i, j: (i, 0, j)),
        ],
        out_specs=pl.BlockSpec((1, SUBLANES, tn), lambda i, j: (i, 0, j)),
        out_shape=jax.ShapeDtypeStruct((depth, SUBLANES, d6), F32),
        compiler_params=_cparams(("parallel", "parallel")),
        name="ada_mod",
    )(c_all, ada_w, ada_b.reshape(depth, 1, d6))


def _mod_spec(d, seg, nb, tile_off):
    def imap(b, i):
        return (jnp.where(i + tile_off == 0, nb, b), 0, seg)
    return pl.BlockSpec((1, 1, d), imap)


def _stream_specs(d):
    return [pl.BlockSpec((1, TM, d), lambda b, i: (b, 0, 0)),
            pl.BlockSpec((1, TM, d), lambda b, i: (b, jnp.maximum(i - 1, 0), 0))]


def _stream_tile(ctx_ref, x_ref):
    return jnp.where(pl.program_id(1) == 0, ctx_ref[0], x_ref[0])


def _gla_in_kernel(c_ref, x_ref, sh_ref, sc_ref, g_ref, w_ref, aw1_ref, aw2_ref, ab_ref,
                   cos_ref, sin_ref, q_ref, k_ref, v_ref, r_ref, gf_ref, gb_ref, *, dk, dv):
    h = _norm_mod(_stream_tile(c_ref, x_ref), g_ref[...], sh_ref[0], sc_ref[0]).astype(BF16)

    z = jnp.dot(h, aw1_ref[...], preferred_element_type=F32)
    z_hi = z.astype(BF16)
    z_lo = (z - z_hi.astype(F32)).astype(BF16)
    grp = lax.broadcasted_iota(jnp.int32, z.shape, 1) // (2 * GLA_LOW_RANK)
    zc = jnp.where((grp & 1) == 0, z_hi, z_lo)
    pre = jnp.dot(zc, aw2_ref[...], preferred_element_type=F32) + ab_ref[...]

    qk = jnp.dot(h, w_ref[:, :2 * dk], preferred_element_type=F32)
    g = (jnp.minimum(pre, 0.0) - jnp.log1p(jnp.exp(-jnp.abs(pre)))) * (1.0 / GLA_GATE_NORM)
    g_hi = g.astype(BF16)
    g_lo = (g - g_hi.astype(F32)).astype(BF16)
    v_ref[0] = jnp.dot(h, w_ref[:, 2 * dk: 2 * dk + dv], preferred_element_type=F32).astype(BF16)

    cos = cos_ref[...]
    sin = sin_ref[...]
    lane = lax.broadcasted_iota(jnp.int32, cos.shape, 1)
    low = (lane & 32) == 0
    qscale = float(LANES) ** -0.5

    def rope(t):
        partner = jnp.where(low, pltpu.roll(t, LANES - 32, 1), pltpu.roll(t, 32, 1))
        return t * cos + partner * sin

    for hd in range(dk // LANES):
        sl = slice(hd * LANES, (hd + 1) * LANES)
        q_ref[0, :, sl] = rope(qk[:, sl]) * qscale
        k_ref[0, :, sl] = rope(qk[:, dk + hd * LANES: dk + (hd + 1) * LANES])

    ti = lax.broadcasted_iota(jnp.int32, (GLA_C, GLA_C), 0)
    tj = lax.broadcasted_iota(jnp.int32, (GLA_C, GLA_C), 1)
    for o_ref, tri, sl in ((gf_ref, tj <= ti, slice(0, dk)), (gb_ref, tj >= ti, slice(dk, 2 * dk))):
        tri = jnp.where(tri, 1.0, 0.0).astype(BF16)
        tri2 = jnp.concatenate([tri, tri], axis=1)
        for ch in range(TM // GLA_C):
            rows = slice(ch * GLA_C, (ch + 1) * GLA_C)
            o_ref[0, rows, :] = jnp.dot(tri2, jnp.concatenate([g_hi[rows, sl], g_lo[rows, sl]], axis=0),
                                        preferred_element_type=F32)
    r_ref[0] = jnp.dot(h, w_ref[:, 2 * dk + dv:], preferred_element_type=F32).astype(BF16)


def _gla_in(ctx, x, mod, norm_g, w_in, aw1, aw2, ab, cos_t, sin_t, dk, dv):
    nb, n, d = x.shape
    t = ctx.shape[1] + n
    nt = t // TM
    kern = functools.partial(_gla_in_kernel, dk=dk, dv=dv)
    row = lambda w: pl.BlockSpec((1, TM, w), lambda b, i: (b, i, 0))
    return pl.pallas_call(
        kern,
        grid=(nb, nt),
        in_specs=_stream_specs(d) + [
            _mod_spec(d, 0, nb, 0),
            _mod_spec(d, 1, nb, 0),
            _resident((1, d)),
            _resident(w_in.shape),
            _resident(aw1.shape),
            _resident(aw2.shape),
            _resident(ab.shape),
            pl.BlockSpec((TM, LANES), lambda b, i: (i, 0)),
            pl.BlockSpec((TM, LANES), lambda b, i: (i, 0)),
        ],
        out_specs=[row(dk), row(dk), row(dv), row(dv), row(dk), row(dk)],
        out_shape=[
            jax.ShapeDtypeStruct((nb, t, dk), F32),
            jax.ShapeDtypeStruct((nb, t, dk), F32),
            jax.ShapeDtypeStruct((nb, t, dv), BF16),
            jax.ShapeDtypeStruct((nb, t, dv), BF16),
            jax.ShapeDtypeStruct((nb, t, dk), F32),
            jax.ShapeDtypeStruct((nb, t, dk), F32),
        ],
        compiler_params=_cparams(("parallel", "parallel")),
        name="gla_in",
    )(ctx, x, mod, mod, norm_g, w_in, aw1, aw2, ab, cos_t, sin_t)


def _row_bcast(b, period, offset):
    c, w = b.shape
    if period == c:
        return jnp.broadcast_to(b[offset:offset + 1, :], (c, w))
    b3 = b.reshape(c // period, period, w)
    return jnp.broadcast_to(b3[:, offset:offset + 1, :], b3.shape).reshape(c, w)


def _pair_scores(lhs, rhs, first):
    zero = jnp.zeros_like(rhs)
    blockdiag = jnp.concatenate([jnp.where(first, rhs, zero), jnp.where(first, zero, rhs)], axis=0)
    return lax.dot_general(lhs, blockdiag, (((1,), (1,)), ((), ())), preferred_element_type=F32)


def _gla_pair(q, k, b, rev, kside, same, causal, first):
    c = q.shape[0]
    b_last = b[0:1, :] if rev else b[c - 1:c, :]
    q_in = (q * jnp.exp(b)).astype(BF16)
    k_out = (k * jnp.exp(b_last - b)).astype(BF16)
    b_ref = _row_bcast(b, GLA_DIAG, GLA_DIAG - 1 if rev else 0)
    qd = (q * jnp.exp(b - b_ref)).astype(BF16)
    kd = (k * jnp.exp(b_ref - b)).astype(BF16)
    att = _pair_scores(qd, kd, first)
    period = 2 * GLA_DIAG
    while period <= c:
        half = period // 2
        b_ref = _row_bcast(b, period, half if rev else half - 1)
        u = (jnp.where(kside[period], k, q) * jnp.exp(-jnp.abs(b - b_ref))).astype(BF16)
        att = jnp.where(same[half], att, _pair_scores(u, u, first))
        period *= 2
    att = jnp.where(causal, att, 0.0).astype(BF16)
    return q_in, k_out, att, jnp.exp(b_last)


def _gla_state_step(q_in, att, k_out, v, decay_row, s):
    lhs = jnp.concatenate([q_in, att], axis=1)
    rhs = jnp.concatenate([s.astype(BF16), v], axis=0)
    o = jnp.dot(lhs, rhs, preferred_element_type=F32)
    decay = jnp.transpose(jnp.broadcast_to(decay_row, (LANES, LANES)))
    decay = jnp.concatenate([decay] * (s.shape[1] // LANES), axis=1)
    kv = lax.dot_general(k_out, v, (((0,), (0,)), ((), ())), preferred_element_type=F32)
    return o, s * decay + kv


def _gla_core_kernel(qf_ref, kf_ref, vf_ref, gf_ref, qb_ref, kb_ref, vb_ref, gb_ref,
                     of_ref, ob_ref, s_ref, *, heads, dvh):
    @pl.when(pl.program_id(1) == 0)
    def _():
        s_ref[...] = jnp.zeros_like(s_ref)

    c = GLA_C
    assert c == LANES and heads % 2 == 0
    row = lax.broadcasted_iota(jnp.int32, (c, 2 * LANES), 0)
    first = lax.broadcasted_iota(jnp.int32, (c, 2 * LANES), 1) < LANES
    ti = lax.broadcasted_iota(jnp.int32, (c, 2 * c), 0)
    tj = lax.broadcasted_iota(jnp.int32, (c, 2 * c), 1) & (c - 1)
    x = ti ^ tj
    same = {}
    half = GLA_DIAG
    while half < c:
        same[half] = x < half
        half *= 2

    n_chunks = qf_ref.shape[1] // c
    dirs = ((qf_ref, kf_ref, vf_ref, gf_ref, of_ref), (qb_ref, kb_ref, vb_ref, gb_ref, ob_ref))
    masks = []
    for d in range(2):
        rev = d == 1
        kside = {}
        period = 2 * GLA_DIAG
        while period <= c:
            off = row & (period - 1)
            kside[period] = (off >= period // 2) if rev else (off < period // 2)
            period *= 2
        masks.append((kside, (tj >= ti) if rev else (tj <= ti)))

    for step in range(n_chunks):
        pre = {}
        for d, (q_ref, k_ref, v_ref, b_ref, o_ref) in enumerate(dirs):
            rev = d == 1
            ch = n_chunks - 1 - step if rev else step
            rows = slice(ch * c, (ch + 1) * c)
            for pr in range(heads // 2):
                psl = slice(2 * pr * LANES, (2 * pr + 2) * LANES)
                pre[d, pr] = _gla_pair(q_ref[0, rows, psl], k_ref[0, rows, psl], b_ref[0, rows, psl],
                                       rev, masks[d][0], same, masks[d][1], first)
        for d, (q_ref, k_ref, v_ref, b_ref, o_ref) in enumerate(dirs):
            ch = n_chunks - 1 - step if d == 1 else step
            rows = slice(ch * c, (ch + 1) * c)
            for pr in range(heads // 2):
                q_in, k_out, att, decay = pre[d, pr]
                for j in range(2):
                    hd = 2 * pr + j
                    hsl = slice(j * LANES, (j + 1) * LANES)
                    vsl = slice(hd * dvh, (hd + 1) * dvh)
                    o, s_new = _gla_state_step(q_in[:, hsl], att[:, j * c:(j + 1) * c], k_out[:, hsl],
                                               v_ref[0, rows, vsl], decay[:, hsl], s_ref[d, hd])
                    o_ref[0, rows, vsl] = o.astype(o_ref.dtype)
                    s_ref[d, hd] = s_new


def _gla_core(q, k, v, gf, gb, heads, n_ctx):
    nb, t, dk = q.shape
    dv = v.shape[-1]
    assert n_ctx % GLA_BLK == 0 and t % GLA_BLK == 0
    nc = t // GLA_BLK
    ncx = n_ctx // GLA_BLK
    fwd = lambda b, s: (b, s, 0)
    bwd = lambda b, s: (b, jnp.where(s < ncx, ncx - 1 - s, nc - 1 + ncx - s), 0)
    kern = functools.partial(_gla_core_kernel, heads=heads, dvh=dv // heads)
    blk = lambda w, m: pl.BlockSpec((1, GLA_BLK, w), m)
    return pl.pallas_call(
        kern,
        grid=(nb, nc),
        in_specs=[blk(dk, fwd), blk(dk, fwd), blk(dv, fwd), blk(dk, fwd),
                  blk(dk, bwd), blk(dk, bwd), blk(dv, bwd), blk(dk, bwd)],
        out_specs=[blk(dv, fwd), blk(dv, bwd)],
        out_shape=[jax.ShapeDtypeStruct((nb, t, dv), BF16)] * 2,
        scratch_shapes=[pltpu.VMEM((2, heads, dk // heads, dv // heads), F32)],
        compiler_params=_cparams(("parallel", "arbitrary")),
        name="gla_core",
    )(q, k, v, gf, q, k, v, gb)


def _gla_out_kernel(of_ref, ob_ref, r_ref, c_ref, x_ref, gate_ref, ng_ref, w_ref, o_ref, *, heads):
    o = of_ref[0].astype(F32) + ob_ref[0].astype(F32)
    r = r_ref[0].astype(F32)
    dvh = o.shape[1] // heads
    parts = []
    for hd in range(heads):
        oh = o[:, hd * dvh:(hd + 1) * dvh]
        ms = jnp.mean(oh * oh, axis=-1, keepdims=True)
        parts.append(oh * lax.rsqrt(ms + EPS) * ng_ref[...])
    y = jnp.concatenate(parts, axis=1) * _silu(r)
    out = jnp.dot(y.astype(BF16), w_ref[...], preferred_element_type=F32)
    o_ref[0] = _stream_tile(c_ref, x_ref) + gate_ref[0] * out


def _gla_out(o_f, o_b, r, ctx, x, mod, norm_g, w_out, heads):
    nb, t, dv = o_f.shape
    d = x.shape[-1]
    row = lambda w: pl.BlockSpec((1, TM, w), lambda b, i: (b, i, 0))
    return pl.pallas_call(
        functools.partial(_gla_out_kernel, heads=heads),
        grid=(nb, t // TM),
        in_specs=[row(dv), row(dv), row(dv)] + _stream_specs(d) + [
            _mod_spec(d, 2, nb, 0), _resident(norm_g.shape), _resident(w_out.shape)],
        out_specs=row(d),
        out_shape=jax.ShapeDtypeStruct((nb, t, d), F32),
        compiler_params=_cparams(("parallel", "parallel")),
        name="gla_out",
    )(o_f, o_b, r, ctx, x, mod, norm_g, w_out)


def _ffn_kernel(x_ref, xp_ref, xn_ref, sh_ref, sc_ref, gate_ref, g_ref, wi_ref,
                cw_ref, cb_ref, wo_ref, fg_ref, o_ref, h_ref, act_ref,
                *, tile_off, seq_starts, seq_ends, chunks, final_norm):
    i = pl.program_id(1) + tile_off
    dff = act_ref.shape[1]
    has_prev = jnp.logical_not(functools.reduce(jnp.logical_or, [i == s for s in seq_starts]))
    has_next = jnp.logical_not(functools.reduce(jnp.logical_or, [i == e for e in seq_ends]))
    g = g_ref[...]
    sh = sh_ref[0]
    sc = sc_ref[0]
    x = x_ref[0]
    h_ref[0:TM, :] = _norm_mod(x, g, sh, sc).astype(BF16)
    h_ref[TM:, :] = jnp.concatenate(
        [jnp.where(has_next, _norm_mod(xn_ref[0], g, sh, sc), 0.0),
         jnp.where(has_prev, _norm_mod(xp_ref[0], g, sh, sc), 0.0)], axis=0).astype(BF16)
    n = TM + 2 * SUBLANES

    for (c0, cw) in chunks:
        a = jnp.dot(h_ref[...], wi_ref[0, :, c0:c0 + cw], preferred_element_type=F32)
        val = jnp.dot(h_ref[0:TM, :], wi_ref[0, :, dff + c0:dff + c0 + cw], preferred_element_type=F32)
        w3 = cw_ref[:, c0:c0 + cw]
        conv = (pltpu.roll(a, 1, 0)[0:TM] * w3[0:1] + a[0:TM] * w3[1:2]
                + pltpu.roll(a, n - 1, 0)[0:TM] * w3[2:3] + cb_ref[:, c0:c0 + cw])
        act_ref[:, c0:c0 + cw] = (_silu(conv) * val).astype(BF16)
    y = x + gate_ref[0] * jnp.dot(act_ref[...], wo_ref[0], preferred_element_type=F32)
    if final_norm:
        ms = jnp.mean(y * y, axis=-1, keepdims=True)
        y = y * lax.rsqrt(ms + EPS) * fg_ref[...]
    o_ref[0] = y


def _ffn(xs, mod, norm_g, w_in, conv_w, conv_b, w_out, final_g, *, layer, lat_only, final_norm):
    nb, t, d = xs.shape
    dff = w_out.shape[1]
    layer_block = lambda w: pl.BlockSpec((1,) + w.shape[1:], lambda *_: (layer, 0, 0),
                                         pipeline_mode=pl.Buffered(1))
    nt = t // TM
    tile_off = 1 if lat_only else 0
    n_steps = nt - tile_off
    bpt = TM // SUBLANES
    nblk = t // SUBLANES
    chunks = []
    c0 = 0
    while c0 < dff:
        cw = min(512, dff - c0)
        chunks.append((c0, cw))
        c0 += cw
    kern = functools.partial(
        _ffn_kernel, tile_off=tile_off, seq_starts=(0, 1), seq_ends=(0, nt - 1),
        chunks=tuple(chunks), final_norm=final_norm)
    out_rows = n_steps * TM
    return pl.pallas_call(
        kern,
        grid=(nb, n_steps),
        in_specs=[
            pl.BlockSpec((1, TM, d), lambda b, i: (b, i + tile_off, 0)),
            pl.BlockSpec((1, SUBLANES, d),
                         lambda b, i: (b, jnp.maximum((i + tile_off) * bpt - 1, 0), 0)),
            pl.BlockSpec((1, SUBLANES, d),
                         lambda b, i: (b, jnp.minimum((i + tile_off + 1) * bpt, nblk - 1), 0)),
            _mod_spec(d, 3, nb, tile_off),
            _mod_spec(d, 4, nb, tile_off),
            _mod_spec(d, 5, nb, tile_off),
            _resident((1, d)),
            layer_block(w_in),
            _resident(conv_w.shape),
            _resident(conv_b.shape),
            layer_block(w_out),
            _resident((1, d)),
        ],
        out_specs=pl.BlockSpec((1, TM, d), lambda b, i: (b, i, 0)),
        out_shape=jax.ShapeDtypeStruct((nb, out_rows, d), F32),
        scratch_shapes=[pltpu.VMEM((TM + 2 * SUBLANES, d), BF16), pltpu.VMEM((TM, dff), BF16)],
        compiler_params=_cparams(("parallel", "parallel")),
        name="ffn_final" if final_norm else "ffn",
    )(xs, xs, xs, mod, mod, mod, norm_g, w_in, conv_w, conv_b, w_out, final_g)


def _na_qkv_kernel(x_ref, sh_ref, sc_ref, g_ref, w_ref, q_ref, k_ref, v_ref, *, d, qscale):
    h = _norm_mod(x_ref[0], g_ref[...], sh_ref[0], sc_ref[0]).astype(BF16)
    proj = jnp.dot(h, w_ref[...], preferred_element_type=F32)
    q_ref[0] = (proj[:, :d] * qscale).astype(BF16)
    k_ref[0] = proj[:, d:2 * d].astype(BF16)
    v_ref[0] = proj[:, 2 * d:].astype(BF16)


def _na_qkv(xs, mod, norm_g, w_qkv, head_dim):
    nb, t, d = xs.shape
    row = lambda w: pl.BlockSpec((1, TM, w), lambda b, i: (b, i, 0))
    return pl.pallas_call(
        functools.partial(_na_qkv_kernel, d=d, qscale=float(head_dim) ** -0.5),
        grid=(nb, t // TM),
        in_specs=[row(d), _mod_spec(d, 0, nb, 0), _mod_spec(d, 1, nb, 0),
                  _resident((1, d)), _resident(w_qkv.shape)],
        out_specs=[row(d)] * 3,
        out_shape=[jax.ShapeDtypeStruct((nb, t, d), BF16)] * 3,
        compiler_params=_cparams(("parallel", "parallel")),
        name="na_qkv",
    )(xs, mod, mod, norm_g, w_qkv)


def _na_attn_kernel(q_ref, k_ref, v_ref, bias_ref, o_ref, *, n_ctx, rows_n, head_dim):
    rb = pl.program_id(2)
    hw = q_ref.shape[-1]
    nwin = NA_KH * GRID_W
    nt = (((1,), (1,)), ((), ()))
    lane = lax.broadcasted_iota(jnp.int32, (GRID_W, hw), 1)
    hmask = [(lane // head_dim) == hd for hd in range(NA_HG)]

    def window(i):
        r = rb * NA_RB + i
        rs = jnp.clip(r - NA_KH // 2, 0, rows_n - NA_KH)
        lo = rs - r + (NA_KH - 1)
        return r, lo, pl.multiple_of(n_ctx + rs * GRID_W, GRID_W)

    def scores(i):
        r, lo, start = window(i)
        q = q_ref[0, pl.ds(pl.multiple_of(n_ctx + r * GRID_W, GRID_W), GRID_W), :]
        qs = jnp.concatenate([jnp.where(hmask[hd], q, jnp.zeros_like(q)) for hd in range(NA_HG)], axis=0)
        s_nb = lax.dot_general(qs, k_ref[0, pl.ds(start, nwin), :], nt, preferred_element_type=F32)
        s_cx = lax.dot_general(qs, k_ref[0, 0:n_ctx, :], nt, preferred_element_type=F32)
        return s_nb, s_cx

    def finish(i, s_nb, s_cx):
        r, lo, start = window(i)
        bias = jnp.concatenate(
            [jnp.concatenate([bias_ref[0, hd, lo + 2 * m] for m in range(NA_KH // 2)], axis=1)
             for hd in range(NA_HG)], axis=0)
        s_nb = s_nb + bias
        m = jnp.maximum(jnp.max(s_nb, axis=-1, keepdims=True), jnp.max(s_cx, axis=-1, keepdims=True))
        p_nb = jnp.exp(s_nb - m)
        p_cx = jnp.exp(s_cx - m)
        l = jnp.sum(p_nb, axis=-1, keepdims=True) + jnp.sum(p_cx, axis=-1, keepdims=True)
        o = (jnp.dot(p_nb.astype(BF16), v_ref[0, pl.ds(start, nwin), :], preferred_element_type=F32)
             + jnp.dot(p_cx.astype(BF16), v_ref[0, 0:n_ctx, :], preferred_element_type=F32))
        o = o * (1.0 / l)
        out = o[0:GRID_W]
        for hd in range(1, NA_HG):
            out = jnp.where(hmask[hd], o[hd * GRID_W:(hd + 1) * GRID_W], out)
        o_ref[0, pl.ds(pl.multiple_of(i * GRID_W, GRID_W), GRID_W), :] = out.astype(BF16)

    def body(it, carry):
        nxt = scores(it * NA_U)
        for u in range(NA_U):
            cur = nxt
            if u + 1 < NA_U:
                nxt = scores(it * NA_U + u + 1)
            finish(it * NA_U + u, *cur)
        return carry

    lax.fori_loop(0, NA_RB // NA_U, body, 0)


def _na_bias(rpb):
    heads, ndr, ndc = rpb.shape
    c = np.arange(GRID_W)
    cs = np.clip(c - NA_KW // 2, 0, GRID_W - NA_KW)
    onehot = np.zeros((2 * ndc, GRID_W, 2 * GRID_W), np.float32)
    mask = np.full((GRID_W, 2 * GRID_W), NEG, np.float32)
    for half in range(2):
        for cq in range(GRID_W):
            for kc in range(cs[cq], cs[cq] + NA_KW):
                onehot[half * ndc + kc - cq + NA_KW - 1, cq, half * GRID_W + kc] = 1.0
                mask[cq, half * GRID_W + kc] = 0.0
    pair = jnp.concatenate([rpb[:, :ndr - 1, :], rpb[:, 1:, :]], axis=-1)
    tiles = jnp.einsum('hdk,kcl->hdcl', pair, onehot, precision=lax.Precision.HIGHEST) + mask
    return tiles.reshape(heads // NA_HG, NA_HG, ndr - 1, GRID_W, 2 * GRID_W)


def _na_attn(q, k, v, bias, n_ctx, head_dim):
    nb, t, d = q.shape
    rows_n = (t - n_ctx) // GRID_W
    nrb = rows_n // NA_RB
    nq = NA_RB * GRID_W
    hw = NA_HG * head_dim
    ngrp = d // hw
    assert n_ctx % nq == 0 or nq % n_ctx == 0
    kern = functools.partial(_na_attn_kernel, n_ctx=n_ctx, rows_n=rows_n, head_dim=head_dim)
    return pl.pallas_call(
        kern,
        grid=(nb, ngrp, nrb),
        in_specs=[
            pl.BlockSpec((1, t, hw), lambda b, g, r: (b, 0, g)),
            pl.BlockSpec((1, t, hw), lambda b, g, r: (b, 0, g)),
            pl.BlockSpec((1, t, hw), lambda b, g, r: (b, 0, g)),
            pl.BlockSpec((1,) + bias.shape[1:], lambda b, g, r: (g, 0, 0, 0, 0)),
        ],
        out_specs=pl.BlockSpec((1, nq, hw), lambda b, g, r: (b, r, g)),
        out_shape=jax.ShapeDtypeStruct((nb, t - n_ctx, d), BF16),
        compiler_params=_cparams(("parallel", "parallel", "arbitrary")),
        name="na_attn",
    )(q, k, v, bias)


def _na_out_kernel(a_ref, x_ref, gate_ref, w_ref, o_ref):
    out = jnp.dot(a_ref[0], w_ref[...], preferred_element_type=F32)
    o_ref[0] = x_ref[0] + gate_ref[0] * out


def _na_out(attn, xs, mod, w_out):
    nb, t, d = xs.shape
    n = attn.shape[1]
    off = (t - n) // TM
    return pl.pallas_call(
        _na_out_kernel,
        grid=(nb, n // TM),
        in_specs=[pl.BlockSpec((1, TM, d), lambda b, i: (b, i, 0)),
                  pl.BlockSpec((1, TM, d), lambda b, i: (b, i + off, 0)),
                  _mod_spec(d, 2, nb, 1),
                  _resident(w_out.shape)],
        out_specs=pl.BlockSpec((1, TM, d), lambda b, i: (b, i + off, 0)),
        out_shape=jax.ShapeDtypeStruct((nb, t, d), F32),
        input_output_aliases={1: 0},
        compiler_params=_cparams(("parallel", "parallel")),
        name="na_out",
    )(attn, xs, mod, w_out)


def _rope_tables(n_lat, n_ctx):
    half = LANES // 2
    inv = 1.0 / (ROPE_BASE ** (np.arange(0, half, 2, dtype=np.float64) / half))
    pos = np.arange(n_lat)
    rows = (pos // GRID_W).astype(np.float64)[:, None] * inv[None, :]
    cols = (pos % GRID_W).astype(np.float64)[:, None] * inv[None, :]
    cos = np.concatenate([np.cos(rows)] * 2 + [np.cos(cols)] * 2, axis=1)
    sin = np.concatenate([-np.sin(rows), np.sin(rows), -np.sin(cols), np.sin(cols)], axis=1)
    cos = np.concatenate([np.ones((n_ctx, LANES)), cos], axis=0).astype(np.float32)
    sin = np.concatenate([np.zeros((n_ctx, LANES)), sin], axis=0).astype(np.float32)
    return jnp.asarray(cos), jnp.asarray(sin)


def kernel(x, c, ctx, c_ctx, ada_w, ada_b, norm1_g, norm2_g, ffn_w_in, ffn_conv_w, ffn_conv_b, ffn_w_out,
           gla_w_in, gla_a_w1, gla_a_w2, gla_a_b, gla_norm_g, gla_w_out, na_w_qkv, na_rpb, na_w_out, final_g):
    nb, n, d = x.shape
    n_ctx = ctx.shape[1]
    depth = ada_w.shape[0]
    dff = ffn_conv_b.shape[-1]
    assert n_ctx == TM and n % TM == 0 and depth == 2 and nb + 1 <= SUBLANES

    c_all = jnp.zeros((SUBLANES, d), F32).at[:nb].set(c).at[nb].set(c_ctx)
    mod = _ada(c_all, ada_w, ada_b)

    m0 = mod[0].reshape(SUBLANES, 1, 6 * d)
    dk = gla_a_w2.shape[-1]
    dv = gla_w_out.shape[1]
    assert 4 * 2 * GLA_LOW_RANK == LANES
    aw1 = jnp.concatenate([gla_a_w1[0, 0], gla_a_w1[0, 1]] * 4, axis=1).astype(BF16)
    zeros = jnp.zeros((GLA_LOW_RANK, dk), F32)
    w2 = jnp.concatenate([jnp.concatenate([gla_a_w2[0, 0], zeros], axis=1),
                          jnp.concatenate([zeros, gla_a_w2[0, 1]], axis=1)], axis=0)
    w2_hi = w2.astype(BF16)
    w2_lo = (w2 - w2_hi.astype(F32)).astype(BF16)
    aw2 = jnp.concatenate([w2_hi, w2_hi, w2_lo, w2_lo], axis=0)
    cos_t, sin_t = _rope_tables(n, n_ctx)
    q, k, v, r, gf, gb = _gla_in(ctx, x, m0, norm1_g[0].reshape(1, d), gla_w_in[0].astype(BF16), aw1, aw2,
                                 gla_a_b[0].reshape(1, 2 * dk), cos_t, sin_t, dk, dv)
    o_f, o_b = _gla_core(q, k, v, gf, gb, GLA_HEADS, n_ctx)
    xs = _gla_out(o_f, o_b, r, ctx, x, m0, gla_norm_g[0].reshape(1, dv // GLA_HEADS),
                  gla_w_out[0].astype(BF16), GLA_HEADS)
    ffn_wi = ffn_w_in.astype(BF16)
    ffn_wo = ffn_w_out.astype(BF16)
    xs = _ffn(xs, m0, norm2_g[0].reshape(1, d), ffn_wi, ffn_conv_w[0],
              ffn_conv_b[0].reshape(1, dff), ffn_wo, final_g.reshape(1, d),
              layer=0, lat_only=False, final_norm=False)

    m1 = mod[1].reshape(SUBLANES, 1, 6 * d)
    head_dim = d // NA_HEADS
    qn, kn, vn = _na_qkv(xs, m1, norm1_g[1].reshape(1, d), na_w_qkv[0].astype(BF16), head_dim)
    bias = _na_bias(na_rpb[0])
    attn = _na_attn(qn, kn, vn, bias, n_ctx, head_dim)
    xs = _na_out(attn, xs, m1, na_w_out[0].astype(BF16))
    return _ffn(xs, m1, norm2_g[1].reshape(1, d), ffn_wi, ffn_conv_w[1],
                ffn_conv_b[1].reshape(1, dff), ffn_wo, final_g.reshape(1, d),
                layer=1, lat_only=True, final_norm=True)
```

```python
import functools

import numpy as np
import jax
import jax.numpy as jnp
from jax import lax
from jax.experimental import pallas as pl
from jax.experimental.pallas import tpu as pltpu

GRID_W = 64
GLA_HEADS = 4
GLA_GATE_NORM = 16.0
GLA_LOW_RANK = 16
ROPE_BASE = 10000.0
NA_HEADS = 16
NA_KH = 8
NA_KW = 16
EPS = 1e-6

LANES = 128
SUBLANES = 8
VMEM_LIMIT = 56 * 1024 * 1024

TM = 256
GLA_C = 128
GLA_BLK = 256
GLA_DIAG = 16
FFN_TM = 512
FFN_CHUNK = 512
NA_RB = 16
NA_U = 8
NA_HG = 4
NEG = -1e30

F32 = jnp.float32
BF16 = jnp.bfloat16


def _cparams(sem):
    return pltpu.CompilerParams(dimension_semantics=sem, vmem_limit_bytes=VMEM_LIMIT)


def _resident(shape):
    nd = len(shape)
    return pl.BlockSpec(shape, lambda *_: (0,) * nd, pipeline_mode=pl.Buffered(1))


def _silu(x):
    return x * (1.0 / (1.0 + jnp.exp(-x)))


def _norm_mod(x, g, shift, scale):
    ms = jnp.mean(x * x, axis=-1, keepdims=True)
    y = x * lax.rsqrt(ms + EPS) * g
    return y * (1.0 + scale) + shift


def _ada_kernel(c_ref, w_ref, b_ref, o_ref):
    sc = _silu(c_ref[...])
    o_ref[0] = jnp.dot(sc, w_ref[0], preferred_element_type=F32,
                       precision=lax.Precision.HIGHEST) + b_ref[0]


def _ada(c_all, ada_w, ada_b):
    depth, d, d6 = ada_w.shape
    tn = 1536
    return pl.pallas_call(
        _ada_kernel,
        grid=(depth, d6 // tn),
        in_specs=[
            pl.BlockSpec((SUBLANES, d), lambda i, j: (0, 0)),
            pl.BlockSpec((1, d, tn), lambda i, j: (i, 0, j)),
            pl.BlockSpec((1, 1, tn), lambda i, j: (i, 0, j)),
        ],
        out_specs=pl.BlockSpec((1, SUBLANES, tn), lambda i, j: (i, 0, j)),
        out_shape=jax.ShapeDtypeStruct((depth, SUBLANES, d6), F32),
        compiler_params=_cparams(("parallel", "parallel")),
        name="ada_mod",
    )(c_all, ada_w, ada_b.reshape(depth, 1, d6))


def _mod_spec(d, seg, nb, tile_off):
    def imap(b, i):
        return (jnp.where(i + tile_off == 0, nb, b), 0, seg)
    return pl.BlockSpec((1, 1, d), imap)


def _stream_specs(d):
    return [pl.BlockSpec((1, TM, d), lambda b, i: (b, 0, 0)),
            pl.BlockSpec((1, TM, d), lambda b, i: (b, jnp.maximum(i - 1, 0), 0))]


def _stream_tile(ctx_ref, x_ref):
    return jnp.where(pl.program_id(1) == 0, ctx_ref[0], x_ref[0])


def _gla_in_kernel(c_ref, x_ref, sh_ref, sc_ref, g_ref, w_ref, aw1_ref, aw2_ref, ab_ref,
                   cos_ref, sin_ref, q_ref, k_ref, v_ref, r_ref, gf_ref, gb_ref, *, dk, dv):
    h = _norm_mod(_stream_tile(c_ref, x_ref), g_ref[...], sh_ref[0], sc_ref[0]).astype(BF16)

    z = jnp.dot(h, aw1_ref[...], preferred_element_type=F32)
    z_hi = z.astype(BF16)
    z_lo = (z - z_hi.astype(F32)).astype(BF16)
    grp = lax.broadcasted_iota(jnp.int32, z.shape, 1) // (2 * GLA_LOW_RANK)
    zc = jnp.where((grp & 1) == 0, z_hi, z_lo)
    pre = jnp.dot(zc, aw2_ref[...], preferred_element_type=F32) + ab_ref[...]

    qk = jnp.dot(h, w_ref[:, :2 * dk], preferred_element_type=F32)
    g = (jnp.minimum(pre, 0.0) - jnp.log1p(jnp.exp(-jnp.abs(pre)))) * (1.0 / GLA_GATE_NORM)
    g_hi = g.astype(BF16)
    g_lo = (g - g_hi.astype(F32)).astype(BF16)
    v_ref[0] = jnp.dot(h, w_ref[:, 2 * dk: 2 * dk + dv], preferred_element_type=F32).astype(BF16)

    cos = cos_ref[...]
    sin = sin_ref[...]
    lane = lax.broadcasted_iota(jnp.int32, cos.shape, 1)
    low = (lane & 32) == 0
    qscale = float(LANES) ** -0.5

    def rope(t):
        partner = jnp.where(low, pltpu.roll(t, LANES - 32, 1), pltpu.roll(t, 32, 1))
        return t * cos + partner * sin

    for hd in range(dk // LANES):
        sl = slice(hd * LANES, (hd + 1) * LANES)
        q_ref[0, :, sl] = rope(qk[:, sl]) * qscale
        k_ref[0, :, sl] = rope(qk[:, dk + hd * LANES: dk + (hd + 1) * LANES])

    ti = lax.broadcasted_iota(jnp.int32, (GLA_C, GLA_C), 0)
    tj = lax.broadcasted_iota(jnp.int32, (GLA_C, GLA_C), 1)
    for o_ref, tri, sl in ((gf_ref, tj <= ti, slice(0, dk)), (gb_ref, tj >= ti, slice(dk, 2 * dk))):
        tri = jnp.where(tri, 1.0, 0.0).astype(BF16)
        tri2 = jnp.concatenate([tri, tri], axis=1)
        for ch in range(TM // GLA_C):
            rows = slice(ch * GLA_C, (ch + 1) * GLA_C)
            o_ref[0, rows, :] = jnp.dot(tri2, jnp.concatenate([g_hi[rows, sl], g_lo[rows, sl]], axis=0),
                                        preferred_element_type=F32)
    r_ref[0] = jnp.dot(h, w_ref[:, 2 * dk + dv:], preferred_element_type=F32).astype(BF16)


def _gla_in(ctx, x, mod, norm_g, w_in, aw1, aw2, ab, cos_t, sin_t, dk, dv):
    nb, n, d = x.shape
    t = ctx.shape[1] + n
    nt = t // TM
    kern = functools.partial(_gla_in_kernel, dk=dk, dv=dv)
    row = lambda w: pl.BlockSpec((1, TM, w), lambda b, i: (b, i, 0))
    return pl.pallas_call(
        kern,
        grid=(nb, nt),
        in_specs=_stream_specs(d) + [
            _mod_spec(d, 0, nb, 0),
            _mod_spec(d, 1, nb, 0),
            _resident((1, d)),
            _resident(w_in.shape),
            _resident(aw1.shape),
            _resident(aw2.shape),
            _resident(ab.shape),
            pl.BlockSpec((TM, LANES), lambda b, i: (i, 0)),
            pl.BlockSpec((TM, LANES), lambda b, i: (i, 0)),
        ],
        out_specs=[row(dk), row(dk), row(dv), row(dv), row(dk), row(dk)],
        out_shape=[
            jax.ShapeDtypeStruct((nb, t, dk), F32),
            jax.ShapeDtypeStruct((nb, t, dk), F32),
            jax.ShapeDtypeStruct((nb, t, dv), BF16),
            jax.ShapeDtypeStruct((nb, t, dv), BF16),
            jax.ShapeDtypeStruct((nb, t, dk), F32),
            jax.ShapeDtypeStruct((nb, t, dk), F32),
        ],
        compiler_params=_cparams(("parallel", "parallel")),
        name="gla_in",
    )(ctx, x, mod, mod, norm_g, w_in, aw1, aw2, ab, cos_t, sin_t)


def _row_bcast(b, period, offset):
    c, w = b.shape
    if period == c:
        return jnp.broadcast_to(b[offset:offset + 1, :], (c, w))
    b3 = b.reshape(c // period, period, w)
    return jnp.broadcast_to(b3[:, offset:offset + 1, :], b3.shape).reshape(c, w)


def _pair_scores(lhs, rhs, first):
    zero = jnp.zeros_like(rhs)
    blockdiag = jnp.concatenate([jnp.where(first, rhs, zero), jnp.where(first, zero, rhs)], axis=0)
    return lax.dot_general(lhs, blockdiag, (((1,), (1,)), ((), ())), preferred_element_type=F32)


def _gla_pair(q, k, b, rev, kside, same, causal, first):
    c = q.shape[0]
    b_last = b[0:1, :] if rev else b[c - 1:c, :]
    q_in = (q * jnp.exp(b)).astype(BF16)
    k_out = (k * jnp.exp(b_last - b)).astype(BF16)
    b_ref = _row_bcast(b, GLA_DIAG, GLA_DIAG - 1 if rev else 0)
    qd = (q * jnp.exp(b - b_ref)).astype(BF16)
    kd = (k * jnp.exp(b_ref - b)).astype(BF16)
    att = _pair_scores(qd, kd, first)
    period = 2 * GLA_DIAG
    while period <= c:
        half = period // 2
        b_ref = _row_bcast(b, period, half if rev else half - 1)
        u = (jnp.where(kside[period], k, q) * jnp.exp(-jnp.abs(b - b_ref))).astype(BF16)
        att = jnp.where(same[half], att, _pair_scores(u, u, first))
        period *= 2
    att = jnp.where(causal, att, 0.0).astype(BF16)
    return q_in, k_out, att, jnp.exp(b_last)


def _gla_state_step(q_in, att, k_out, v, decay_row, s):
    lhs = jnp.concatenate([q_in, att], axis=1)
    rhs = jnp.concatenate([s.astype(BF16), v], axis=0)
    o = jnp.dot(lhs, rhs, preferred_element_type=F32)
    decay = jnp.transpose(jnp.broadcast_to(decay_row, (LANES, LANES)))
    decay = jnp.concatenate([decay] * (s.shape[1] // LANES), axis=1)
    kv = lax.dot_general(k_out, v, (((0,), (0,)), ((), ())), preferred_element_type=F32)
    return o, s * decay + kv


def _gla_core_kernel(qf_ref, kf_ref, vf_ref, gf_ref, qb_ref, kb_ref, vb_ref, gb_ref,
                     of_ref, ob_ref, s_ref, *, heads, dvh):
    @pl.when(pl.program_id(1) == 0)
    def _():
        s_ref[...] = jnp.zeros_like(s_ref)

    c = GLA_C
    assert c == LANES and heads % 2 == 0
    row = lax.broadcasted_iota(jnp.int32, (c, 2 * LANES), 0)
    first = lax.broadcasted_iota(jnp.int32, (c, 2 * LANES), 1) < LANES
    ti = lax.broadcasted_iota(jnp.int32, (c, 2 * c), 0)
    tj = lax.broadcasted_iota(jnp.int32, (c, 2 * c), 1) & (c - 1)
    x = ti ^ tj
    same = {}
    half = GLA_DIAG
    while half < c:
        same[half] = x < half
        half *= 2

    n_chunks = qf_ref.shape[1] // c
    dirs = ((qf_ref, kf_ref, vf_ref, gf_ref, of_ref), (qb_ref, kb_ref, vb_ref, gb_ref, ob_ref))
    masks = []
    for d in range(2):
        rev = d == 1
        kside = {}
        period = 2 * GLA_DIAG
        while period <= c:
            off = row & (period - 1)
            kside[period] = (off >= period // 2) if rev else (off < period // 2)
            period *= 2
        masks.append((kside, (tj >= ti) if rev else (tj <= ti)))

    for step in range(n_chunks):
        pre = {}
        for d, (q_ref, k_ref, v_ref, b_ref, o_ref) in enumerate(dirs):
            rev = d == 1
            ch = n_chunks - 1 - step if rev else step
            rows = slice(ch * c, (ch + 1) * c)
            for pr in range(heads // 2):
                psl = slice(2 * pr * LANES, (2 * pr + 2) * LANES)
                pre[d, pr] = _gla_pair(q_ref[0, rows, psl], k_ref[0, rows, psl], b_ref[0, rows, psl],
                                       rev, masks[d][0], same, masks[d][1], first)
        for d, (q_ref, k_ref, v_ref, b_ref, o_ref) in enumerate(dirs):
            ch = n_chunks - 1 - step if d == 1 else step
            rows = slice(ch * c, (ch + 1) * c)
            for pr in range(heads // 2):
                q_in, k_out, att, decay = pre[d, pr]
                for j in range(2):
                    hd = 2 * pr + j
                    hsl = slice(j * LANES, (j + 1) * LANES)
                    vsl = slice(hd * dvh, (hd + 1) * dvh)
                    o, s_new = _gla_state_step(q_in[:, hsl], att[:, j * c:(j + 1) * c], k_out[:, hsl],
                                               v_ref[0, rows, vsl], decay[:, hsl], s_ref[d, hd])
                    o_ref[0, rows, vsl] = o.astype(o_ref.dtype)
                    s_ref[d, hd] = s_new


def _gla_core(q, k, v, gf, gb, heads, n_ctx):
    nb, t, dk = q.shape
    dv = v.shape[-1]
    assert n_ctx % GLA_BLK == 0 and t % GLA_BLK == 0
    nc = t // GLA_BLK
    ncx = n_ctx // GLA_BLK
    fwd = lambda b, s: (b, s, 0)
    bwd = lambda b, s: (b, jnp.where(s < ncx, ncx - 1 - s, nc - 1 + ncx - s), 0)
    kern = functools.partial(_gla_core_kernel, heads=heads, dvh=dv // heads)
    blk = lambda w, m: pl.BlockSpec((1, GLA_BLK, w), m)
    return pl.pallas_call(
        kern,
        grid=(nb, nc),
        in_specs=[blk(dk, fwd), blk(dk, fwd), blk(dv, fwd), blk(dk, fwd),
                  blk(dk, bwd), blk(dk, bwd), blk(dv, bwd), blk(dk, bwd)],
        out_specs=[blk(dv, fwd), blk(dv, bwd)],
        out_shape=[jax.ShapeDtypeStruct((nb, t, dv), BF16)] * 2,
        scratch_shapes=[pltpu.VMEM((2, heads, dk // heads, dv // heads), F32)],
        compiler_params=_cparams(("parallel", "arbitrary")),
        name="gla_core",
    )(q, k, v, gf, q, k, v, gb)


def _gla_out_kernel(of_ref, ob_ref, r_ref, c_ref, x_ref, gate_ref, sh2_ref, sc2_ref, ng_ref, w_ref, g2_ref,
                    xo_ref, ho_ref, *, heads):
    o = of_ref[0].astype(F32) + ob_ref[0].astype(F32)
    r = r_ref[0].astype(F32)
    dvh = o.shape[1] // heads
    parts = []
    for hd in range(heads):
        oh = o[:, hd * dvh:(hd + 1) * dvh]
        ms = jnp.mean(oh * oh, axis=-1, keepdims=True)
        parts.append(oh * lax.rsqrt(ms + EPS) * ng_ref[...])
    y = jnp.concatenate(parts, axis=1) * _silu(r)
    out = jnp.dot(y.astype(BF16), w_ref[...], preferred_element_type=F32)
    x_mid = _stream_tile(c_ref, x_ref) + gate_ref[0] * out
    xo_ref[0] = x_mid
    ho_ref[0] = _norm_mod(x_mid, g2_ref[...], sh2_ref[0], sc2_ref[0]).astype(BF16)


def _gla_out(o_f, o_b, r, ctx, x, mod, norm_g, w_out, norm2_g, heads):
    nb, t, dv = o_f.shape
    d = x.shape[-1]
    nt = t // TM
    row = lambda w: pl.BlockSpec((1, TM, w), lambda b, i: (b, i, 0))
    out = pl.BlockSpec((1, TM, d), lambda b, i: (b, jnp.where(i == 0, nt - 1, i - 1), 0))
    return pl.pallas_call(
        functools.partial(_gla_out_kernel, heads=heads),
        grid=(nb, nt),
        in_specs=[row(dv), row(dv), row(dv)] + _stream_specs(d) + [
            _mod_spec(d, 2, nb, 0), _mod_spec(d, 3, nb, 0), _mod_spec(d, 4, nb, 0),
            _resident(norm_g.shape), _resident(w_out.shape), _resident((1, d))],
        out_specs=[out, out],
        out_shape=[jax.ShapeDtypeStruct((nb, t, d), F32), jax.ShapeDtypeStruct((nb, t, d), BF16)],
        compiler_params=_cparams(("parallel", "parallel")),
        name="gla_out",
    )(o_f, o_b, r, ctx, x, mod, mod, mod, norm_g, w_out, norm2_g)


def _ffn_kernel(h_ref, hp_ref, hn_ref, x_ref, gate_ref, wi_ref, cw_ref, cb_ref, wo_ref, fg_ref,
                o_ref, hbuf_ref, act_ref, *, chunks, final_norm):
    i = pl.program_id(1)
    tm = h_ref.shape[1]
    dff = act_ref.shape[1]
    hs = SUBLANES
    has_prev = i > 0
    has_next = i < pl.num_programs(1) - 1
    hbuf_ref[0:tm, :] = h_ref[0]
    nxt = jnp.where(has_next, hn_ref[0, 0:hs, :].astype(F32), 0.0)
    prv = jnp.where(has_prev, hp_ref[0, hs:2 * hs, :].astype(F32), 0.0)
    hbuf_ref[tm:, :] = jnp.concatenate([nxt, prv], axis=0).astype(BF16)
    n = tm + 2 * hs

    for (c0, cw) in chunks:
        a = jnp.dot(hbuf_ref[...], wi_ref[0, :, c0:c0 + cw], preferred_element_type=F32)
        val = jnp.dot(hbuf_ref[0:tm, :], wi_ref[0, :, dff + c0:dff + c0 + cw], preferred_element_type=F32)
        w3 = cw_ref[:, c0:c0 + cw]
        conv = (pltpu.roll(a, 1, 0)[0:tm] * w3[0:1] + a[0:tm] * w3[1:2]
                + pltpu.roll(a, n - 1, 0)[0:tm] * w3[2:3] + cb_ref[:, c0:c0 + cw])
        act_ref[:, c0:c0 + cw] = (_silu(conv) * val).astype(BF16)
    y = x_ref[0] + gate_ref[0] * jnp.dot(act_ref[...], wo_ref[0], preferred_element_type=F32)
    if final_norm:
        ms = jnp.mean(y * y, axis=-1, keepdims=True)
        y = y * lax.rsqrt(ms + EPS) * fg_ref[...]
    o_ref[0] = y


def _ffn(h, x, mod, w_in, conv_w, conv_b, w_out, final_g, *, row0, rows, layer, is_ctx, final_norm):
    nb, _, d = x.shape
    t = rows
    dff = w_out.shape[1]
    layer_block = lambda w: pl.BlockSpec((1,) + w.shape[1:], lambda *_: (layer, 0, 0),
                                         pipeline_mode=pl.Buffered(1))
    tm = min(FFN_TM, t)
    hb = 2 * SUBLANES
    bpt = tm // hb
    assert row0 % tm == 0 and rows % tm == 0
    t0 = row0 // tm
    b0 = row0 // hb
    nblk = t // hb
    mod_row = (lambda b: nb) if is_ctx else (lambda b: b)
    chunks = []
    c0 = 0
    while c0 < dff:
        cw = min(FFN_CHUNK, dff - c0)
        chunks.append((c0, cw))
        c0 += cw
    kern = functools.partial(_ffn_kernel, chunks=tuple(chunks), final_norm=final_norm)
    row = pl.BlockSpec((1, tm, d), lambda b, i: (b, t0 + i, 0))
    return pl.pallas_call(
        kern,
        grid=(nb, t // tm),
        in_specs=[
            row,
            pl.BlockSpec((1, hb, d), lambda b, i: (b, b0 + jnp.maximum(i * bpt - 1, 0), 0)),
            pl.BlockSpec((1, hb, d), lambda b, i: (b, b0 + jnp.minimum((i + 1) * bpt, nblk - 1), 0)),
            row,
            pl.BlockSpec((1, 1, d), lambda b, i: (mod_row(b), 0, 5)),
            layer_block(w_in),
            _resident(conv_w.shape),
            _resident(conv_b.shape),
            layer_block(w_out),
            _resident((1, d)),
        ],
        out_specs=pl.BlockSpec((1, tm, d), lambda b, i: (b, i, 0)),
        out_shape=jax.ShapeDtypeStruct((nb, t, d), F32),
        scratch_shapes=[pltpu.VMEM((tm + 2 * SUBLANES, d), BF16), pltpu.VMEM((tm, dff), BF16)],
        compiler_params=_cparams(("parallel", "parallel")),
        name=("ffn_final" if final_norm else "ffn") + ("_ctx" if is_ctx else ""),
    )(h, h, h, x, mod, w_in, conv_w, conv_b, w_out, final_g)


def _na_qkv_kernel(c_ref, x_ref, sh_ref, sc_ref, g_ref, w_ref, q_ref, k_ref, v_ref, *, d, qscale):
    h = _norm_mod(_stream_tile(c_ref, x_ref), g_ref[...], sh_ref[0], sc_ref[0]).astype(BF16)
    proj = jnp.dot(h, w_ref[...], preferred_element_type=F32)
    q_ref[0] = (proj[:, :d] * qscale).astype(BF16)
    k_ref[0] = proj[:, d:2 * d].astype(BF16)
    v_ref[0] = proj[:, 2 * d:].astype(BF16)


def _na_qkv(ctx, x, mod, norm_g, w_qkv, head_dim):
    nb, n, d = x.shape
    t = ctx.shape[1] + n
    row = lambda w: pl.BlockSpec((1, TM, w), lambda b, i: (b, i, 0))
    return pl.pallas_call(
        functools.partial(_na_qkv_kernel, d=d, qscale=float(head_dim) ** -0.5),
        grid=(nb, t // TM),
        in_specs=_stream_specs(d) + [_mod_spec(d, 0, nb, 0), _mod_spec(d, 1, nb, 0),
                                     _resident((1, d)), _resident(w_qkv.shape)],
        out_specs=[row(d)] * 3,
        out_shape=[jax.ShapeDtypeStruct((nb, t, d), BF16)] * 3,
        compiler_params=_cparams(("parallel", "parallel")),
        name="na_qkv",
    )(ctx, x, mod, mod, norm_g, w_qkv)


def _na_attn_kernel(q_ref, k_ref, v_ref, bias_ref, o_ref, *, n_ctx, rows_n, head_dim):
    rb = pl.program_id(2)
    hw = q_ref.shape[-1]
    nwin = NA_KH * GRID_W
    nt = (((1,), (1,)), ((), ()))
    lane = lax.broadcasted_iota(jnp.int32, (GRID_W, hw), 1)
    hmask = [(lane // head_dim) == hd for hd in range(NA_HG)]

    def window(i):
        r = rb * NA_RB + i
        rs = jnp.clip(r - NA_KH // 2, 0, rows_n - NA_KH)
        lo = rs - r + (NA_KH - 1)
        return r, lo, pl.multiple_of(n_ctx + rs * GRID_W, GRID_W)

    def scores(i):
        r, lo, start = window(i)
        q = q_ref[0, pl.ds(pl.multiple_of(n_ctx + r * GRID_W, GRID_W), GRID_W), :]
        qs = jnp.concatenate([jnp.where(hmask[hd], q, jnp.zeros_like(q)) for hd in range(NA_HG)], axis=0)
        bias = jnp.concatenate(
            [jnp.concatenate([bias_ref[0, hd, lo + 2 * m] for m in range(NA_KH // 2)], axis=1)
             for hd in range(NA_HG)], axis=0)
        s_nb = lax.dot_general(qs, k_ref[0, pl.ds(start, nwin), :], nt, preferred_element_type=F32) + bias
        s_cx = lax.dot_general(qs, k_ref[0, 0:n_ctx, :], nt, preferred_element_type=F32)
        m = jnp.maximum(jnp.max(s_nb, axis=-1, keepdims=True), jnp.max(s_cx, axis=-1, keepdims=True))
        return s_nb, s_cx, m

    def finish(i, s_nb, s_cx, m):
        r, lo, start = window(i)
        p_nb = jnp.exp(s_nb - m)
        p_cx = jnp.exp(s_cx - m)
        l = jnp.sum(p_nb, axis=-1, keepdims=True) + jnp.sum(p_cx, axis=-1, keepdims=True)
        o = (jnp.dot(p_nb.astype(BF16), v_ref[0, pl.ds(start, nwin), :], preferred_element_type=F32)
             + jnp.dot(p_cx.astype(BF16), v_ref[0, 0:n_ctx, :], preferred_element_type=F32))
        o = o * (1.0 / l)
        out = o[0:GRID_W]
        for hd in range(1, NA_HG):
            out = jnp.where(hmask[hd], o[hd * GRID_W:(hd + 1) * GRID_W], out)
        o_ref[0, pl.ds(pl.multiple_of(i * GRID_W, GRID_W), GRID_W), :] = out.astype(BF16)

    def body(it, carry):
        nxt = scores(it * NA_U)
        for u in range(NA_U):
            cur = nxt
            if u + 1 < NA_U:
                nxt = scores(it * NA_U + u + 1)
            finish(it * NA_U + u, *cur)
        return carry

    lax.fori_loop(0, NA_RB // NA_U, body, 0)


def _na_bias(rpb):
    heads, ndr, ndc = rpb.shape
    c = np.arange(GRID_W)
    cs = np.clip(c - NA_KW // 2, 0, GRID_W - NA_KW)
    onehot = np.zeros((2 * ndc, GRID_W, 2 * GRID_W), np.float32)
    mask = np.full((GRID_W, 2 * GRID_W), NEG, np.float32)
    for half in range(2):
        for cq in range(GRID_W):
            for kc in range(cs[cq], cs[cq] + NA_KW):
                onehot[half * ndc + kc - cq + NA_KW - 1, cq, half * GRID_W + kc] = 1.0
                mask[cq, half * GRID_W + kc] = 0.0
    pair = jnp.concatenate([rpb[:, :ndr - 1, :], rpb[:, 1:, :]], axis=-1)
    tiles = jnp.einsum('hdk,kcl->hdcl', pair, onehot, precision=lax.Precision.HIGHEST) + mask
    return tiles.reshape(heads // NA_HG, NA_HG, ndr - 1, GRID_W, 2 * GRID_W)


def _na_attn(q, k, v, bias, n_ctx, head_dim):
    nb, t, d = q.shape
    rows_n = (t - n_ctx) // GRID_W
    nrb = rows_n // NA_RB
    nq = NA_RB * GRID_W
    hw = NA_HG * head_dim
    ngrp = d // hw
    assert n_ctx % nq == 0 or nq % n_ctx == 0
    kern = functools.partial(_na_attn_kernel, n_ctx=n_ctx, rows_n=rows_n, head_dim=head_dim)
    return pl.pallas_call(
        kern,
        grid=(nb, ngrp, nrb),
        in_specs=[
            pl.BlockSpec((1, t, hw), lambda b, g, r: (b, 0, g)),
            pl.BlockSpec((1, t, hw), lambda b, g, r: (b, 0, g)),
            pl.BlockSpec((1, t, hw), lambda b, g, r: (b, 0, g)),
            pl.BlockSpec((1,) + bias.shape[1:], lambda b, g, r: (g, 0, 0, 0, 0)),
        ],
        out_specs=pl.BlockSpec((1, nq, hw), lambda b, g, r: (b, r, g)),
        out_shape=jax.ShapeDtypeStruct((nb, t - n_ctx, d), BF16),
        compiler_params=_cparams(("parallel", "parallel", "arbitrary")),
        name="na_attn",
    )(q, k, v, bias)


def _na_out_kernel(a_ref, x_ref, gate_ref, sh2_ref, sc2_ref, w_ref, g2_ref, o_ref, h_ref):
    out = jnp.dot(a_ref[0], w_ref[...], preferred_element_type=F32)
    x_mid = x_ref[0] + gate_ref[0] * out
    o_ref[0] = x_mid
    h_ref[0] = _norm_mod(x_mid, g2_ref[...], sh2_ref[0], sc2_ref[0]).astype(BF16)


def _na_out(attn, x, mod, w_out, norm2_g):
    nb, n, d = x.shape
    tm = FFN_TM
    row = pl.BlockSpec((1, tm, d), lambda b, i: (b, i, 0))
    seg = lambda s: pl.BlockSpec((1, 1, d), lambda b, i: (b, 0, s))
    return pl.pallas_call(
        _na_out_kernel,
        grid=(nb, n // tm),
        in_specs=[row, row, seg(2), seg(3), seg(4), _resident(w_out.shape), _resident((1, d))],
        out_specs=[row, row],
        out_shape=[jax.ShapeDtypeStruct((nb, n, d), F32), jax.ShapeDtypeStruct((nb, n, d), BF16)],
        compiler_params=_cparams(("parallel", "parallel")),
        name="na_out",
    )(attn, x, mod, mod, mod, w_out, norm2_g)


def _rope_tables(n_lat, n_ctx):
    half = LANES // 2
    inv = 1.0 / (ROPE_BASE ** (np.arange(0, half, 2, dtype=np.float64) / half))
    pos = np.arange(n_lat)
    rows = (pos // GRID_W).astype(np.float64)[:, None] * inv[None, :]
    cols = (pos % GRID_W).astype(np.float64)[:, None] * inv[None, :]
    cos = np.concatenate([np.cos(rows)] * 2 + [np.cos(cols)] * 2, axis=1)
    sin = np.concatenate([-np.sin(rows), np.sin(rows), -np.sin(cols), np.sin(cols)], axis=1)
    cos = np.concatenate([np.ones((n_ctx, LANES)), cos], axis=0).astype(np.float32)
    sin = np.concatenate([np.zeros((n_ctx, LANES)), sin], axis=0).astype(np.float32)
    return jnp.asarray(cos), jnp.asarray(sin)


def kernel(x, c, ctx, c_ctx, ada_w, ada_b, norm1_g, norm2_g, ffn_w_in, ffn_conv_w, ffn_conv_b, ffn_w_out,
           gla_w_in, gla_a_w1, gla_a_w2, gla_a_b, gla_norm_g, gla_w_out, na_w_qkv, na_rpb, na_w_out, final_g):
    nb, n, d = x.shape
    n_ctx = ctx.shape[1]
    depth = ada_w.shape[0]
    dff = ffn_conv_b.shape[-1]
    assert n_ctx == TM and n % TM == 0 and depth == 2 and nb + 1 <= SUBLANES

    c_all = jnp.zeros((SUBLANES, d), F32).at[:nb].set(c).at[nb].set(c_ctx)
    mod = _ada(c_all, ada_w, ada_b)

    m0 = mod[0].reshape(SUBLANES, 1, 6 * d)
    dk = gla_a_w2.shape[-1]
    dv = gla_w_out.shape[1]
    assert 4 * 2 * GLA_LOW_RANK == LANES
    aw1 = jnp.concatenate([gla_a_w1[0, 0], gla_a_w1[0, 1]] * 4, axis=1).astype(BF16)
    zeros = jnp.zeros((GLA_LOW_RANK, dk), F32)
    w2 = jnp.concatenate([jnp.concatenate([gla_a_w2[0, 0], zeros], axis=1),
                          jnp.concatenate([zeros, gla_a_w2[0, 1]], axis=1)], axis=0)
    w2_hi = w2.astype(BF16)
    w2_lo = (w2 - w2_hi.astype(F32)).astype(BF16)
    aw2 = jnp.concatenate([w2_hi, w2_hi, w2_lo, w2_lo], axis=0)
    cos_t, sin_t = _rope_tables(n, n_ctx)
    q, k, v, r, gf, gb = _gla_in(ctx, x, m0, norm1_g[0].reshape(1, d), gla_w_in[0].astype(BF16), aw1, aw2,
                                 gla_a_b[0].reshape(1, 2 * dk), cos_t, sin_t, dk, dv)
    o_f, o_b = _gla_core(q, k, v, gf, gb, GLA_HEADS, n_ctx)
    x_mid, h_mid = _gla_out(o_f, o_b, r, ctx, x, m0, gla_norm_g[0].reshape(1, dv // GLA_HEADS),
                            gla_w_out[0].astype(BF16), norm2_g[0].reshape(1, d), GLA_HEADS)
    ffn_wi = ffn_w_in.astype(BF16)
    ffn_wo = ffn_w_out.astype(BF16)
    ffn0 = functools.partial(_ffn, h_mid, x_mid, m0, ffn_wi, ffn_conv_w[0], ffn_conv_b[0].reshape(1, dff),
                             ffn_wo, final_g.reshape(1, d), layer=0, final_norm=False)
    x_lat = ffn0(row0=0, rows=n, is_ctx=False)
    x_ctx = ffn0(row0=n, rows=n_ctx, is_ctx=True)

    m1 = mod[1].reshape(SUBLANES, 1, 6 * d)
    head_dim = d // NA_HEADS
    qn, kn, vn = _na_qkv(x_ctx, x_lat, m1, norm1_g[1].reshape(1, d), na_w_qkv[0].astype(BF16), head_dim)
    bias = _na_bias(na_rpb[0])
    attn = _na_attn(qn, kn, vn, bias, n_ctx, head_dim)
    x_lat, h_lat = _na_out(attn, x_lat, m1, na_w_out[0].astype(BF16), norm2_g[1].reshape(1, d))
    return _ffn(h_lat, x_lat, m1, ffn_wi, ffn_conv_w[1], ffn_conv_b[1].reshape(1, dff), ffn_wo,
                final_g.reshape(1, d), row0=0, rows=n, layer=1, is_ctx=False, final_norm=True)
```

```python
import functools

import numpy as np
import jax
import jax.numpy as jnp
from jax import lax
from jax.experimental import pallas as pl
from jax.experimental.pallas import tpu as pltpu

GRID_W = 64
GLA_HEADS = 4
GLA_GATE_NORM = 16.0
GLA_LOW_RANK = 16
ROPE_BASE = 10000.0
NA_HEADS = 16
NA_KH = 8
NA_KW = 16
EPS = 1e-6
LOG2E = 1.4426950408889634

LANES = 128
SUBLANES = 8
VMEM_LIMIT = 56 * 1024 * 1024

TM = 256
GLA_C = 128
GLA_BLK = 256
GLA_DIAG = 16
FFN_TM = 512
FFN_CHUNK = 512
NA_RB = 16
NA_U = 16
NA_HG = 4
NEG = -1e30

F32 = jnp.float32
BF16 = jnp.bfloat16


def _cparams(sem):
    return pltpu.CompilerParams(dimension_semantics=sem, vmem_limit_bytes=VMEM_LIMIT)


def _resident(shape):
    nd = len(shape)
    return pl.BlockSpec(shape, lambda *_: (0,) * nd, pipeline_mode=pl.Buffered(1))


def _silu(x):
    return x * (1.0 / (1.0 + jnp.exp(-x)))


def _norm_mod(x, g, shift, scale):
    ms = jnp.mean(x * x, axis=-1, keepdims=True)
    y = x * lax.rsqrt(ms + EPS) * g
    return y * (1.0 + scale) + shift


def _ada_kernel(c_ref, w_ref, b_ref, o_ref, *, n_rows):
    tn = w_ref.shape[2]
    rows = []
    for r in range(n_rows):
        sc = _silu(c_ref[r])
        parts = [jnp.sum(w_ref[0, :, j * LANES:(j + 1) * LANES] * sc, axis=0, keepdims=True)
                 for j in range(tn // LANES)]
        rows.append(jnp.concatenate(parts, axis=1) + b_ref[0])
    rows.append(jnp.zeros((SUBLANES - n_rows, tn), F32))
    o_ref[0] = jnp.concatenate(rows, axis=0)


def _ada(c_rows, ada_w, ada_b):
    depth, d, d6 = ada_w.shape
    n_rows = c_rows.shape[0]
    tn = 1536
    c_b = jnp.broadcast_to(c_rows[:, :, None], (n_rows, d, LANES))
    return pl.pallas_call(
        functools.partial(_ada_kernel, n_rows=n_rows),
        grid=(depth, d6 // tn),
        in_specs=[
            pl.BlockSpec((n_rows, d, LANES), lambda i, j: (0, 0, 0)),
            pl.BlockSpec((1, d, tn), lambda i, j: (i, 0, j)),
            pl.BlockSpec((1, 1, tn), lambda i, j: (i, 0, j)),
        ],
        out_specs=pl.BlockSpec((1, SUBLANES, tn), lambda i, j: (i, 0, j)),
        out_shape=jax.ShapeDtypeStruct((depth, SUBLANES, d6), F32),
        compiler_params=_cparams(("parallel", "parallel")),
        name="ada_mod",
    )(c_b, ada_w, ada_b.reshape(depth, 1, d6))


def _mod_spec(d, seg, nb, tile_off):
    def imap(b, i):
        return (jnp.where(i + tile_off == 0, nb, b), 0, seg)
    return pl.BlockSpec((1, 1, d), imap)


def _stream_specs(d):
    return [pl.BlockSpec((1, TM, d), lambda b, i: (b, 0, 0)),
            pl.BlockSpec((1, TM, d), lambda b, i: (b, jnp.maximum(i - 1, 0), 0))]


def _stream_tile(ctx_ref, x_ref):
    return jnp.where(pl.program_id(1) == 0, ctx_ref[0], x_ref[0])


def _gla_in_kernel(c_ref, x_ref, sh_ref, sc_ref, g_ref, w_ref, aw1_ref, aw2_ref, ab_ref,
                   cos_ref, sin_ref, q_ref, k_ref, v_ref, r_ref, gf_ref, gb_ref, *, dk, dv):
    h = _norm_mod(_stream_tile(c_ref, x_ref), g_ref[...], sh_ref[0], sc_ref[0]).astype(BF16)

    z = jnp.dot(h, aw1_ref[...], preferred_element_type=F32)
    z_hi = z.astype(BF16)
    z_lo = (z - z_hi.astype(F32)).astype(BF16)
    grp = lax.broadcasted_iota(jnp.int32, z.shape, 1) // (2 * GLA_LOW_RANK)
    zc = jnp.where((grp & 1) == 0, z_hi, z_lo)
    pre = jnp.dot(zc, aw2_ref[...], preferred_element_type=F32) + ab_ref[...]

    qk = jnp.dot(h, w_ref[:, :2 * dk], preferred_element_type=F32)
    g = (jnp.minimum(pre, 0.0) - jnp.log1p(jnp.exp(-jnp.abs(pre)))) * (LOG2E / GLA_GATE_NORM)
    g_hi = g.astype(BF16)
    g_lo = (g - g_hi.astype(F32)).astype(BF16)
    v_ref[0] = jnp.dot(h, w_ref[:, 2 * dk: 2 * dk + dv], preferred_element_type=F32).astype(BF16)

    cos = cos_ref[...]
    sin = sin_ref[...]
    lane = lax.broadcasted_iota(jnp.int32, cos.shape, 1)
    low = (lane & 32) == 0
    qscale = float(LANES) ** -0.5

    def rope(t):
        partner = jnp.where(low, pltpu.roll(t, LANES - 32, 1), pltpu.roll(t, 32, 1))
        return t * cos + partner * sin

    for hd in range(dk // LANES):
        sl = slice(hd * LANES, (hd + 1) * LANES)
        q_ref[0, :, sl] = rope(qk[:, sl]) * qscale
        k_ref[0, :, sl] = rope(qk[:, dk + hd * LANES: dk + (hd + 1) * LANES])

    ti = lax.broadcasted_iota(jnp.int32, (GLA_C, GLA_C), 0)
    tj = lax.broadcasted_iota(jnp.int32, (GLA_C, GLA_C), 1)
    for o_ref, tri, sl in ((gf_ref, tj <= ti, slice(0, dk)), (gb_ref, tj >= ti, slice(dk, 2 * dk))):
        tri = jnp.where(tri, 1.0, 0.0).astype(BF16)
        tri2 = jnp.concatenate([tri, tri], axis=1)
        for ch in range(TM // GLA_C):
            rows = slice(ch * GLA_C, (ch + 1) * GLA_C)
            o_ref[0, rows, :] = jnp.dot(tri2, jnp.concatenate([g_hi[rows, sl], g_lo[rows, sl]], axis=0),
                                        preferred_element_type=F32)
    r_ref[0] = jnp.dot(h, w_ref[:, 2 * dk + dv:], preferred_element_type=F32).astype(BF16)


def _gla_in(ctx, x, mod, norm_g, w_in, aw1, aw2, ab, cos_t, sin_t, dk, dv):
    nb, n, d = x.shape
    t = ctx.shape[1] + n
    nt = t // TM
    kern = functools.partial(_gla_in_kernel, dk=dk, dv=dv)
    row = lambda w: pl.BlockSpec((1, TM, w), lambda b, i: (b, i, 0))
    return pl.pallas_call(
        kern,
        grid=(nb, nt),
        in_specs=_stream_specs(d) + [
            _mod_spec(d, 0, nb, 0),
            _mod_spec(d, 1, nb, 0),
            _resident((1, d)),
            _resident(w_in.shape),
            _resident(aw1.shape),
            _resident(aw2.shape),
            _resident(ab.shape),
            pl.BlockSpec((TM, LANES), lambda b, i: (i, 0)),
            pl.BlockSpec((TM, LANES), lambda b, i: (i, 0)),
        ],
        out_specs=[row(dk), row(dk), row(dv), row(dv), row(dk), row(dk)],
        out_shape=[
            jax.ShapeDtypeStruct((nb, t, dk), F32),
            jax.ShapeDtypeStruct((nb, t, dk), F32),
            jax.ShapeDtypeStruct((nb, t, dv), BF16),
            jax.ShapeDtypeStruct((nb, t, dv), BF16),
            jax.ShapeDtypeStruct((nb, t, dk), F32),
            jax.ShapeDtypeStruct((nb, t, dk), F32),
        ],
        compiler_params=_cparams(("parallel", "parallel")),
        name="gla_in",
    )(ctx, x, mod, mod, norm_g, w_in, aw1, aw2, ab, cos_t, sin_t)


def _row_bcast(b, period, offset):
    c, w = b.shape
    if period == c:
        return jnp.broadcast_to(b[offset:offset + 1, :], (c, w))
    b3 = b.reshape(c // period, period, w)
    return jnp.broadcast_to(b3[:, offset:offset + 1, :], b3.shape).reshape(c, w)


def _pair_scores(lhs, rhs, first):
    zero = jnp.zeros_like(rhs)
    blockdiag = jnp.concatenate([jnp.where(first, rhs, zero), jnp.where(first, zero, rhs)], axis=0)
    return lax.dot_general(lhs, blockdiag, (((1,), (1,)), ((), ())), preferred_element_type=F32)


def _gla_pair(q, k, b, rev, kside, same, causal, first):
    c = q.shape[0]
    b_last = b[0:1, :] if rev else b[c - 1:c, :]
    q_in = (q * jnp.exp2(b)).astype(BF16)
    k_out = (k * jnp.exp2(b_last - b)).astype(BF16)
    b_ref = _row_bcast(b, GLA_DIAG, GLA_DIAG - 1 if rev else 0)
    qd = (q * jnp.exp2(b - b_ref)).astype(BF16)
    kd = (k * jnp.exp2(b_ref - b)).astype(BF16)
    att = _pair_scores(qd, kd, first)
    period = 2 * GLA_DIAG
    while period <= c:
        half = period // 2
        b_ref = _row_bcast(b, period, half if rev else half - 1)
        u = (jnp.where(kside[period], k, q) * jnp.exp2(-jnp.abs(b - b_ref))).astype(BF16)
        att = jnp.where(same[half], att, _pair_scores(u, u, first))
        period *= 2
    att = jnp.where(causal, att, 0.0).astype(BF16)
    return q_in, k_out, att, jnp.exp2(b_last)


def _gla_state_step(q_in, att, k_out, v, decay_row, s):
    lhs = jnp.concatenate([q_in, att], axis=1)
    rhs = jnp.concatenate([s.astype(BF16), v], axis=0)
    o = jnp.dot(lhs, rhs, preferred_element_type=F32)
    decay = jnp.transpose(jnp.broadcast_to(decay_row, (LANES, LANES)))
    decay = jnp.concatenate([decay] * (s.shape[1] // LANES), axis=1)
    kv = lax.dot_general(k_out, v, (((0,), (0,)), ((), ())), preferred_element_type=F32)
    return o, s * decay + kv


def _gla_core_kernel(qf_ref, kf_ref, vf_ref, gf_ref, qb_ref, kb_ref, vb_ref, gb_ref,
                     of_ref, ob_ref, s_ref, *, heads, dvh):
    @pl.when(pl.program_id(1) == 0)
    def _():
        s_ref[...] = jnp.zeros_like(s_ref)

    c = GLA_C
    assert c == LANES and heads % 2 == 0
    row = lax.broadcasted_iota(jnp.int32, (c, 2 * LANES), 0)
    first = lax.broadcasted_iota(jnp.int32, (c, 2 * LANES), 1) < LANES
    ti = lax.broadcasted_iota(jnp.int32, (c, 2 * c), 0)
    tj = lax.broadcasted_iota(jnp.int32, (c, 2 * c), 1) & (c - 1)
    x = ti ^ tj
    same = {}
    half = GLA_DIAG
    while half < c:
        same[half] = x < half
        half *= 2

    n_chunks = qf_ref.shape[1] // c
    dirs = ((qf_ref, kf_ref, vf_ref, gf_ref, of_ref), (qb_ref, kb_ref, vb_ref, gb_ref, ob_ref))
    masks = []
    for d in range(2):
        rev = d == 1
        kside = {}
        period = 2 * GLA_DIAG
        while period <= c:
            off = row & (period - 1)
            kside[period] = (off >= period // 2) if rev else (off < period // 2)
            period *= 2
        masks.append((kside, (tj >= ti) if rev else (tj <= ti)))

    for step in range(n_chunks):
        pre = {}
        for d, (q_ref, k_ref, v_ref, b_ref, o_ref) in enumerate(dirs):
            rev = d == 1
            ch = n_chunks - 1 - step if rev else step
            rows = slice(ch * c, (ch + 1) * c)
            for pr in range(heads // 2):
                psl = slice(2 * pr * LANES, (2 * pr + 2) * LANES)
                pre[d, pr] = _gla_pair(q_ref[0, rows, psl], k_ref[0, rows, psl], b_ref[0, rows, psl],
                                       rev, masks[d][0], same, masks[d][1], first)
        for d, (q_ref, k_ref, v_ref, b_ref, o_ref) in enumerate(dirs):
            ch = n_chunks - 1 - step if d == 1 else step
            rows = slice(ch * c, (ch + 1) * c)
            for pr in range(heads // 2):
                q_in, k_out, att, decay = pre[d, pr]
                for j in range(2):
                    hd = 2 * pr + j
                    hsl = slice(j * LANES, (j + 1) * LANES)
                    vsl = slice(hd * dvh, (hd + 1) * dvh)
                    o, s_new = _gla_state_step(q_in[:, hsl], att[:, j * c:(j + 1) * c], k_out[:, hsl],
                                               v_ref[0, rows, vsl], decay[:, hsl], s_ref[d, hd])
                    o_ref[0, rows, vsl] = o.astype(o_ref.dtype)
                    s_ref[d, hd] = s_new


def _gla_core(q, k, v, gf, gb, heads, n_ctx):
    nb, t, dk = q.shape
    dv = v.shape[-1]
    assert n_ctx % GLA_BLK == 0 and t % GLA_BLK == 0
    nc = t // GLA_BLK
    ncx = n_ctx // GLA_BLK
    fwd = lambda b, s: (b, s, 0)
    bwd = lambda b, s: (b, jnp.where(s < ncx, ncx - 1 - s, nc - 1 + ncx - s), 0)
    kern = functools.partial(_gla_core_kernel, heads=heads, dvh=dv // heads)
    blk = lambda w, m: pl.BlockSpec((1, GLA_BLK, w), m)
    return pl.pallas_call(
        kern,
        grid=(nb, nc),
        in_specs=[blk(dk, fwd), blk(dk, fwd), blk(dv, fwd), blk(dk, fwd),
                  blk(dk, bwd), blk(dk, bwd), blk(dv, bwd), blk(dk, bwd)],
        out_specs=[blk(dv, fwd), blk(dv, bwd)],
        out_shape=[jax.ShapeDtypeStruct((nb, t, dv), BF16)] * 2,
        scratch_shapes=[pltpu.VMEM((2, heads, dk // heads, dv // heads), F32)],
        compiler_params=_cparams(("parallel", "arbitrary")),
        name="gla_core",
    )(q, k, v, gf, q, k, v, gb)


def _gla_out_kernel(of_ref, ob_ref, r_ref, c_ref, x_ref, gate_ref, sh2_ref, sc2_ref, ng_ref, w_ref, g2_ref,
                    xo_ref, ho_ref, *, heads):
    o = of_ref[0].astype(F32) + ob_ref[0].astype(F32)
    r = r_ref[0].astype(F32)
    dvh = o.shape[1] // heads
    parts = []
    for hd in range(heads):
        oh = o[:, hd * dvh:(hd + 1) * dvh]
        ms = jnp.mean(oh * oh, axis=-1, keepdims=True)
        parts.append(oh * lax.rsqrt(ms + EPS) * ng_ref[...])
    y = jnp.concatenate(parts, axis=1) * _silu(r)
    out = jnp.dot(y.astype(BF16), w_ref[...], preferred_element_type=F32)
    x_mid = _stream_tile(c_ref, x_ref) + gate_ref[0] * out
    xo_ref[0] = x_mid
    ho_ref[0] = _norm_mod(x_mid, g2_ref[...], sh2_ref[0], sc2_ref[0]).astype(BF16)


def _gla_out(o_f, o_b, r, ctx, x, mod, norm_g, w_out, norm2_g, heads):
    nb, t, dv = o_f.shape
    d = x.shape[-1]
    nt = t // TM
    row = lambda w: pl.BlockSpec((1, TM, w), lambda b, i: (b, i, 0))
    out = pl.BlockSpec((1, TM, d), lambda b, i: (b, jnp.where(i == 0, nt - 1, i - 1), 0))
    return pl.pallas_call(
        functools.partial(_gla_out_kernel, heads=heads),
        grid=(nb, nt),
        in_specs=[row(dv), row(dv), row(dv)] + _stream_specs(d) + [
            _mod_spec(d, 2, nb, 0), _mod_spec(d, 3, nb, 0), _mod_spec(d, 4, nb, 0),
            _resident(norm_g.shape), _resident(w_out.shape), _resident((1, d))],
        out_specs=[out, out],
        out_shape=[jax.ShapeDtypeStruct((nb, t, d), F32), jax.ShapeDtypeStruct((nb, t, d), BF16)],
        compiler_params=_cparams(("parallel", "parallel")),
        name="gla_out",
    )(o_f, o_b, r, ctx, x, mod, mod, mod, norm_g, w_out, norm2_g)


def _ffn_kernel(h_ref, hp_ref, hn_ref, x_ref, gate_ref, wi_ref, cw_ref, cb_ref, wo_ref, fg_ref,
                o_ref, hbuf_ref, act_ref, *, chunks, final_norm):
    i = pl.program_id(1)
    tm = h_ref.shape[1]
    dff = act_ref.shape[1]
    hs = SUBLANES
    has_prev = i > 0
    has_next = i < pl.num_programs(1) - 1
    hbuf_ref[0:tm, :] = h_ref[0]
    nxt = jnp.where(has_next, hn_ref[0, 0:hs, :].astype(F32), 0.0)
    prv = jnp.where(has_prev, hp_ref[0, hs:2 * hs, :].astype(F32), 0.0)
    hbuf_ref[tm:, :] = jnp.concatenate([nxt, prv], axis=0).astype(BF16)
    n = tm + 2 * hs

    for (c0, cw) in chunks:
        a = jnp.dot(hbuf_ref[...], wi_ref[0, :, c0:c0 + cw], preferred_element_type=F32)
        val = jnp.dot(hbuf_ref[0:tm, :], wi_ref[0, :, dff + c0:dff + c0 + cw], preferred_element_type=F32)
        w3 = cw_ref[:, c0:c0 + cw]
        conv = (pltpu.roll(a, 1, 0)[0:tm] * w3[0:1] + a[0:tm] * w3[1:2]
                + pltpu.roll(a, n - 1, 0)[0:tm] * w3[2:3] + cb_ref[:, c0:c0 + cw])
        act_ref[:, c0:c0 + cw] = (_silu(conv) * val).astype(BF16)
    y = x_ref[0] + gate_ref[0] * jnp.dot(act_ref[...], wo_ref[0], preferred_element_type=F32)
    if final_norm:
        ms = jnp.mean(y * y, axis=-1, keepdims=True)
        y = y * lax.rsqrt(ms + EPS) * fg_ref[...]
    o_ref[0] = y


def _ffn(h, x, mod, w_in, conv_w, conv_b, w_out, final_g, *, row0, rows, layer, is_ctx, final_norm):
    nb, _, d = x.shape
    t = rows
    dff = w_out.shape[1]
    layer_block = lambda w: pl.BlockSpec((1,) + w.shape[1:], lambda *_: (layer, 0, 0),
                                         pipeline_mode=pl.Buffered(1))
    tm = min(FFN_TM, t)
    hb = 2 * SUBLANES
    bpt = tm // hb
    assert row0 % tm == 0 and rows % tm == 0
    t0 = row0 // tm
    b0 = row0 // hb
    nblk = t // hb
    mod_row = (lambda b: nb) if is_ctx else (lambda b: b)
    chunks = []
    c0 = 0
    while c0 < dff:
        cw = min(FFN_CHUNK, dff - c0)
        chunks.append((c0, cw))
        c0 += cw
    kern = functools.partial(_ffn_kernel, chunks=tuple(chunks), final_norm=final_norm)
    row = pl.BlockSpec((1, tm, d), lambda b, i: (b, t0 + i, 0))
    return pl.pallas_call(
        kern,
        grid=(nb, t // tm),
        in_specs=[
            row,
            pl.BlockSpec((1, hb, d), lambda b, i: (b, b0 + jnp.maximum(i * bpt - 1, 0), 0)),
            pl.BlockSpec((1, hb, d), lambda b, i: (b, b0 + jnp.minimum((i + 1) * bpt, nblk - 1), 0)),
            row,
            pl.BlockSpec((1, 1, d), lambda b, i: (mod_row(b), 0, 5)),
            layer_block(w_in),
            _resident(conv_w.shape),
            _resident(conv_b.shape),
            layer_block(w_out),
            _resident((1, d)),
        ],
        out_specs=pl.BlockSpec((1, tm, d), lambda b, i: (b, i, 0)),
        out_shape=jax.ShapeDtypeStruct((nb, t, d), F32),
        scratch_shapes=[pltpu.VMEM((tm + 2 * SUBLANES, d), BF16), pltpu.VMEM((tm, dff), BF16)],
        compiler_params=_cparams(("parallel", "parallel")),
        name=("ffn_final" if final_norm else "ffn") + ("_ctx" if is_ctx else ""),
    )(h, h, h, x, mod, w_in, conv_w, conv_b, w_out, final_g)


def _na_qkv_kernel(c_ref, x_ref, sh_ref, sc_ref, g_ref, w_ref, q_ref, k_ref, v_ref, *, d, qscale):
    h = _norm_mod(_stream_tile(c_ref, x_ref), g_ref[...], sh_ref[0], sc_ref[0]).astype(BF16)
    proj = jnp.dot(h, w_ref[...], preferred_element_type=F32)
    q_ref[0] = (proj[:, :d] * qscale).astype(BF16)
    k_ref[0] = proj[:, d:2 * d].astype(BF16)
    v_ref[0] = proj[:, 2 * d:].astype(BF16)


def _na_qkv(ctx, x, mod, norm_g, w_qkv, head_dim):
    nb, n, d = x.shape
    t = ctx.shape[1] + n
    row = lambda w: pl.BlockSpec((1, TM, w), lambda b, i: (b, i, 0))
    return pl.pallas_call(
        functools.partial(_na_qkv_kernel, d=d, qscale=LOG2E * float(head_dim) ** -0.5),
        grid=(nb, t // TM),
        in_specs=_stream_specs(d) + [_mod_spec(d, 0, nb, 0), _mod_spec(d, 1, nb, 0),
                                     _resident((1, d)), _resident(w_qkv.shape)],
        out_specs=[row(d)] * 3,
        out_shape=[jax.ShapeDtypeStruct((nb, t, d), BF16)] * 3,
        compiler_params=_cparams(("parallel", "parallel")),
        name="na_qkv",
    )(ctx, x, mod, mod, norm_g, w_qkv)


def _na_attn_kernel(q_ref, k_ref, v_ref, bias_ref, o_ref, *, n_ctx, rows_n, head_dim):
    rb = pl.program_id(2)
    hw = q_ref.shape[-1]
    nwin = NA_KH * GRID_W
    nt = (((1,), (1,)), ((), ()))
    lane = lax.broadcasted_iota(jnp.int32, (GRID_W, hw), 1)
    hmask = [(lane // head_dim) == hd for hd in range(NA_HG)]

    def window(i):
        r = rb * NA_RB + i
        rs = jnp.clip(r - NA_KH // 2, 0, rows_n - NA_KH)
        lo = rs - r + (NA_KH - 1)
        return r, lo, pl.multiple_of(n_ctx + rs * GRID_W, GRID_W)

    def scores(i):
        r, lo, start = window(i)
        q = q_ref[0, pl.ds(pl.multiple_of(n_ctx + r * GRID_W, GRID_W), GRID_W), :]
        qs = jnp.concatenate([jnp.where(hmask[hd], q, jnp.zeros_like(q)) for hd in range(NA_HG)], axis=0)
        bias = jnp.concatenate(
            [jnp.concatenate([bias_ref[0, hd, lo + 2 * m] for m in range(NA_KH // 2)], axis=1)
             for hd in range(NA_HG)], axis=0)
        s_nb = lax.dot_general(qs, k_ref[0, pl.ds(start, nwin), :], nt, preferred_element_type=F32) + bias
        s_cx = lax.dot_general(qs, k_ref[0, 0:n_ctx, :], nt, preferred_element_type=F32)
        s = jnp.concatenate([s_nb, s_cx], axis=1)
        return s, jnp.max(s, axis=-1, keepdims=True)

    def finish(i, s, m):
        r, lo, start = window(i)
        p = jnp.exp2(s - m)
        l = jnp.sum(p, axis=-1, keepdims=True)
        p = p.astype(BF16)
        o = (jnp.dot(p[:, :nwin], v_ref[0, pl.ds(start, nwin), :], preferred_element_type=F32)
             + jnp.dot(p[:, nwin:], v_ref[0, 0:n_ctx, :], preferred_element_type=F32))
        o = o * (1.0 / l)
        out = o[0:GRID_W]
        for hd in range(1, NA_HG):
            out = jnp.where(hmask[hd], o[hd * GRID_W:(hd + 1) * GRID_W], out)
        o_ref[0, pl.ds(pl.multiple_of(i * GRID_W, GRID_W), GRID_W), :] = out.astype(BF16)

    def body(it, carry):
        nxt = scores(it * NA_U)
        for u in range(NA_U):
            cur = nxt
            if u + 1 < NA_U:
                nxt = scores(it * NA_U + u + 1)
            finish(it * NA_U + u, *cur)
        return carry

    lax.fori_loop(0, NA_RB // NA_U, body, 0)


def _na_bias(rpb):
    heads, ndr, ndc = rpb.shape
    c = np.arange(GRID_W)
    cs = np.clip(c - NA_KW // 2, 0, GRID_W - NA_KW)
    onehot = np.zeros((2 * ndc, GRID_W, 2 * GRID_W), np.float32)
    mask = np.full((GRID_W, 2 * GRID_W), NEG, np.float32)
    for half in range(2):
        for cq in range(GRID_W):
            for kc in range(cs[cq], cs[cq] + NA_KW):
                onehot[half * ndc + kc - cq + NA_KW - 1, cq, half * GRID_W + kc] = 1.0
                mask[cq, half * GRID_W + kc] = 0.0
    pair = jnp.concatenate([rpb[:, :ndr - 1, :], rpb[:, 1:, :]], axis=-1)
    tiles = jnp.einsum('hdk,kcl->hdcl', pair * LOG2E, onehot, precision=lax.Precision.HIGHEST) + mask
    return tiles.reshape(heads // NA_HG, NA_HG, ndr - 1, GRID_W, 2 * GRID_W)


def _na_attn(q, k, v, bias, n_ctx, head_dim):
    nb, t, d = q.shape
    rows_n = (t - n_ctx) // GRID_W
    nrb = rows_n // NA_RB
    nq = NA_RB * GRID_W
    hw = NA_HG * head_dim
    ngrp = d // hw
    assert n_ctx % nq == 0 or nq % n_ctx == 0
    kern = functools.partial(_na_attn_kernel, n_ctx=n_ctx, rows_n=rows_n, head_dim=head_dim)
    return pl.pallas_call(
        kern,
        grid=(nb, ngrp, nrb),
        in_specs=[
            pl.BlockSpec((1, t, hw), lambda b, g, r: (b, 0, g)),
            pl.BlockSpec((1, t, hw), lambda b, g, r: (b, 0, g)),
            pl.BlockSpec((1, t, hw), lambda b, g, r: (b, 0, g)),
            pl.BlockSpec((1,) + bias.shape[1:], lambda b, g, r: (g, 0, 0, 0, 0)),
        ],
        out_specs=pl.BlockSpec((1, nq, hw), lambda b, g, r: (b, r, g)),
        out_shape=jax.ShapeDtypeStruct((nb, t - n_ctx, d), BF16),
        compiler_params=_cparams(("parallel", "parallel", "arbitrary")),
        name="na_attn",
    )(q, k, v, bias)


def _na_out_kernel(a_ref, x_ref, gate_ref, sh2_ref, sc2_ref, w_ref, g2_ref, o_ref, h_ref):
    out = jnp.dot(a_ref[0], w_ref[...], preferred_element_type=F32)
    x_mid = x_ref[0] + gate_ref[0] * out
    o_ref[0] = x_mid
    h_ref[0] = _norm_mod(x_mid, g2_ref[...], sh2_ref[0], sc2_ref[0]).astype(BF16)


def _na_out(attn, x, mod, w_out, norm2_g):
    nb, n, d = x.shape
    tm = FFN_TM
    row = pl.BlockSpec((1, tm, d), lambda b, i: (b, i, 0))
    seg = lambda s: pl.BlockSpec((1, 1, d), lambda b, i: (b, 0, s))
    return pl.pallas_call(
        _na_out_kernel,
        grid=(nb, n // tm),
        in_specs=[row, row, seg(2), seg(3), seg(4), _resident(w_out.shape), _resident((1, d))],
        out_specs=[row, row],
        out_shape=[jax.ShapeDtypeStruct((nb, n, d), F32), jax.ShapeDtypeStruct((nb, n, d), BF16)],
        compiler_params=_cparams(("parallel", "parallel")),
        name="na_out",
    )(attn, x, mod, mod, mod, w_out, norm2_g)


def _rope_tables(n_lat, n_ctx):
    half = LANES // 2
    inv = 1.0 / (ROPE_BASE ** (np.arange(0, half, 2, dtype=np.float64) / half))
    pos = np.arange(n_lat)
    rows = (pos // GRID_W).astype(np.float64)[:, None] * inv[None, :]
    cols = (pos % GRID_W).astype(np.float64)[:, None] * inv[None, :]
    cos = np.concatenate([np.cos(rows)] * 2 + [np.cos(cols)] * 2, axis=1)
    sin = np.concatenate([-np.sin(rows), np.sin(rows), -np.sin(cols), np.sin(cols)], axis=1)
    cos = np.concatenate([np.ones((n_ctx, LANES)), cos], axis=0).astype(np.float32)
    sin = np.concatenate([np.zeros((n_ctx, LANES)), sin], axis=0).astype(np.float32)
    return jnp.asarray(cos), jnp.asarray(sin)


def kernel(x, c, ctx, c_ctx, ada_w, ada_b, norm1_g, norm2_g, ffn_w_in, ffn_conv_w, ffn_conv_b, ffn_w_out,
           gla_w_in, gla_a_w1, gla_a_w2, gla_a_b, gla_norm_g, gla_w_out, na_w_qkv, na_rpb, na_w_out, final_g):
    nb, n, d = x.shape
    n_ctx = ctx.shape[1]
    depth = ada_w.shape[0]
    dff = ffn_conv_b.shape[-1]
    assert n_ctx == TM and n % TM == 0 and depth == 2 and nb + 1 <= SUBLANES

    mod = _ada(jnp.concatenate([c, c_ctx[None, :]], axis=0), ada_w, ada_b)

    m0 = mod[0].reshape(SUBLANES, 1, 6 * d)
    dk = gla_a_w2.shape[-1]
    dv = gla_w_out.shape[1]
    assert 4 * 2 * GLA_LOW_RANK == LANES
    aw1 = jnp.concatenate([gla_a_w1[0, 0], gla_a_w1[0, 1]] * 4, axis=1).astype(BF16)
    zeros = jnp.zeros((GLA_LOW_RANK, dk), F32)
    w2 = jnp.concatenate([jnp.concatenate([gla_a_w2[0, 0], zeros], axis=1),
                          jnp.concatenate([zeros, gla_a_w2[0, 1]], axis=1)], axis=0)
    w2_hi = w2.astype(BF16)
    w2_lo = (w2 - w2_hi.astype(F32)).astype(BF16)
    aw2 = jnp.concatenate([w2_hi, w2_hi, w2_lo, w2_lo], axis=0)
    cos_t, sin_t = _rope_tables(n, n_ctx)
    q, k, v, r, gf, gb = _gla_in(ctx, x, m0, norm1_g[0].reshape(1, d), gla_w_in[0].astype(BF16), aw1, aw2,
                                 gla_a_b[0].reshape(1, 2 * dk), cos_t, sin_t, dk, dv)
    o_f, o_b = _gla_core(q, k, v, gf, gb, GLA_HEADS, n_ctx)
    x_mid, h_mid = _gla_out(o_f, o_b, r, ctx, x, m0, gla_norm_g[0].reshape(1, dv // GLA_HEADS),
                            gla_w_out[0].astype(BF16), norm2_g[0].reshape(1, d), GLA_HEADS)
    ffn_wi = ffn_w_in.astype(BF16)
    ffn_wo = ffn_w_out.astype(BF16)
    ffn0 = functools.partial(_ffn, h_mid, x_mid, m0, ffn_wi, ffn_conv_w[0], ffn_conv_b[0].reshape(1, dff),
                             ffn_wo, final_g.reshape(1, d), layer=0, final_norm=False)
    x_lat = ffn0(row0=0, rows=n, is_ctx=False)
    x_ctx = ffn0(row0=n, rows=n_ctx, is_ctx=True)

    m1 = mod[1].reshape(SUBLANES, 1, 6 * d)
    head_dim = d // NA_HEADS
    qn, kn, vn = _na_qkv(x_ctx, x_lat, m1, norm1_g[1].reshape(1, d), na_w_qkv[0].astype(BF16), head_dim)
    bias = _na_bias(na_rpb[0])
    attn = _na_attn(qn, kn, vn, bias, n_ctx, head_dim)
    x_lat, h_lat = _na_out(attn, x_lat, m1, na_w_out[0].astype(BF16), norm2_g[1].reshape(1, d))
    return _ffn(h_lat, x_lat, m1, ffn_wi, ffn_conv_w[1], ffn_conv_b[1].reshape(1, dff), ffn_wo,
                final_g.reshape(1, d), row0=0, rows=n, layer=1, is_ctx=False, final_norm=True)
```

```python
import functools

import numpy as np
import jax
import jax.numpy as jnp
from jax import lax
from jax.experimental import pallas as pl
from jax.experimental.pallas import tpu as pltpu

GRID_W = 64
GLA_HEADS = 4
GLA_GATE_NORM = 16.0
GLA_LOW_RANK = 16
ROPE_BASE = 10000.0
NA_HEADS = 16
NA_KH = 8
NA_KW = 16
EPS = 1e-6
LOG2E = 1.4426950408889634

LANES = 128
SUBLANES = 8
VMEM_LIMIT = 56 * 1024 * 1024

TM = 256
GLA_C = 128
GLA_BLK = 256
GLA_DIAG = 16
FFN_TM = 512
FFN_CHUNK = 512
NA_RB = 16
NA_U = 16
NA_HG = 4
NEG = -1e30

F32 = jnp.float32
BF16 = jnp.bfloat16


def _cparams(sem):
    return pltpu.CompilerParams(dimension_semantics=sem, vmem_limit_bytes=VMEM_LIMIT)


def _resident(shape):
    nd = len(shape)
    return pl.BlockSpec(shape, lambda *_: (0,) * nd, pipeline_mode=pl.Buffered(1))


def _silu(x):
    return x * (1.0 / (1.0 + jnp.exp(-x)))


def _norm_mod(x, g, shift, scale):
    ms = jnp.mean(x * x, axis=-1, keepdims=True)
    y = x * lax.rsqrt(ms + EPS) * g
    return y * (1.0 + scale) + shift


def _ada_kernel(c_ref, w_ref, b_ref, o_ref, *, n_rows):
    tn = w_ref.shape[2]
    rows = []
    for r in range(n_rows):
        sc = _silu(c_ref[r])
        parts = [jnp.sum(w_ref[0, :, j * LANES:(j + 1) * LANES] * sc, axis=0, keepdims=True)
                 for j in range(tn // LANES)]
        rows.append(jnp.concatenate(parts, axis=1) + b_ref[0])
    rows.append(jnp.zeros((SUBLANES - n_rows, tn), F32))
    o_ref[0] = jnp.concatenate(rows, axis=0)


def _ada(c_rows, ada_w, ada_b):
    depth, d, d6 = ada_w.shape
    n_rows = c_rows.shape[0]
    tn = 1536
    c_b = jnp.broadcast_to(c_rows[:, :, None], (n_rows, d, LANES))
    return pl.pallas_call(
        functools.partial(_ada_kernel, n_rows=n_rows),
        grid=(depth, d6 // tn),
        in_specs=[
            pl.BlockSpec((n_rows, d, LANES), lambda i, j: (0, 0, 0)),
            pl.BlockSpec((1, d, tn), lambda i, j: (i, 0, j)),
            pl.BlockSpec((1, 1, tn), lambda i, j: (i, 0, j)),
        ],
        out_specs=pl.BlockSpec((1, SUBLANES, tn), lambda i, j: (i, 0, j)),
        out_shape=jax.ShapeDtypeStruct((depth, SUBLANES, d6), F32),
        compiler_params=_cparams(("parallel", "parallel")),
        name="ada_mod",
    )(c_b, ada_w, ada_b.reshape(depth, 1, d6))


def _mod_spec(d, seg, nb, tile_off):
    def imap(b, i):
        return (jnp.where(i + tile_off == 0, nb, b), 0, seg)
    return pl.BlockSpec((1, 1, d), imap)


def _stream_specs(d):
    return [pl.BlockSpec((1, TM, d), lambda b, i: (b, 0, 0)),
            pl.BlockSpec((1, TM, d), lambda b, i: (b, jnp.maximum(i - 1, 0), 0))]


def _stream_tile(ctx_ref, x_ref):
    return jnp.where(pl.program_id(1) == 0, ctx_ref[0], x_ref[0])


def _cast_specs(arrays, nb, steps):
    specs = []
    for a in arrays:
        rows, cols = a.shape
        tiles = rows // (2 * SUBLANES)
        nblk = max(k for k in range(1, nb * steps + 1) if tiles % k == 0)
        specs.append(pl.BlockSpec((rows // nblk, cols),
                                  lambda b, i, nblk=nblk: (jnp.minimum(b * steps + i, nblk - 1), 0)))
    return specs


def _cast_blocks(src_refs, dst_refs):
    for src, dst in zip(src_refs, dst_refs):
        dst[...] = src[...].astype(BF16)


def _gla_in_kernel(c_ref, x_ref, sh_ref, sc_ref, g_ref, w_ref, aw1_ref, aw2_ref, ab_ref,
                   cos_ref, sin_ref, *rest, dk, dv, n_cast):
    q_ref, k_ref, v_ref, r_ref, gf_ref, gb_ref = rest[n_cast:n_cast + 6]
    _cast_blocks(rest[:n_cast], rest[n_cast + 6:])
    h = _norm_mod(_stream_tile(c_ref, x_ref), g_ref[...], sh_ref[0], sc_ref[0]).astype(BF16)

    z = jnp.dot(h, aw1_ref[...], preferred_element_type=F32)
    z_hi = z.astype(BF16)
    z_lo = (z - z_hi.astype(F32)).astype(BF16)
    grp = lax.broadcasted_iota(jnp.int32, z.shape, 1) // (2 * GLA_LOW_RANK)
    zc = jnp.where((grp & 1) == 0, z_hi, z_lo)
    pre = jnp.dot(zc, aw2_ref[...], preferred_element_type=F32) + ab_ref[...]

    qk = jnp.dot(h, w_ref[:, :2 * dk], preferred_element_type=F32)
    g = (jnp.minimum(pre, 0.0) - jnp.log1p(jnp.exp(-jnp.abs(pre)))) * (LOG2E / GLA_GATE_NORM)
    g_hi = g.astype(BF16)
    g_lo = (g - g_hi.astype(F32)).astype(BF16)
    v_ref[0] = jnp.dot(h, w_ref[:, 2 * dk: 2 * dk + dv], preferred_element_type=F32).astype(BF16)

    cos = cos_ref[...]
    sin = sin_ref[...]
    lane = lax.broadcasted_iota(jnp.int32, cos.shape, 1)
    low = (lane & 32) == 0
    qscale = float(LANES) ** -0.5

    def rope(t):
        partner = jnp.where(low, pltpu.roll(t, LANES - 32, 1), pltpu.roll(t, 32, 1))
        return t * cos + partner * sin

    for hd in range(dk // LANES):
        sl = slice(hd * LANES, (hd + 1) * LANES)
        q_ref[0, :, sl] = rope(qk[:, sl]) * qscale
        k_ref[0, :, sl] = rope(qk[:, dk + hd * LANES: dk + (hd + 1) * LANES])

    ti = lax.broadcasted_iota(jnp.int32, (GLA_C, GLA_C), 0)
    tj = lax.broadcasted_iota(jnp.int32, (GLA_C, GLA_C), 1)
    for o_ref, tri, sl in ((gf_ref, tj <= ti, slice(0, dk)), (gb_ref, tj >= ti, slice(dk, 2 * dk))):
        tri = jnp.where(tri, 1.0, 0.0).astype(BF16)
        tri2 = jnp.concatenate([tri, tri], axis=1)
        for ch in range(TM // GLA_C):
            rows = slice(ch * GLA_C, (ch + 1) * GLA_C)
            o_ref[0, rows, :] = jnp.dot(tri2, jnp.concatenate([g_hi[rows, sl], g_lo[rows, sl]], axis=0),
                                        preferred_element_type=F32)
    r_ref[0] = jnp.dot(h, w_ref[:, 2 * dk + dv:], preferred_element_type=F32).astype(BF16)


def _gla_in(ctx, x, mod, norm_g, w_in, aw1, aw2, ab, cos_t, sin_t, dk, dv, cast):
    nb, n, d = x.shape
    t = ctx.shape[1] + n
    nt = t // TM
    kern = functools.partial(_gla_in_kernel, dk=dk, dv=dv, n_cast=len(cast))
    row = lambda w: pl.BlockSpec((1, TM, w), lambda b, i: (b, i, 0))
    cast_specs = _cast_specs(cast, nb, nt)
    return pl.pallas_call(
        kern,
        grid=(nb, nt),
        in_specs=_stream_specs(d) + [
            _mod_spec(d, 0, nb, 0),
            _mod_spec(d, 1, nb, 0),
            _resident((1, d)),
            _resident(w_in.shape),
            _resident(aw1.shape),
            _resident(aw2.shape),
            _resident(ab.shape),
            pl.BlockSpec((TM, LANES), lambda b, i: (i, 0)),
            pl.BlockSpec((TM, LANES), lambda b, i: (i, 0)),
        ] + cast_specs,
        out_specs=[row(dk), row(dk), row(dv), row(dv), row(dk), row(dk)] + cast_specs,
        out_shape=[
            jax.ShapeDtypeStruct((nb, t, dk), F32),
            jax.ShapeDtypeStruct((nb, t, dk), F32),
            jax.ShapeDtypeStruct((nb, t, dv), BF16),
            jax.ShapeDtypeStruct((nb, t, dv), BF16),
            jax.ShapeDtypeStruct((nb, t, dk), F32),
            jax.ShapeDtypeStruct((nb, t, dk), F32),
        ] + [jax.ShapeDtypeStruct(a.shape, BF16) for a in cast],
        compiler_params=_cparams(("arbitrary", "arbitrary")),
        name="gla_in",
    )(ctx, x, mod, mod, norm_g, w_in, aw1, aw2, ab, cos_t, sin_t, *cast)


def _row_bcast(b, period, offset):
    c, w = b.shape
    if period == c:
        return jnp.broadcast_to(b[offset:offset + 1, :], (c, w))
    b3 = b.reshape(c // period, period, w)
    return jnp.broadcast_to(b3[:, offset:offset + 1, :], b3.shape).reshape(c, w)


def _pair_scores(lhs, rhs, first):
    zero = jnp.zeros_like(rhs)
    blockdiag = jnp.concatenate([jnp.where(first, rhs, zero), jnp.where(first, zero, rhs)], axis=0)
    return lax.dot_general(lhs, blockdiag, (((1,), (1,)), ((), ())), preferred_element_type=F32)


def _gla_pair(q, k, b, rev, kside, same, causal, first):
    c = q.shape[0]
    b_last = b[0:1, :] if rev else b[c - 1:c, :]
    q_in = (q * jnp.exp2(b)).astype(BF16)
    k_out = (k * jnp.exp2(b_last - b)).astype(BF16)
    b_ref = _row_bcast(b, GLA_DIAG, GLA_DIAG - 1 if rev else 0)
    qd = (q * jnp.exp2(b - b_ref)).astype(BF16)
    kd = (k * jnp.exp2(b_ref - b)).astype(BF16)
    att = _pair_scores(qd, kd, first)
    period = 2 * GLA_DIAG
    while period <= c:
        half = period // 2
        b_ref = _row_bcast(b, period, half if rev else half - 1)
        u = (jnp.where(kside[period], k, q) * jnp.exp2(-jnp.abs(b - b_ref))).astype(BF16)
        att = jnp.where(same[half], att, _pair_scores(u, u, first))
        period *= 2
    att = jnp.where(causal, att, 0.0).astype(BF16)
    return q_in, k_out, att, jnp.exp2(b_last)


def _gla_state_step(q_in, att, k_out, v, decay_row, s):
    lhs = jnp.concatenate([q_in, att], axis=1)
    rhs = jnp.concatenate([s.astype(BF16), v], axis=0)
    o = jnp.dot(lhs, rhs, preferred_element_type=F32)
    decay = jnp.transpose(jnp.broadcast_to(decay_row, (LANES, LANES)))
    decay = jnp.concatenate([decay] * (s.shape[1] // LANES), axis=1)
    kv = lax.dot_general(k_out, v, (((0,), (0,)), ((), ())), preferred_element_type=F32)
    return o, s * decay + kv


def _gla_core_kernel(qf_ref, kf_ref, vf_ref, gf_ref, qb_ref, kb_ref, vb_ref, gb_ref,
                     of_ref, ob_ref, s_ref, *, heads, dvh):
    @pl.when(pl.program_id(1) == 0)
    def _():
        s_ref[...] = jnp.zeros_like(s_ref)

    c = GLA_C
    assert c == LANES and heads % 2 == 0
    row = lax.broadcasted_iota(jnp.int32, (c, 2 * LANES), 0)
    first = lax.broadcasted_iota(jnp.int32, (c, 2 * LANES), 1) < LANES
    ti = lax.broadcasted_iota(jnp.int32, (c, 2 * c), 0)
    tj = lax.broadcasted_iota(jnp.int32, (c, 2 * c), 1) & (c - 1)
    x = ti ^ tj
    same = {}
    half = GLA_DIAG
    while half < c:
        same[half] = x < half
        half *= 2

    n_chunks = qf_ref.shape[1] // c
    dirs = ((qf_ref, kf_ref, vf_ref, gf_ref, of_ref), (qb_ref, kb_ref, vb_ref, gb_ref, ob_ref))
    masks = []
    for d in range(2):
        rev = d == 1
        kside = {}
        period = 2 * GLA_DIAG
        while period <= c:
            off = row & (period - 1)
            kside[period] = (off >= period // 2) if rev else (off < period // 2)
            period *= 2
        masks.append((kside, (tj >= ti) if rev else (tj <= ti)))

    for step in range(n_chunks):
        pre = {}
        for d, (q_ref, k_ref, v_ref, b_ref, o_ref) in enumerate(dirs):
            rev = d == 1
            ch = n_chunks - 1 - step if rev else step
            rows = slice(ch * c, (ch + 1) * c)
            for pr in range(heads // 2):
                psl = slice(2 * pr * LANES, (2 * pr + 2) * LANES)
                pre[d, pr] = _gla_pair(q_ref[0, rows, psl], k_ref[0, rows, psl], b_ref[0, rows, psl],
                                       rev, masks[d][0], same, masks[d][1], first)
        for d, (q_ref, k_ref, v_ref, b_ref, o_ref) in enumerate(dirs):
            ch = n_chunks - 1 - step if d == 1 else step
            rows = slice(ch * c, (ch + 1) * c)
            for pr in range(heads // 2):
                q_in, k_out, att, decay = pre[d, pr]
                for j in range(2):
                    hd = 2 * pr + j
                    hsl = slice(j * LANES, (j + 1) * LANES)
                    vsl = slice(hd * dvh, (hd + 1) * dvh)
                    o, s_new = _gla_state_step(q_in[:, hsl], att[:, j * c:(j + 1) * c], k_out[:, hsl],
                                               v_ref[0, rows, vsl], decay[:, hsl], s_ref[d, hd])
                    o_ref[0, rows, vsl] = o.astype(o_ref.dtype)
                    s_ref[d, hd] = s_new


def _gla_core(q, k, v, gf, gb, heads, n_ctx):
    nb, t, dk = q.shape
    dv = v.shape[-1]
    assert n_ctx % GLA_BLK == 0 and t % GLA_BLK == 0
    nc = t // GLA_BLK
    ncx = n_ctx // GLA_BLK
    fwd = lambda b, s: (b, s, 0)
    bwd = lambda b, s: (b, jnp.where(s < ncx, ncx - 1 - s, nc - 1 + ncx - s), 0)
    kern = functools.partial(_gla_core_kernel, heads=heads, dvh=dv // heads)
    blk = lambda w, m: pl.BlockSpec((1, GLA_BLK, w), m)
    return pl.pallas_call(
        kern,
        grid=(nb, nc),
        in_specs=[blk(dk, fwd), blk(dk, fwd), blk(dv, fwd), blk(dk, fwd),
                  blk(dk, bwd), blk(dk, bwd), blk(dv, bwd), blk(dk, bwd)],
        out_specs=[blk(dv, fwd), blk(dv, bwd)],
        out_shape=[jax.ShapeDtypeStruct((nb, t, dv), BF16)] * 2,
        scratch_shapes=[pltpu.VMEM((2, heads, dk // heads, dv // heads), F32)],
        compiler_params=_cparams(("parallel", "arbitrary")),
        name="gla_core",
    )(q, k, v, gf, q, k, v, gb)


def _gla_out_kernel(of_ref, ob_ref, r_ref, c_ref, x_ref, gate_ref, sh2_ref, sc2_ref, ng_ref, w_ref, g2_ref,
                    xo_ref, ho_ref, *, heads):
    o = of_ref[0].astype(F32) + ob_ref[0].astype(F32)
    r = r_ref[0].astype(F32)
    dvh = o.shape[1] // heads
    parts = []
    for hd in range(heads):
        oh = o[:, hd * dvh:(hd + 1) * dvh]
        ms = jnp.mean(oh * oh, axis=-1, keepdims=True)
        parts.append(oh * lax.rsqrt(ms + EPS) * ng_ref[...])
    y = jnp.concatenate(parts, axis=1) * _silu(r)
    out = jnp.dot(y.astype(BF16), w_ref[...], preferred_element_type=F32)
    x_mid = _stream_tile(c_ref, x_ref) + gate_ref[0] * out
    xo_ref[0] = x_mid
    ho_ref[0] = _norm_mod(x_mid, g2_ref[...], sh2_ref[0], sc2_ref[0]).astype(BF16)


def _gla_out(o_f, o_b, r, ctx, x, mod, norm_g, w_out, norm2_g, heads):
    nb, t, dv = o_f.shape
    d = x.shape[-1]
    nt = t // TM
    row = lambda w: pl.BlockSpec((1, TM, w), lambda b, i: (b, i, 0))
    out = pl.BlockSpec((1, TM, d), lambda b, i: (b, jnp.where(i == 0, nt - 1, i - 1), 0))
    return pl.pallas_call(
        functools.partial(_gla_out_kernel, heads=heads),
        grid=(nb, nt),
        in_specs=[row(dv), row(dv), row(dv)] + _stream_specs(d) + [
            _mod_spec(d, 2, nb, 0), _mod_spec(d, 3, nb, 0), _mod_spec(d, 4, nb, 0),
            _resident(norm_g.shape), _resident(w_out.shape), _resident((1, d))],
        out_specs=[out, out],
        out_shape=[jax.ShapeDtypeStruct((nb, t, d), F32), jax.ShapeDtypeStruct((nb, t, d), BF16)],
        compiler_params=_cparams(("parallel", "parallel")),
        name="gla_out",
    )(o_f, o_b, r, ctx, x, mod, mod, mod, norm_g, w_out, norm2_g)


def _ffn_kernel(h_ref, hp_ref, hn_ref, x_ref, gate_ref, wi_ref, cw_ref, cb_ref, wo_ref, fg_ref,
                *rest, chunks, final_norm, n_cast):
    o_ref = rest[n_cast]
    hbuf_ref, act_ref = rest[-2:]
    _cast_blocks(rest[:n_cast], rest[n_cast + 1:-2])
    i = pl.program_id(1)
    tm = h_ref.shape[1]
    dff = act_ref.shape[1]
    hs = SUBLANES
    has_prev = i > 0
    has_next = i < pl.num_programs(1) - 1
    hbuf_ref[0:tm, :] = h_ref[0]
    nxt = jnp.where(has_next, hn_ref[0, 0:hs, :].astype(F32), 0.0)
    prv = jnp.where(has_prev, hp_ref[0, hs:2 * hs, :].astype(F32), 0.0)
    hbuf_ref[tm:, :] = jnp.concatenate([nxt, prv], axis=0).astype(BF16)
    n = tm + 2 * hs

    for (c0, cw) in chunks:
        a = jnp.dot(hbuf_ref[...], wi_ref[:, c0:c0 + cw], preferred_element_type=F32)
        val = jnp.dot(hbuf_ref[0:tm, :], wi_ref[:, dff + c0:dff + c0 + cw], preferred_element_type=F32)
        w3 = cw_ref[:, c0:c0 + cw]
        conv = (pltpu.roll(a, 1, 0)[0:tm] * w3[0:1] + a[0:tm] * w3[1:2]
                + pltpu.roll(a, n - 1, 0)[0:tm] * w3[2:3] + cb_ref[:, c0:c0 + cw])
        act_ref[:, c0:c0 + cw] = (_silu(conv) * val).astype(BF16)
    y = x_ref[0] + gate_ref[0] * jnp.dot(act_ref[...], wo_ref[...], preferred_element_type=F32)
    if final_norm:
        ms = jnp.mean(y * y, axis=-1, keepdims=True)
        y = y * lax.rsqrt(ms + EPS) * fg_ref[...]
    o_ref[0] = y


def _ffn(h, x, mod, w_in, conv_w, conv_b, w_out, final_g, *, row0, rows, is_ctx, final_norm, cast=()):
    nb, _, d = x.shape
    t = rows
    dff = w_out.shape[0]
    tm = min(FFN_TM, t)
    hb = 2 * SUBLANES
    bpt = tm // hb
    assert row0 % tm == 0 and rows % tm == 0
    t0 = row0 // tm
    b0 = row0 // hb
    nblk = t // hb
    mod_row = (lambda b: nb) if is_ctx else (lambda b: b)
    chunks = []
    c0 = 0
    while c0 < dff:
        cw = min(FFN_CHUNK, dff - c0)
        chunks.append((c0, cw))
        c0 += cw
    kern = functools.partial(_ffn_kernel, chunks=tuple(chunks), final_norm=final_norm, n_cast=len(cast))
    row = pl.BlockSpec((1, tm, d), lambda b, i: (b, t0 + i, 0))
    cast_specs = _cast_specs(cast, nb, t // tm)
    out = pl.pallas_call(
        kern,
        grid=(nb, t // tm),
        in_specs=[
            row,
            pl.BlockSpec((1, hb, d), lambda b, i: (b, b0 + jnp.maximum(i * bpt - 1, 0), 0)),
            pl.BlockSpec((1, hb, d), lambda b, i: (b, b0 + jnp.minimum((i + 1) * bpt, nblk - 1), 0)),
            row,
            pl.BlockSpec((1, 1, d), lambda b, i: (mod_row(b), 0, 5)),
            _resident(w_in.shape),
            _resident(conv_w.shape),
            _resident(conv_b.shape),
            _resident(w_out.shape),
            _resident((1, d)),
        ] + cast_specs,
        out_specs=[pl.BlockSpec((1, tm, d), lambda b, i: (b, i, 0))] + cast_specs,
        out_shape=[jax.ShapeDtypeStruct((nb, t, d), F32)] + [jax.ShapeDtypeStruct(a.shape, BF16) for a in cast],
        scratch_shapes=[pltpu.VMEM((tm + 2 * SUBLANES, d), BF16), pltpu.VMEM((tm, dff), BF16)],
        compiler_params=_cparams(("arbitrary", "arbitrary") if cast else ("parallel", "parallel")),
        name=("ffn_final" if final_norm else "ffn") + ("_ctx" if is_ctx else ""),
    )(h, h, h, x, mod, w_in, conv_w, conv_b, w_out, final_g, *cast)
    return out if cast else out[0]


def _na_qkv_kernel(c_ref, x_ref, sh_ref, sc_ref, g_ref, w_ref, q_ref, k_ref, v_ref, *, d, qscale):
    h = _norm_mod(_stream_tile(c_ref, x_ref), g_ref[...], sh_ref[0], sc_ref[0]).astype(BF16)
    proj = jnp.dot(h, w_ref[...], preferred_element_type=F32)
    q_ref[0] = (proj[:, :d] * qscale).astype(BF16)
    k_ref[0] = proj[:, d:2 * d].astype(BF16)
    v_ref[0] = proj[:, 2 * d:].astype(BF16)


def _na_qkv(ctx, x, mod, norm_g, w_qkv, head_dim):
    nb, n, d = x.shape
    t = ctx.shape[1] + n
    row = lambda w: pl.BlockSpec((1, TM, w), lambda b, i: (b, i, 0))
    return pl.pallas_call(
        functools.partial(_na_qkv_kernel, d=d, qscale=LOG2E * float(head_dim) ** -0.5),
        grid=(nb, t // TM),
        in_specs=_stream_specs(d) + [_mod_spec(d, 0, nb, 0), _mod_spec(d, 1, nb, 0),
                                     _resident((1, d)), _resident(w_qkv.shape)],
        out_specs=[row(d)] * 3,
        out_shape=[jax.ShapeDtypeStruct((nb, t, d), BF16)] * 3,
        compiler_params=_cparams(("parallel", "parallel")),
        name="na_qkv",
    )(ctx, x, mod, mod, norm_g, w_qkv)


def _na_attn_kernel(q_ref, k_ref, v_ref, bias_ref, o_ref, *, n_ctx, rows_n, head_dim):
    rb = pl.program_id(2)
    hw = q_ref.shape[-1]
    nwin = NA_KH * GRID_W
    nt = (((1,), (1,)), ((), ()))
    lane = lax.broadcasted_iota(jnp.int32, (GRID_W, hw), 1)
    hmask = [(lane // head_dim) == hd for hd in range(NA_HG)]

    def window(i):
        r = rb * NA_RB + i
        rs = jnp.clip(r - NA_KH // 2, 0, rows_n - NA_KH)
        lo = rs - r + (NA_KH - 1)
        return r, lo, pl.multiple_of(n_ctx + rs * GRID_W, GRID_W)

    def scores(i):
        r, lo, start = window(i)
        q = q_ref[0, pl.ds(pl.multiple_of(n_ctx + r * GRID_W, GRID_W), GRID_W), :]
        qs = jnp.concatenate([jnp.where(hmask[hd], q, jnp.zeros_like(q)) for hd in range(NA_HG)], axis=0)
        bias = jnp.concatenate(
            [jnp.concatenate([bias_ref[0, hd, lo + 2 * m] for m in range(NA_KH // 2)], axis=1)
             for hd in range(NA_HG)], axis=0)
        s_nb = lax.dot_general(qs, k_ref[0, pl.ds(start, nwin), :], nt, preferred_element_type=F32) + bias
        s_cx = lax.dot_general(qs, k_ref[0, 0:n_ctx, :], nt, preferred_element_type=F32)
        s = jnp.concatenate([s_nb, s_cx], axis=1)
        return s, jnp.max(s, axis=-1, keepdims=True)

    def finish(i, s, m):
        r, lo, start = window(i)
        p = jnp.exp2(s - m)
        l = jnp.sum(p, axis=-1, keepdims=True)
        p = p.astype(BF16)
        o = (jnp.dot(p[:, :nwin], v_ref[0, pl.ds(start, nwin), :], preferred_element_type=F32)
             + jnp.dot(p[:, nwin:], v_ref[0, 0:n_ctx, :], preferred_element_type=F32))
        o = o * (1.0 / l)
        out = o[0:GRID_W]
        for hd in range(1, NA_HG):
            out = jnp.where(hmask[hd], o[hd * GRID_W:(hd + 1) * GRID_W], out)
        o_ref[0, pl.ds(pl.multiple_of(i * GRID_W, GRID_W), GRID_W), :] = out.astype(BF16)

    def body(it, carry):
        nxt = scores(it * NA_U)
        for u in range(NA_U):
            cur = nxt
            if u + 1 < NA_U:
                nxt = scores(it * NA_U + u + 1)
            finish(it * NA_U + u, *cur)
        return carry

    lax.fori_loop(0, NA_RB // NA_U, body, 0)


def _na_bias(rpb):
    heads, ndr, ndc = rpb.shape
    c = np.arange(GRID_W)
    cs = np.clip(c - NA_KW // 2, 0, GRID_W - NA_KW)
    onehot = np.zeros((2 * ndc, GRID_W, 2 * GRID_W), np.float32)
    mask = np.full((GRID_W, 2 * GRID_W), NEG, np.float32)
    for half in range(2):
        for cq in range(GRID_W):
            for kc in range(cs[cq], cs[cq] + NA_KW):
                onehot[half * ndc + kc - cq + NA_KW - 1, cq, half * GRID_W + kc] = 1.0
                mask[cq, half * GRID_W + kc] = 0.0
    pair = jnp.concatenate([rpb[:, :ndr - 1, :], rpb[:, 1:, :]], axis=-1)
    tiles = jnp.einsum('hdk,kcl->hdcl', pair * LOG2E, onehot, precision=lax.Precision.HIGHEST) + mask
    return tiles.reshape(heads // NA_HG, NA_HG, ndr - 1, GRID_W, 2 * GRID_W)


def _na_attn(q, k, v, bias, n_ctx, head_dim):
    nb, t, d = q.shape
    rows_n = (t - n_ctx) // GRID_W
    nrb = rows_n // NA_RB
    nq = NA_RB * GRID_W
    hw = NA_HG * head_dim
    ngrp = d // hw
    assert n_ctx % nq == 0 or nq % n_ctx == 0
    kern = functools.partial(_na_attn_kernel, n_ctx=n_ctx, rows_n=rows_n, head_dim=head_dim)
    return pl.pallas_call(
        kern,
        grid=(nb, ngrp, nrb),
        in_specs=[
            pl.BlockSpec((1, t, hw), lambda b, g, r: (b, 0, g)),
            pl.BlockSpec((1, t, hw), lambda b, g, r: (b, 0, g)),
            pl.BlockSpec((1, t, hw), lambda b, g, r: (b, 0, g)),
            pl.BlockSpec((1,) + bias.shape[1:], lambda b, g, r: (g, 0, 0, 0, 0)),
        ],
        out_specs=pl.BlockSpec((1, nq, hw), lambda b, g, r: (b, r, g)),
        out_shape=jax.ShapeDtypeStruct((nb, t - n_ctx, d), BF16),
        compiler_params=_cparams(("parallel", "parallel", "arbitrary")),
        name="na_attn",
    )(q, k, v, bias)


def _na_out_kernel(a_ref, x_ref, gate_ref, sh2_ref, sc2_ref, w_ref, g2_ref, o_ref, h_ref):
    out = jnp.dot(a_ref[0], w_ref[...], preferred_element_type=F32)
    x_mid = x_ref[0] + gate_ref[0] * out
    o_ref[0] = x_mid
    h_ref[0] = _norm_mod(x_mid, g2_ref[...], sh2_ref[0], sc2_ref[0]).astype(BF16)


def _na_out(attn, x, mod, w_out, norm2_g):
    nb, n, d = x.shape
    tm = FFN_TM
    row = pl.BlockSpec((1, tm, d), lambda b, i: (b, i, 0))
    seg = lambda s: pl.BlockSpec((1, 1, d), lambda b, i: (b, 0, s))
    return pl.pallas_call(
        _na_out_kernel,
        grid=(nb, n // tm),
        in_specs=[row, row, seg(2), seg(3), seg(4), _resident(w_out.shape), _resident((1, d))],
        out_specs=[row, row],
        out_shape=[jax.ShapeDtypeStruct((nb, n, d), F32), jax.ShapeDtypeStruct((nb, n, d), BF16)],
        compiler_params=_cparams(("parallel", "parallel")),
        name="na_out",
    )(attn, x, mod, mod, mod, w_out, norm2_g)


def _rope_tables(n_lat, n_ctx):
    half = LANES // 2
    inv = 1.0 / (ROPE_BASE ** (np.arange(0, half, 2, dtype=np.float64) / half))
    pos = np.arange(n_lat)
    rows = (pos // GRID_W).astype(np.float64)[:, None] * inv[None, :]
    cols = (pos % GRID_W).astype(np.float64)[:, None] * inv[None, :]
    cos = np.concatenate([np.cos(rows)] * 2 + [np.cos(cols)] * 2, axis=1)
    sin = np.concatenate([-np.sin(rows), np.sin(rows), -np.sin(cols), np.sin(cols)], axis=1)
    cos = np.concatenate([np.ones((n_ctx, LANES)), cos], axis=0).astype(np.float32)
    sin = np.concatenate([np.zeros((n_ctx, LANES)), sin], axis=0).astype(np.float32)
    return jnp.asarray(cos), jnp.asarray(sin)


def kernel(x, c, ctx, c_ctx, ada_w, ada_b, norm1_g, norm2_g, ffn_w_in, ffn_conv_w, ffn_conv_b, ffn_w_out,
           gla_w_in, gla_a_w1, gla_a_w2, gla_a_b, gla_norm_g, gla_w_out, na_w_qkv, na_rpb, na_w_out, final_g):
    nb, n, d = x.shape
    n_ctx = ctx.shape[1]
    depth = ada_w.shape[0]
    dff = ffn_conv_b.shape[-1]
    assert n_ctx == TM and n % TM == 0 and depth == 2 and nb + 1 <= SUBLANES

    mod = _ada(jnp.concatenate([c, c_ctx[None, :]], axis=0), ada_w, ada_b)

    m0 = mod[0].reshape(SUBLANES, 1, 6 * d)
    dk = gla_a_w2.shape[-1]
    dv = gla_w_out.shape[1]
    assert 4 * 2 * GLA_LOW_RANK == LANES
    aw1 = jnp.concatenate([gla_a_w1[0, 0], gla_a_w1[0, 1]] * 4, axis=1).astype(BF16)
    zeros = jnp.zeros((GLA_LOW_RANK, dk), F32)
    w2 = jnp.concatenate([jnp.concatenate([gla_a_w2[0, 0], zeros], axis=1),
                          jnp.concatenate([zeros, gla_a_w2[0, 1]], axis=1)], axis=0)
    w2_hi = w2.astype(BF16)
    w2_lo = (w2 - w2_hi.astype(F32)).astype(BF16)
    aw2 = jnp.concatenate([w2_hi, w2_hi, w2_lo, w2_lo], axis=0)
    cos_t, sin_t = _rope_tables(n, n_ctx)
    q, k, v, r, gf, gb, ffn_wi0, ffn_wo0 = _gla_in(
        ctx, x, m0, norm1_g[0].reshape(1, d), gla_w_in[0].astype(BF16), aw1, aw2,
        gla_a_b[0].reshape(1, 2 * dk), cos_t, sin_t, dk, dv, cast=(ffn_w_in[0], ffn_w_out[0]))
    o_f, o_b = _gla_core(q, k, v, gf, gb, GLA_HEADS, n_ctx)
    x_mid, h_mid = _gla_out(o_f, o_b, r, ctx, x, m0, gla_norm_g[0].reshape(1, dv // GLA_HEADS),
                            gla_w_out[0].astype(BF16), norm2_g[0].reshape(1, d), GLA_HEADS)
    ffn0 = functools.partial(_ffn, h_mid, x_mid, m0, ffn_wi0, ffn_conv_w[0], ffn_conv_b[0].reshape(1, dff),
                             ffn_wo0, final_g.reshape(1, d), final_norm=False)
    x_lat, ffn_wi1, ffn_wo1, w_qkv, w_na_out = ffn0(
        row0=0, rows=n, is_ctx=False, cast=(ffn_w_in[1], ffn_w_out[1], na_w_qkv[0], na_w_out[0]))
    x_ctx = ffn0(row0=n, rows=n_ctx, is_ctx=True)

    m1 = mod[1].reshape(SUBLANES, 1, 6 * d)
    head_dim = d // NA_HEADS
    qn, kn, vn = _na_qkv(x_ctx, x_lat, m1, norm1_g[1].reshape(1, d), w_qkv, head_dim)
    bias = _na_bias(na_rpb[0])
    attn = _na_attn(qn, kn, vn, bias, n_ctx, head_dim)
    x_lat, h_lat = _na_out(attn, x_lat, m1, w_na_out, norm2_g[1].reshape(1, d))
    return _ffn(h_lat, x_lat, m1, ffn_wi1, ffn_conv_w[1], ffn_conv_b[1].reshape(1, dff), ffn_wo1,
                final_g.reshape(1, d), row0=0, rows=n, is_ctx=False, final_norm=True)
```

```python
import functools

import numpy as np
import jax
import jax.numpy as jnp
from jax import lax
from jax.experimental import pallas as pl
from jax.experimental.pallas import tpu as pltpu

GRID_W = 64
GLA_HEADS = 4
GLA_GATE_NORM = 16.0
GLA_LOW_RANK = 16
ROPE_BASE = 10000.0
NA_HEADS = 16
NA_KH = 8
NA_KW = 16
EPS = 1e-6
LOG2E = 1.4426950408889634

LANES = 128
SUBLANES = 8
VMEM_LIMIT = 56 * 1024 * 1024

TM = 256
GLA_C = 128
GLA_BLK = 256
GLA_DIAG = 16
FFN_TM = 512
FFN_CHUNK = 512
NA_RB = 32
NA_U = 16
NA_HG = 4
NEG = -1e30

F32 = jnp.float32
BF16 = jnp.bfloat16


def _cparams(sem):
    return pltpu.CompilerParams(dimension_semantics=sem, vmem_limit_bytes=VMEM_LIMIT)


def _resident(shape):
    nd = len(shape)
    return pl.BlockSpec(shape, lambda *_: (0,) * nd, pipeline_mode=pl.Buffered(1))


def _silu(x):
    return x * (1.0 / (1.0 + jnp.exp(-x)))


def _norm_mod(x, g, shift, scale):
    ms = jnp.mean(x * x, axis=-1, keepdims=True)
    y = x * lax.rsqrt(ms + EPS) * g
    return y * (1.0 + scale) + shift


def _ada_kernel(c_ref, w_ref, b_ref, o_ref, *, n_rows):
    tn = w_ref.shape[2]
    rows = []
    for r in range(n_rows):
        sc = _silu(c_ref[r])
        parts = [jnp.sum(w_ref[0, :, j * LANES:(j + 1) * LANES] * sc, axis=0, keepdims=True)
                 for j in range(tn // LANES)]
        rows.append(jnp.concatenate(parts, axis=1) + b_ref[0])
    rows.append(jnp.zeros((SUBLANES - n_rows, tn), F32))
    o_ref[0] = jnp.concatenate(rows, axis=0)


def _ada(c_rows, ada_w, ada_b):
    depth, d, d6 = ada_w.shape
    n_rows = c_rows.shape[0]
    tn = 1536
    c_b = jnp.broadcast_to(c_rows[:, :, None], (n_rows, d, LANES))
    return pl.pallas_call(
        functools.partial(_ada_kernel, n_rows=n_rows),
        grid=(depth, d6 // tn),
        in_specs=[
            pl.BlockSpec((n_rows, d, LANES), lambda i, j: (0, 0, 0)),
            pl.BlockSpec((1, d, tn), lambda i, j: (i, 0, j)),
            pl.BlockSpec((1, 1, tn), lambda i, j: (i, 0, j)),
        ],
        out_specs=pl.BlockSpec((1, SUBLANES, tn), lambda i, j: (i, 0, j)),
        out_shape=jax.ShapeDtypeStruct((depth, SUBLANES, d6), F32),
        compiler_params=_cparams(("parallel", "parallel")),
        name="ada_mod",
    )(c_b, ada_w, ada_b.reshape(depth, 1, d6))


def _mod_spec(d, seg, nb, tile_off):
    def imap(b, i):
        return (jnp.where(i + tile_off == 0, nb, b), 0, seg)
    return pl.BlockSpec((1, 1, d), imap)


def _stream_specs(d):
    return [pl.BlockSpec((1, TM, d), lambda b, i: (b, 0, 0)),
            pl.BlockSpec((1, TM, d), lambda b, i: (b, jnp.maximum(i - 1, 0), 0))]


def _stream_tile(ctx_ref, x_ref):
    return jnp.where(pl.program_id(1) == 0, ctx_ref[0], x_ref[0])


def _cast_specs(arrays, nb, steps):
    in_specs, out_specs = [], []
    for a, layer in arrays:
        _, rows, cols = a.shape
        tiles = rows // (2 * SUBLANES)
        nblk = max(k for k in range(1, nb * steps + 1) if tiles % k == 0)
        in_specs.append(pl.BlockSpec(
            (1, rows // nblk, cols),
            lambda b, i, nblk=nblk, layer=layer: (layer, jnp.minimum(b * steps + i, nblk - 1), 0)))
        out_specs.append(pl.BlockSpec(
            (rows // nblk, cols), lambda b, i, nblk=nblk: (jnp.minimum(b * steps + i, nblk - 1), 0)))
    return in_specs, out_specs


def _cast_blocks(src_refs, dst_refs):
    for src, dst in zip(src_refs, dst_refs):
        dst[...] = src[0].astype(BF16)


def _gla_in_kernel(c_ref, x_ref, sh_ref, sc_ref, g_ref, w_ref, aw1_ref, aw2_ref, ab_ref,
                   cos_ref, sin_ref, *rest, dk, dv, n_cast):
    q_ref, k_ref, v_ref, r_ref, gf_ref, gb_ref = rest[n_cast:n_cast + 6]
    _cast_blocks(rest[:n_cast], rest[n_cast + 6:])
    h = _norm_mod(_stream_tile(c_ref, x_ref), g_ref[...], sh_ref[0], sc_ref[0]).astype(BF16)

    z = jnp.dot(h, aw1_ref[...], preferred_element_type=F32)
    z_hi = z.astype(BF16)
    z_lo = (z - z_hi.astype(F32)).astype(BF16)
    grp = lax.broadcasted_iota(jnp.int32, z.shape, 1) // (2 * GLA_LOW_RANK)
    zc = jnp.where((grp & 1) == 0, z_hi, z_lo)
    pre = jnp.dot(zc, aw2_ref[...], preferred_element_type=F32) + ab_ref[...]

    qk = jnp.dot(h, w_ref[:, :2 * dk], preferred_element_type=F32)
    g = (jnp.minimum(pre, 0.0) - jnp.log1p(jnp.exp(-jnp.abs(pre)))) * (LOG2E / GLA_GATE_NORM)
    g_hi = g.astype(BF16)
    g_lo = (g - g_hi.astype(F32)).astype(BF16)
    v_ref[0] = jnp.dot(h, w_ref[:, 2 * dk: 2 * dk + dv], preferred_element_type=F32).astype(BF16)

    cos = cos_ref[...]
    sin = sin_ref[...]
    lane = lax.broadcasted_iota(jnp.int32, cos.shape, 1)
    low = (lane & 32) == 0
    qscale = float(LANES) ** -0.5

    def rope(t):
        partner = jnp.where(low, pltpu.roll(t, LANES - 32, 1), pltpu.roll(t, 32, 1))
        return t * cos + partner * sin

    for hd in range(dk // LANES):
        sl = slice(hd * LANES, (hd + 1) * LANES)
        q_ref[0, :, sl] = rope(qk[:, sl]) * qscale
        k_ref[0, :, sl] = rope(qk[:, dk + hd * LANES: dk + (hd + 1) * LANES])

    ti = lax.broadcasted_iota(jnp.int32, (GLA_C, GLA_C), 0)
    tj = lax.broadcasted_iota(jnp.int32, (GLA_C, GLA_C), 1)
    for o_ref, tri, sl in ((gf_ref, tj <= ti, slice(0, dk)), (gb_ref, tj >= ti, slice(dk, 2 * dk))):
        tri = jnp.where(tri, 1.0, 0.0).astype(BF16)
        tri2 = jnp.concatenate([tri, tri], axis=1)
        for ch in range(TM // GLA_C):
            rows = slice(ch * GLA_C, (ch + 1) * GLA_C)
            o_ref[0, rows, :] = jnp.dot(tri2, jnp.concatenate([g_hi[rows, sl], g_lo[rows, sl]], axis=0),
                                        preferred_element_type=F32)
    r_ref[0] = jnp.dot(h, w_ref[:, 2 * dk + dv:], preferred_element_type=F32).astype(BF16)


def _gla_in(ctx, x, mod, norm_g, w_in, aw1, aw2, ab, cos_t, sin_t, dk, dv, cast):
    nb, n, d = x.shape
    t = ctx.shape[1] + n
    nt = t // TM
    kern = functools.partial(_gla_in_kernel, dk=dk, dv=dv, n_cast=len(cast))
    row = lambda w: pl.BlockSpec((1, TM, w), lambda b, i: (b, i, 0))
    cast_in, cast_out = _cast_specs(cast, nb, nt)
    return pl.pallas_call(
        kern,
        grid=(nb, nt),
        in_specs=_stream_specs(d) + [
            _mod_spec(d, 0, nb, 0),
            _mod_spec(d, 1, nb, 0),
            _resident((1, d)),
            _resident(w_in.shape),
            _resident(aw1.shape),
            _resident(aw2.shape),
            _resident(ab.shape),
            pl.BlockSpec((TM, LANES), lambda b, i: (i, 0)),
            pl.BlockSpec((TM, LANES), lambda b, i: (i, 0)),
        ] + cast_in,
        out_specs=[row(dk), row(dk), row(dv), row(dv), row(dk), row(dk)] + cast_out,
        out_shape=[
            jax.ShapeDtypeStruct((nb, t, dk), F32),
            jax.ShapeDtypeStruct((nb, t, dk), F32),
            jax.ShapeDtypeStruct((nb, t, dv), BF16),
            jax.ShapeDtypeStruct((nb, t, dv), BF16),
            jax.ShapeDtypeStruct((nb, t, dk), F32),
            jax.ShapeDtypeStruct((nb, t, dk), F32),
        ] + [jax.ShapeDtypeStruct(a.shape[1:], BF16) for a, _ in cast],
        compiler_params=_cparams(("arbitrary", "arbitrary")),
        name="gla_in",
    )(ctx, x, mod, mod, norm_g, w_in, aw1, aw2, ab, cos_t, sin_t, *[a for a, _ in cast])


def _row_bcast(b, period, offset):
    c, w = b.shape
    if period == c:
        return jnp.broadcast_to(b[offset:offset + 1, :], (c, w))
    b3 = b.reshape(c // period, period, w)
    return jnp.broadcast_to(b3[:, offset:offset + 1, :], b3.shape).reshape(c, w)


def _pair_scores(lhs, rhs, first):
    zero = jnp.zeros_like(rhs)
    blockdiag = jnp.concatenate([jnp.where(first, rhs, zero), jnp.where(first, zero, rhs)], axis=0)
    return lax.dot_general(lhs, blockdiag, (((1,), (1,)), ((), ())), preferred_element_type=F32)


def _gla_pair(q, k, b, rev, kside, same, causal, first):
    c = q.shape[0]
    b_last = b[0:1, :] if rev else b[c - 1:c, :]
    q_in = (q * jnp.exp2(b)).astype(BF16)
    k_out = (k * jnp.exp2(b_last - b)).astype(BF16)
    b_ref = _row_bcast(b, GLA_DIAG, GLA_DIAG - 1 if rev else 0)
    qd = (q * jnp.exp2(b - b_ref)).astype(BF16)
    kd = (k * jnp.exp2(b_ref - b)).astype(BF16)
    att = _pair_scores(qd, kd, first)
    period = 2 * GLA_DIAG
    while period <= c:
        half = period // 2
        b_ref = _row_bcast(b, period, half if rev else half - 1)
        u = (jnp.where(kside[period], k, q) * jnp.exp2(-jnp.abs(b - b_ref))).astype(BF16)
        att = jnp.where(same[half], att, _pair_scores(u, u, first))
        period *= 2
    att = jnp.where(causal, att, 0.0).astype(BF16)
    return q_in, k_out, att, jnp.exp2(b_last)


def _gla_state_step(q_in, att, k_out, v, decay_row, s):
    lhs = jnp.concatenate([q_in, att], axis=1)
    rhs = jnp.concatenate([s.astype(BF16), v], axis=0)
    o = jnp.dot(lhs, rhs, preferred_element_type=F32)
    decay = jnp.transpose(jnp.broadcast_to(decay_row, (LANES, LANES)))
    decay = jnp.concatenate([decay] * (s.shape[1] // LANES), axis=1)
    kv = lax.dot_general(k_out, v, (((0,), (0,)), ((), ())), preferred_element_type=F32)
    return o, s * decay + kv


def _gla_core_kernel(qf_ref, kf_ref, vf_ref, gf_ref, qb_ref, kb_ref, vb_ref, gb_ref,
                     of_ref, ob_ref, s_ref, *, heads, dvh):
    @pl.when(pl.program_id(1) == 0)
    def _():
        s_ref[...] = jnp.zeros_like(s_ref)

    c = GLA_C
    assert c == LANES and heads % 2 == 0
    row = lax.broadcasted_iota(jnp.int32, (c, 2 * LANES), 0)
    first = lax.broadcasted_iota(jnp.int32, (c, 2 * LANES), 1) < LANES
    ti = lax.broadcasted_iota(jnp.int32, (c, 2 * c), 0)
    tj = lax.broadcasted_iota(jnp.int32, (c, 2 * c), 1) & (c - 1)
    x = ti ^ tj
    same = {}
    half = GLA_DIAG
    while half < c:
        same[half] = x < half
        half *= 2

    n_chunks = qf_ref.shape[1] // c
    dirs = ((qf_ref, kf_ref, vf_ref, gf_ref, of_ref), (qb_ref, kb_ref, vb_ref, gb_ref, ob_ref))
    masks = []
    for d in range(2):
        rev = d == 1
        kside = {}
        period = 2 * GLA_DIAG
        while period <= c:
            off = row & (period - 1)
            kside[period] = (off >= period // 2) if rev else (off < period // 2)
            period *= 2
        masks.append((kside, (tj >= ti) if rev else (tj <= ti)))

    for step in range(n_chunks):
        pre = {}
        for d, (q_ref, k_ref, v_ref, b_ref, o_ref) in enumerate(dirs):
            rev = d == 1
            ch = n_chunks - 1 - step if rev else step
            rows = slice(ch * c, (ch + 1) * c)
            for pr in range(heads // 2):
                psl = slice(2 * pr * LANES, (2 * pr + 2) * LANES)
                pre[d, pr] = _gla_pair(q_ref[0, rows, psl], k_ref[0, rows, psl], b_ref[0, rows, psl],
                                       rev, masks[d][0], same, masks[d][1], first)
        for d, (q_ref, k_ref, v_ref, b_ref, o_ref) in enumerate(dirs):
            ch = n_chunks - 1 - step if d == 1 else step
            rows = slice(ch * c, (ch + 1) * c)
            for pr in range(heads // 2):
                q_in, k_out, att, decay = pre[d, pr]
                for j in range(2):
                    hd = 2 * pr + j
                    hsl = slice(j * LANES, (j + 1) * LANES)
                    vsl = slice(hd * dvh, (hd + 1) * dvh)
                    o, s_new = _gla_state_step(q_in[:, hsl], att[:, j * c:(j + 1) * c], k_out[:, hsl],
                                               v_ref[0, rows, vsl], decay[:, hsl], s_ref[d, hd])
                    o_ref[0, rows, vsl] = o.astype(o_ref.dtype)
                    s_ref[d, hd] = s_new


def _gla_core(q, k, v, gf, gb, heads, n_ctx):
    nb, t, dk = q.shape
    dv = v.shape[-1]
    assert n_ctx % GLA_BLK == 0 and t % GLA_BLK == 0
    nc = t // GLA_BLK
    ncx = n_ctx // GLA_BLK
    fwd = lambda b, s: (b, s, 0)
    bwd = lambda b, s: (b, jnp.where(s < ncx, ncx - 1 - s, nc - 1 + ncx - s), 0)
    kern = functools.partial(_gla_core_kernel, heads=heads, dvh=dv // heads)
    blk = lambda w, m: pl.BlockSpec((1, GLA_BLK, w), m)
    return pl.pallas_call(
        kern,
        grid=(nb, nc),
        in_specs=[blk(dk, fwd), blk(dk, fwd), blk(dv, fwd), blk(dk, fwd),
                  blk(dk, bwd), blk(dk, bwd), blk(dv, bwd), blk(dk, bwd)],
        out_specs=[blk(dv, fwd), blk(dv, bwd)],
        out_shape=[jax.ShapeDtypeStruct((nb, t, dv), BF16)] * 2,
        scratch_shapes=[pltpu.VMEM((2, heads, dk // heads, dv // heads), F32)],
        compiler_params=_cparams(("parallel", "arbitrary")),
        name="gla_core",
    )(q, k, v, gf, q, k, v, gb)


def _gla_out_kernel(of_ref, ob_ref, r_ref, c_ref, x_ref, gate_ref, sh2_ref, sc2_ref, ng_ref, w_ref, g2_ref,
                    xo_ref, ho_ref, *, heads):
    o = of_ref[0].astype(F32) + ob_ref[0].astype(F32)
    r = r_ref[0].astype(F32)
    dvh = o.shape[1] // heads
    parts = []
    for hd in range(heads):
        oh = o[:, hd * dvh:(hd + 1) * dvh]
        ms = jnp.mean(oh * oh, axis=-1, keepdims=True)
        parts.append(oh * lax.rsqrt(ms + EPS) * ng_ref[...])
    y = jnp.concatenate(parts, axis=1) * _silu(r)
    out = jnp.dot(y.astype(BF16), w_ref[...], preferred_element_type=F32)
    x_mid = _stream_tile(c_ref, x_ref) + gate_ref[0] * out
    xo_ref[0] = x_mid
    ho_ref[0] = _norm_mod(x_mid, g2_ref[...], sh2_ref[0], sc2_ref[0]).astype(BF16)


def _gla_out(o_f, o_b, r, ctx, x, mod, norm_g, w_out, norm2_g, heads):
    nb, t, dv = o_f.shape
    d = x.shape[-1]
    nt = t // TM
    row = lambda w: pl.BlockSpec((1, TM, w), lambda b, i: (b, i, 0))
    out = pl.BlockSpec((1, TM, d), lambda b, i: (b, jnp.where(i == 0, nt - 1, i - 1), 0))
    return pl.pallas_call(
        functools.partial(_gla_out_kernel, heads=heads),
        grid=(nb, nt),
        in_specs=[row(dv), row(dv), row(dv)] + _stream_specs(d) + [
            _mod_spec(d, 2, nb, 0), _mod_spec(d, 3, nb, 0), _mod_spec(d, 4, nb, 0),
            _resident(norm_g.shape), _resident(w_out.shape), _resident((1, d))],
        out_specs=[out, out],
        out_shape=[jax.ShapeDtypeStruct((nb, t, d), F32), jax.ShapeDtypeStruct((nb, t, d), BF16)],
        compiler_params=_cparams(("parallel", "parallel")),
        name="gla_out",
    )(o_f, o_b, r, ctx, x, mod, mod, mod, norm_g, w_out, norm2_g)


def _ffn_kernel(h_ref, hp_ref, hn_ref, x_ref, gate_ref, wi_ref, cw_ref, cb_ref, wo_ref, fg_ref,
                *rest, chunks, final_norm, n_cast):
    o_ref = rest[n_cast]
    hbuf_ref, act_ref = rest[-2:]
    _cast_blocks(rest[:n_cast], rest[n_cast + 1:-2])
    i = pl.program_id(1)
    tm = h_ref.shape[1]
    dff = act_ref.shape[1]
    hs = SUBLANES
    has_prev = i > 0
    has_next = i < pl.num_programs(1) - 1
    hbuf_ref[0:tm, :] = h_ref[0]
    nxt = jnp.where(has_next, hn_ref[0, 0:hs, :].astype(F32), 0.0)
    prv = jnp.where(has_prev, hp_ref[0, hs:2 * hs, :].astype(F32), 0.0)
    hbuf_ref[tm:, :] = jnp.concatenate([nxt, prv], axis=0).astype(BF16)
    n = tm + 2 * hs

    for (c0, cw) in chunks:
        a = jnp.dot(hbuf_ref[...], wi_ref[:, c0:c0 + cw], preferred_element_type=F32)
        val = jnp.dot(hbuf_ref[0:tm, :], wi_ref[:, dff + c0:dff + c0 + cw], preferred_element_type=F32)
        w3 = cw_ref[:, c0:c0 + cw]
        conv = (pltpu.roll(a, 1, 0)[0:tm] * w3[0:1] + a[0:tm] * w3[1:2]
                + pltpu.roll(a, n - 1, 0)[0:tm] * w3[2:3] + cb_ref[:, c0:c0 + cw])
        act_ref[:, c0:c0 + cw] = (_silu(conv) * val).astype(BF16)
    y = x_ref[0] + gate_ref[0] * jnp.dot(act_ref[...], wo_ref[...], preferred_element_type=F32)
    if final_norm:
        ms = jnp.mean(y * y, axis=-1, keepdims=True)
        y = y * lax.rsqrt(ms + EPS) * fg_ref[...]
    o_ref[0] = y


def _ffn(h, x, mod, w_in, conv_w, conv_b, w_out, final_g, *, row0, rows, is_ctx, final_norm, cast=()):
    nb, _, d = x.shape
    t = rows
    dff = w_out.shape[0]
    tm = min(FFN_TM, t)
    hb = 2 * SUBLANES
    bpt = tm // hb
    assert row0 % tm == 0 and rows % tm == 0
    t0 = row0 // tm
    b0 = row0 // hb
    nblk = t // hb
    mod_row = (lambda b: nb) if is_ctx else (lambda b: b)
    chunks = []
    c0 = 0
    while c0 < dff:
        cw = min(FFN_CHUNK, dff - c0)
        chunks.append((c0, cw))
        c0 += cw
    kern = functools.partial(_ffn_kernel, chunks=tuple(chunks), final_norm=final_norm, n_cast=len(cast))
    row = pl.BlockSpec((1, tm, d), lambda b, i: (b, t0 + i, 0))
    cast_in, cast_out = _cast_specs(cast, nb, t // tm)
    out = pl.pallas_call(
        kern,
        grid=(nb, t // tm),
        in_specs=[
            row,
            pl.BlockSpec((1, hb, d), lambda b, i: (b, b0 + jnp.maximum(i * bpt - 1, 0), 0)),
            pl.BlockSpec((1, hb, d), lambda b, i: (b, b0 + jnp.minimum((i + 1) * bpt, nblk - 1), 0)),
            row,
            pl.BlockSpec((1, 1, d), lambda b, i: (mod_row(b), 0, 5)),
            _resident(w_in.shape),
            _resident(conv_w.shape),
            _resident(conv_b.shape),
            _resident(w_out.shape),
            _resident((1, d)),
        ] + cast_in,
        out_specs=[pl.BlockSpec((1, tm, d), lambda b, i: (b, i, 0))] + cast_out,
        out_shape=[jax.ShapeDtypeStruct((nb, t, d), F32)]
        + [jax.ShapeDtypeStruct(a.shape[1:], BF16) for a, _ in cast],
        scratch_shapes=[pltpu.VMEM((tm + 2 * SUBLANES, d), BF16), pltpu.VMEM((tm, dff), BF16)],
        compiler_params=_cparams(("arbitrary", "arbitrary") if cast else ("parallel", "parallel")),
        name=("ffn_final" if final_norm else "ffn") + ("_ctx" if is_ctx else ""),
    )(h, h, h, x, mod, w_in, conv_w, conv_b, w_out, final_g, *[a for a, _ in cast])
    return out if cast else out[0]


def _na_qkv_kernel(c_ref, x_ref, sh_ref, sc_ref, g_ref, w_ref, q_ref, k_ref, v_ref, *, d, qscale):
    h = _norm_mod(_stream_tile(c_ref, x_ref), g_ref[...], sh_ref[0], sc_ref[0]).astype(BF16)
    proj = jnp.dot(h, w_ref[...], preferred_element_type=F32)
    q_ref[0] = (proj[:, :d] * qscale).astype(BF16)
    k_ref[0] = proj[:, d:2 * d].astype(BF16)
    v_ref[0] = proj[:, 2 * d:].astype(BF16)


def _na_qkv(ctx, x, mod, norm_g, w_qkv, head_dim):
    nb, n, d = x.shape
    t = ctx.shape[1] + n
    row = lambda w: pl.BlockSpec((1, TM, w), lambda b, i: (b, i, 0))
    return pl.pallas_call(
        functools.partial(_na_qkv_kernel, d=d, qscale=LOG2E * float(head_dim) ** -0.5),
        grid=(nb, t // TM),
        in_specs=_stream_specs(d) + [_mod_spec(d, 0, nb, 0), _mod_spec(d, 1, nb, 0),
                                     _resident((1, d)), _resident(w_qkv.shape)],
        out_specs=[row(d)] * 3,
        out_shape=[jax.ShapeDtypeStruct((nb, t, d), BF16)] * 3,
        compiler_params=_cparams(("parallel", "parallel")),
        name="na_qkv",
    )(ctx, x, mod, mod, norm_g, w_qkv)


def _na_attn_kernel(q_ref, k_ref, v_ref, bias_ref, o_ref, *, n_ctx, rows_n, head_dim):
    rb = pl.program_id(2)
    hw = q_ref.shape[-1]
    nwin = NA_KH * GRID_W
    nt = (((1,), (1,)), ((), ()))
    lane = lax.broadcasted_iota(jnp.int32, (GRID_W, hw), 1)
    hmask = [(lane // head_dim) == hd for hd in range(NA_HG)]

    def window(i):
        r = rb * NA_RB + i
        rs = jnp.clip(r - NA_KH // 2, 0, rows_n - NA_KH)
        lo = rs - r + (NA_KH - 1)
        return r, lo, pl.multiple_of(n_ctx + rs * GRID_W, GRID_W)

    def scores(i):
        r, lo, start = window(i)
        q = q_ref[0, pl.ds(pl.multiple_of(n_ctx + r * GRID_W, GRID_W), GRID_W), :]
        qs = jnp.concatenate([jnp.where(hmask[hd], q, jnp.zeros_like(q)) for hd in range(NA_HG)], axis=0)
        bias = jnp.concatenate(
            [jnp.concatenate([bias_ref[0, hd, lo + 2 * m] for m in range(NA_KH // 2)], axis=1)
             for hd in range(NA_HG)], axis=0)
        s_nb = lax.dot_general(qs, k_ref[0, pl.ds(start, nwin), :], nt, preferred_element_type=F32) + bias
        s_cx = lax.dot_general(qs, k_ref[0, 0:n_ctx, :], nt, preferred_element_type=F32)
        s = jnp.concatenate([s_nb, s_cx], axis=1)
        return s, jnp.max(s, axis=-1, keepdims=True)

    def finish(i, s, m):
        r, lo, start = window(i)
        p = jnp.exp2(s - m)
        l = jnp.sum(p, axis=-1, keepdims=True)
        p = p.astype(BF16)
        o = (jnp.dot(p[:, :nwin], v_ref[0, pl.ds(start, nwin), :], preferred_element_type=F32)
             + jnp.dot(p[:, nwin:], v_ref[0, 0:n_ctx, :], preferred_element_type=F32))
        o = o * (1.0 / l)
        out = o[0:GRID_W]
        for hd in range(1, NA_HG):
            out = jnp.where(hmask[hd], o[hd * GRID_W:(hd + 1) * GRID_W], out)
        o_ref[0, pl.ds(pl.multiple_of(i * GRID_W, GRID_W), GRID_W), :] = out.astype(BF16)

    def body(it, carry):
        nxt = scores(it * NA_U)
        for u in range(NA_U):
            cur = nxt
            if u + 1 < NA_U:
                nxt = scores(it * NA_U + u + 1)
            finish(it * NA_U + u, *cur)
        return carry

    lax.fori_loop(0, NA_RB // NA_U, body, 0)


def _na_bias(rpb):
    heads, ndr, ndc = rpb.shape
    c = np.arange(GRID_W)
    cs = np.clip(c - NA_KW // 2, 0, GRID_W - NA_KW)
    onehot = np.zeros((2 * ndc, GRID_W, 2 * GRID_W), np.float32)
    mask = np.full((GRID_W, 2 * GRID_W), NEG, np.float32)
    for half in range(2):
        for cq in range(GRID_W):
            for kc in range(cs[cq], cs[cq] + NA_KW):
                onehot[half * ndc + kc - cq + NA_KW - 1, cq, half * GRID_W + kc] = 1.0
                mask[cq, half * GRID_W + kc] = 0.0
    pair = jnp.concatenate([rpb[:, :ndr - 1, :], rpb[:, 1:, :]], axis=-1)
    tiles = jnp.einsum('hdk,kcl->hdcl', pair * LOG2E, onehot, precision=lax.Precision.HIGHEST) + mask
    return tiles.reshape(heads // NA_HG, NA_HG, ndr - 1, GRID_W, 2 * GRID_W)


def _na_attn(q, k, v, bias, n_ctx, head_dim):
    nb, t, d = q.shape
    rows_n = (t - n_ctx) // GRID_W
    nrb = rows_n // NA_RB
    nq = NA_RB * GRID_W
    hw = NA_HG * head_dim
    ngrp = d // hw
    assert n_ctx % nq == 0 or nq % n_ctx == 0
    kern = functools.partial(_na_attn_kernel, n_ctx=n_ctx, rows_n=rows_n, head_dim=head_dim)
    return pl.pallas_call(
        kern,
        grid=(nb, ngrp, nrb),
        in_specs=[
            pl.BlockSpec((1, t, hw), lambda b, g, r: (b, 0, g)),
            pl.BlockSpec((1, t, hw), lambda b, g, r: (b, 0, g)),
            pl.BlockSpec((1, t, hw), lambda b, g, r: (b, 0, g)),
            pl.BlockSpec((1,) + bias.shape[1:], lambda b, g, r: (g, 0, 0, 0, 0)),
        ],
        out_specs=pl.BlockSpec((1, nq, hw), lambda b, g, r: (b, r, g)),
        out_shape=jax.ShapeDtypeStruct((nb, t - n_ctx, d), BF16),
        compiler_params=_cparams(("parallel", "parallel", "arbitrary")),
        name="na_attn",
    )(q, k, v, bias)


def _na_out_kernel(a_ref, x_ref, gate_ref, sh2_ref, sc2_ref, w_ref, g2_ref, o_ref, h_ref):
    out = jnp.dot(a_ref[0], w_ref[...], preferred_element_type=F32)
    x_mid = x_ref[0] + gate_ref[0] * out
    o_ref[0] = x_mid
    h_ref[0] = _norm_mod(x_mid, g2_ref[...], sh2_ref[0], sc2_ref[0]).astype(BF16)


def _na_out(attn, x, mod, w_out, norm2_g):
    nb, n, d = x.shape
    tm = FFN_TM
    row = pl.BlockSpec((1, tm, d), lambda b, i: (b, i, 0))
    seg = lambda s: pl.BlockSpec((1, 1, d), lambda b, i: (b, 0, s))
    return pl.pallas_call(
        _na_out_kernel,
        grid=(nb, n // tm),
        in_specs=[row, row, seg(2), seg(3), seg(4), _resident(w_out.shape), _resident((1, d))],
        out_specs=[row, row],
        out_shape=[jax.ShapeDtypeStruct((nb, n, d), F32), jax.ShapeDtypeStruct((nb, n, d), BF16)],
        compiler_params=_cparams(("parallel", "parallel")),
        name="na_out",
    )(attn, x, mod, mod, mod, w_out, norm2_g)


def _rope_tables(n_lat, n_ctx):
    half = LANES // 2
    inv = 1.0 / (ROPE_BASE ** (np.arange(0, half, 2, dtype=np.float64) / half))
    pos = np.arange(n_lat)
    rows = (pos // GRID_W).astype(np.float64)[:, None] * inv[None, :]
    cols = (pos % GRID_W).astype(np.float64)[:, None] * inv[None, :]
    cos = np.concatenate([np.cos(rows)] * 2 + [np.cos(cols)] * 2, axis=1)
    sin = np.concatenate([-np.sin(rows), np.sin(rows), -np.sin(cols), np.sin(cols)], axis=1)
    cos = np.concatenate([np.ones((n_ctx, LANES)), cos], axis=0).astype(np.float32)
    sin = np.concatenate([np.zeros((n_ctx, LANES)), sin], axis=0).astype(np.float32)
    return jnp.asarray(cos), jnp.asarray(sin)


def kernel(x, c, ctx, c_ctx, ada_w, ada_b, norm1_g, norm2_g, ffn_w_in, ffn_conv_w, ffn_conv_b, ffn_w_out,
           gla_w_in, gla_a_w1, gla_a_w2, gla_a_b, gla_norm_g, gla_w_out, na_w_qkv, na_rpb, na_w_out, final_g):
    nb, n, d = x.shape
    n_ctx = ctx.shape[1]
    depth = ada_w.shape[0]
    dff = ffn_conv_b.shape[-1]
    assert n_ctx == TM and n % TM == 0 and depth == 2 and nb + 1 <= SUBLANES

    mod = _ada(jnp.concatenate([c, c_ctx[None, :]], axis=0), ada_w, ada_b)

    m0 = mod[0].reshape(SUBLANES, 1, 6 * d)
    dk = gla_a_w2.shape[-1]
    dv = gla_w_out.shape[1]
    assert 4 * 2 * GLA_LOW_RANK == LANES
    aw1 = jnp.concatenate([gla_a_w1[0, 0], gla_a_w1[0, 1]] * 4, axis=1).astype(BF16)
    zeros = jnp.zeros((GLA_LOW_RANK, dk), F32)
    w2 = jnp.concatenate([jnp.concatenate([gla_a_w2[0, 0], zeros], axis=1),
                          jnp.concatenate([zeros, gla_a_w2[0, 1]], axis=1)], axis=0)
    w2_hi = w2.astype(BF16)
    w2_lo = (w2 - w2_hi.astype(F32)).astype(BF16)
    aw2 = jnp.concatenate([w2_hi, w2_hi, w2_lo, w2_lo], axis=0)
    cos_t, sin_t = _rope_tables(n, n_ctx)
    q, k, v, r, gf, gb, ffn_wi0, ffn_wo0 = _gla_in(
        ctx, x, m0, norm1_g[0].reshape(1, d), gla_w_in[0].astype(BF16), aw1, aw2,
        gla_a_b[0].reshape(1, 2 * dk), cos_t, sin_t, dk, dv, cast=((ffn_w_in, 0), (ffn_w_out, 0)))
    o_f, o_b = _gla_core(q, k, v, gf, gb, GLA_HEADS, n_ctx)
    x_mid, h_mid = _gla_out(o_f, o_b, r, ctx, x, m0, gla_norm_g[0].reshape(1, dv // GLA_HEADS),
                            gla_w_out[0].astype(BF16), norm2_g[0].reshape(1, d), GLA_HEADS)
    ffn0 = functools.partial(_ffn, h_mid, x_mid, m0, ffn_wi0, ffn_conv_w[0], ffn_conv_b[0].reshape(1, dff),
                             ffn_wo0, final_g.reshape(1, d), final_norm=False)
    x_lat, ffn_wi1, ffn_wo1, w_qkv, w_na_out = ffn0(
        row0=0, rows=n, is_ctx=False, cast=((ffn_w_in, 1), (ffn_w_out, 1), (na_w_qkv, 0), (na_w_out, 0)))
    x_ctx = ffn0(row0=n, rows=n_ctx, is_ctx=True)

    m1 = mod[1].reshape(SUBLANES, 1, 6 * d)
    head_dim = d // NA_HEADS
    qn, kn, vn = _na_qkv(x_ctx, x_lat, m1, norm1_g[1].reshape(1, d), w_qkv, head_dim)
    bias = _na_bias(na_rpb[0])
    attn = _na_attn(qn, kn, vn, bias, n_ctx, head_dim)
    x_lat, h_lat = _na_out(attn, x_lat, m1, w_na_out, norm2_g[1].reshape(1, d))
    return _ffn(h_lat, x_lat, m1, ffn_wi1, ffn_conv_w[1], ffn_conv_b[1].reshape(1, dff), ffn_wo1,
                final_g.reshape(1, d), row0=0, rows=n, is_ctx=False, final_norm=True)
```

```python
import functools

import numpy as np
import jax
import jax.numpy as jnp
from jax import lax
from jax.experimental import pallas as pl
from jax.experimental.pallas import tpu as pltpu

GRID_W = 64
GLA_HEADS = 4
GLA_GATE_NORM = 16.0
GLA_LOW_RANK = 16
ROPE_BASE = 10000.0
NA_HEADS = 16
NA_KH = 8
NA_KW = 16
EPS = 1e-6
LOG2E = 1.4426950408889634

LANES = 128
SUBLANES = 8
VMEM_LIMIT = 56 * 1024 * 1024

TM = 256
GLA_C = 128
GLA_BLK = 256
GLA_DIAG = 16
FFN_TM = 512
FFN_CHUNK = 512
NA_RB = 32
NA_U = 16
NA_HG = 4
NEG = -1e30

F32 = jnp.float32
BF16 = jnp.bfloat16


def _cparams(sem):
    return pltpu.CompilerParams(dimension_semantics=sem, vmem_limit_bytes=VMEM_LIMIT)


def _resident(shape):
    nd = len(shape)
    return pl.BlockSpec(shape, lambda *_: (0,) * nd, pipeline_mode=pl.Buffered(1))


def _silu(x):
    return x * (1.0 / (1.0 + jnp.exp(-x)))


def _norm_mod(x, g, shift, scale):
    ms = jnp.mean(x * x, axis=-1, keepdims=True)
    return x * lax.rsqrt(ms + EPS) * (g * (1.0 + scale)) + shift


def _ada_kernel(c_ref, w_ref, b_ref, *rest, n_rows, n_cast):
    o_ref = rest[n_cast]
    _cast_blocks(rest[:n_cast], rest[n_cast + 1:])
    tn = w_ref.shape[2]
    rows = []
    for r in range(n_rows):
        sc = _silu(c_ref[r])
        parts = [jnp.sum(w_ref[0, :, j * LANES:(j + 1) * LANES] * sc, axis=0, keepdims=True)
                 for j in range(tn // LANES)]
        rows.append(jnp.concatenate(parts, axis=1) + b_ref[0])
    rows.append(jnp.zeros((SUBLANES - n_rows, tn), F32))
    o_ref[0] = jnp.concatenate(rows, axis=0)


def _ada(c_rows, ada_w, ada_b, cast):
    depth, d, d6 = ada_w.shape
    n_rows = c_rows.shape[0]
    tn = 1536
    c_b = jnp.broadcast_to(c_rows[:, :, None], (n_rows, d, LANES))
    cast_in, cast_out = _cast_specs(cast, depth, d6 // tn)
    return pl.pallas_call(
        functools.partial(_ada_kernel, n_rows=n_rows, n_cast=len(cast)),
        grid=(depth, d6 // tn),
        in_specs=[
            pl.BlockSpec((n_rows, d, LANES), lambda i, j: (0, 0, 0)),
            pl.BlockSpec((1, d, tn), lambda i, j: (i, 0, j)),
            pl.BlockSpec((1, 1, tn), lambda i, j: (i, 0, j)),
        ] + cast_in,
        out_specs=[pl.BlockSpec((1, SUBLANES, tn), lambda i, j: (i, 0, j))] + cast_out,
        out_shape=[jax.ShapeDtypeStruct((depth, SUBLANES, d6), F32)]
        + [jax.ShapeDtypeStruct(a.shape[1:], BF16) for a, _ in cast],
        compiler_params=_cparams(("arbitrary", "arbitrary")),
        name="ada_mod",
    )(c_b, ada_w, ada_b.reshape(depth, 1, d6), *[a for a, _ in cast])


def _mod_spec(d, seg, nb, tile_off):
    def imap(b, i):
        return (jnp.where(i + tile_off == 0, nb, b), 0, seg)
    return pl.BlockSpec((1, 1, d), imap)


def _stream_specs(d):
    return [pl.BlockSpec((1, TM, d), lambda b, i: (b, 0, 0)),
            pl.BlockSpec((1, TM, d), lambda b, i: (b, jnp.maximum(i - 1, 0), 0))]


def _stream_tile(ctx_ref, x_ref):
    return jnp.where(pl.program_id(1) == 0, ctx_ref[0], x_ref[0])


def _cast_specs(arrays, nb, steps):
    in_specs, out_specs = [], []
    for a, layer in arrays:
        _, rows, cols = a.shape
        tiles = rows // (2 * SUBLANES)
        nblk = max(k for k in range(1, nb * steps + 1) if tiles % k == 0)
        in_specs.append(pl.BlockSpec(
            (1, rows // nblk, cols),
            lambda b, i, nblk=nblk, layer=layer: (layer, jnp.minimum(b * steps + i, nblk - 1), 0)))
        out_specs.append(pl.BlockSpec(
            (rows // nblk, cols), lambda b, i, nblk=nblk: (jnp.minimum(b * steps + i, nblk - 1), 0)))
    return in_specs, out_specs


def _cast_blocks(src_refs, dst_refs):
    for src, dst in zip(src_refs, dst_refs):
        dst[...] = src[0].astype(BF16)


def _gla_in_kernel(c_ref, x_ref, sh_ref, sc_ref, g_ref, w_ref, aw1_ref, aw2_ref, ab_ref,
                   cos_ref, sin_ref, *rest, dk, dv, n_cast):
    q_ref, k_ref, v_ref, r_ref, gf_ref, gb_ref = rest[n_cast:n_cast + 6]
    _cast_blocks(rest[:n_cast], rest[n_cast + 6:])
    h = _norm_mod(_stream_tile(c_ref, x_ref), g_ref[...], sh_ref[0], sc_ref[0]).astype(BF16)

    z = jnp.dot(h, aw1_ref[...], preferred_element_type=F32)
    z_hi = z.astype(BF16)
    z_lo = (z - z_hi.astype(F32)).astype(BF16)
    grp = lax.broadcasted_iota(jnp.int32, z.shape, 1) // (2 * GLA_LOW_RANK)
    zc = jnp.where((grp & 1) == 0, z_hi, z_lo)
    pre = jnp.dot(zc, aw2_ref[...], preferred_element_type=F32) + ab_ref[...]

    qk = jnp.dot(h, w_ref[:, :2 * dk], preferred_element_type=F32)
    g = (jnp.minimum(pre, 0.0) - jnp.log(1.0 + jnp.exp(-jnp.abs(pre)))) * (LOG2E / GLA_GATE_NORM)
    g_hi = g.astype(BF16)
    g_lo = (g - g_hi.astype(F32)).astype(BF16)
    v_ref[0] = jnp.dot(h, w_ref[:, 2 * dk: 2 * dk + dv], preferred_element_type=F32).astype(BF16)

    cos = cos_ref[...]
    sin = sin_ref[...]
    lane = lax.broadcasted_iota(jnp.int32, cos.shape, 1)
    low = (lane & 32) == 0
    qscale = float(LANES) ** -0.5

    def rope(t):
        partner = jnp.where(low, pltpu.roll(t, LANES - 32, 1), pltpu.roll(t, 32, 1))
        return t * cos + partner * sin

    for hd in range(dk // LANES):
        sl = slice(hd * LANES, (hd + 1) * LANES)
        q_ref[0, :, sl] = rope(qk[:, sl]) * qscale
        k_ref[0, :, sl] = rope(qk[:, dk + hd * LANES: dk + (hd + 1) * LANES])

    ti = lax.broadcasted_iota(jnp.int32, (GLA_C, GLA_C), 0)
    tj = lax.broadcasted_iota(jnp.int32, (GLA_C, GLA_C), 1)
    for o_ref, tri, sl in ((gf_ref, tj <= ti, slice(0, dk)), (gb_ref, tj >= ti, slice(dk, 2 * dk))):
        tri = jnp.where(tri, 1.0, 0.0).astype(BF16)
        tri2 = jnp.concatenate([tri, tri], axis=1)
        for ch in range(TM // GLA_C):
            rows = slice(ch * GLA_C, (ch + 1) * GLA_C)
            o_ref[0, rows, :] = jnp.dot(tri2, jnp.concatenate([g_hi[rows, sl], g_lo[rows, sl]], axis=0),
                                        preferred_element_type=F32)
    r_ref[0] = jnp.dot(h, w_ref[:, 2 * dk + dv:], preferred_element_type=F32).astype(BF16)


def _gla_in(ctx, x, mod, norm_g, w_in, aw1, aw2, ab, cos_t, sin_t, dk, dv, cast):
    nb, n, d = x.shape
    t = ctx.shape[1] + n
    nt = t // TM
    kern = functools.partial(_gla_in_kernel, dk=dk, dv=dv, n_cast=len(cast))
    row = lambda w: pl.BlockSpec((1, TM, w), lambda b, i: (b, i, 0))
    cast_in, cast_out = _cast_specs(cast, nb, nt)
    return pl.pallas_call(
        kern,
        grid=(nb, nt),
        in_specs=_stream_specs(d) + [
            _mod_spec(d, 0, nb, 0),
            _mod_spec(d, 1, nb, 0),
            _resident((1, d)),
            _resident(w_in.shape),
            _resident(aw1.shape),
            _resident(aw2.shape),
            _resident(ab.shape),
            pl.BlockSpec((TM, LANES), lambda b, i: (i, 0)),
            pl.BlockSpec((TM, LANES), lambda b, i: (i, 0)),
        ] + cast_in,
        out_specs=[row(dk), row(dk), row(dv), row(dv), row(dk), row(dk)] + cast_out,
        out_shape=[
            jax.ShapeDtypeStruct((nb, t, dk), F32),
            jax.ShapeDtypeStruct((nb, t, dk), F32),
            jax.ShapeDtypeStruct((nb, t, dv), BF16),
            jax.ShapeDtypeStruct((nb, t, dv), BF16),
            jax.ShapeDtypeStruct((nb, t, dk), F32),
            jax.ShapeDtypeStruct((nb, t, dk), F32),
        ] + [jax.ShapeDtypeStruct(a.shape[1:], BF16) for a, _ in cast],
        compiler_params=_cparams(("arbitrary", "arbitrary")),
        name="gla_in",
    )(ctx, x, mod, mod, norm_g, w_in, aw1, aw2, ab, cos_t, sin_t, *[a for a, _ in cast])


def _row_bcast(b, period, offset):
    c, w = b.shape
    if period == c:
        return jnp.broadcast_to(b[offset:offset + 1, :], (c, w))
    b3 = b.reshape(c // period, period, w)
    return jnp.broadcast_to(b3[:, offset:offset + 1, :], b3.shape).reshape(c, w)


def _pair_scores(lhs, rhs, first):
    zero = jnp.zeros_like(rhs)
    blockdiag = jnp.concatenate([jnp.where(first, rhs, zero), jnp.where(first, zero, rhs)], axis=0)
    return lax.dot_general(lhs, blockdiag, (((1,), (1,)), ((), ())), preferred_element_type=F32)


def _gla_pair(q, k, b, rev, kside, same, causal, first):
    c = q.shape[0]
    b_last = b[0:1, :] if rev else b[c - 1:c, :]
    q_in = (q * jnp.exp2(b)).astype(BF16)
    k_out = (k * jnp.exp2(b_last - b)).astype(BF16)
    b_ref = _row_bcast(b, GLA_DIAG, GLA_DIAG - 1 if rev else 0)
    qd = (q * jnp.exp2(b - b_ref)).astype(BF16)
    kd = (k * jnp.exp2(b_ref - b)).astype(BF16)
    att = _pair_scores(qd, kd, first)
    period = 2 * GLA_DIAG
    while period <= c:
        half = period // 2
        b_ref = _row_bcast(b, period, half if rev else half - 1)
        u = (jnp.where(kside[period], k, q) * jnp.exp2(-jnp.abs(b - b_ref))).astype(BF16)
        att = jnp.where(same[half], att, _pair_scores(u, u, first))
        period *= 2
    att = jnp.where(causal, att, 0.0).astype(BF16)
    return q_in, k_out, att, jnp.exp2(b_last)


def _gla_state_step(q_in, att, k_out, v, decay_row, s):
    lhs = jnp.concatenate([q_in, att], axis=1)
    rhs = jnp.concatenate([s.astype(BF16), v], axis=0)
    o = jnp.dot(lhs, rhs, preferred_element_type=F32)
    decay = jnp.transpose(jnp.broadcast_to(decay_row, (LANES, LANES)))
    decay = jnp.concatenate([decay] * (s.shape[1] // LANES), axis=1)
    kv = lax.dot_general(k_out, v, (((0,), (0,)), ((), ())), preferred_element_type=F32)
    return o, s * decay + kv


def _gla_core_kernel(qf_ref, kf_ref, vf_ref, gf_ref, qb_ref, kb_ref, vb_ref, gb_ref,
                     of_ref, ob_ref, s_ref, *, heads, dvh):
    @pl.when(pl.program_id(1) == 0)
    def _():
        s_ref[...] = jnp.zeros_like(s_ref)

    c = GLA_C
    assert c == LANES and heads % 2 == 0
    row = lax.broadcasted_iota(jnp.int32, (c, 2 * LANES), 0)
    first = lax.broadcasted_iota(jnp.int32, (c, 2 * LANES), 1) < LANES
    ti = lax.broadcasted_iota(jnp.int32, (c, 2 * c), 0)
    tj = lax.broadcasted_iota(jnp.int32, (c, 2 * c), 1) & (c - 1)
    x = ti ^ tj
    same = {}
    half = GLA_DIAG
    while half < c:
        same[half] = x < half
        half *= 2

    n_chunks = qf_ref.shape[1] // c
    dirs = ((qf_ref, kf_ref, vf_ref, gf_ref, of_ref), (qb_ref, kb_ref, vb_ref, gb_ref, ob_ref))
    masks = []
    for d in range(2):
        rev = d == 1
        kside = {}
        period = 2 * GLA_DIAG
        while period <= c:
            off = row & (period - 1)
            kside[period] = (off >= period // 2) if rev else (off < period // 2)
            period *= 2
        masks.append((kside, (tj >= ti) if rev else (tj <= ti)))

    for step in range(n_chunks):
        pre = {}
        for d, (q_ref, k_ref, v_ref, b_ref, o_ref) in enumerate(dirs):
            rev = d == 1
            ch = n_chunks - 1 - step if rev else step
            rows = slice(ch * c, (ch + 1) * c)
            for pr in range(heads // 2):
                psl = slice(2 * pr * LANES, (2 * pr + 2) * LANES)
                pre[d, pr] = _gla_pair(q_ref[0, rows, psl], k_ref[0, rows, psl], b_ref[0, rows, psl],
                                       rev, masks[d][0], same, masks[d][1], first)
        for d, (q_ref, k_ref, v_ref, b_ref, o_ref) in enumerate(dirs):
            ch = n_chunks - 1 - step if d == 1 else step
            rows = slice(ch * c, (ch + 1) * c)
            for pr in range(heads // 2):
                q_in, k_out, att, decay = pre[d, pr]
                for j in range(2):
                    hd = 2 * pr + j
                    hsl = slice(j * LANES, (j + 1) * LANES)
                    vsl = slice(hd * dvh, (hd + 1) * dvh)
                    o, s_new = _gla_state_step(q_in[:, hsl], att[:, j * c:(j + 1) * c], k_out[:, hsl],
                                               v_ref[0, rows, vsl], decay[:, hsl], s_ref[d, hd])
                    o_ref[0, rows, vsl] = o.astype(o_ref.dtype)
                    s_ref[d, hd] = s_new


def _gla_core(q, k, v, gf, gb, heads, n_ctx):
    nb, t, dk = q.shape
    dv = v.shape[-1]
    assert n_ctx % GLA_BLK == 0 and t % GLA_BLK == 0
    nc = t // GLA_BLK
    ncx = n_ctx // GLA_BLK
    fwd = lambda b, s: (b, s, 0)
    bwd = lambda b, s: (b, jnp.where(s < ncx, ncx - 1 - s, nc - 1 + ncx - s), 0)
    kern = functools.partial(_gla_core_kernel, heads=heads, dvh=dv // heads)
    blk = lambda w, m: pl.BlockSpec((1, GLA_BLK, w), m)
    return pl.pallas_call(
        kern,
        grid=(nb, nc),
        in_specs=[blk(dk, fwd), blk(dk, fwd), blk(dv, fwd), blk(dk, fwd),
                  blk(dk, bwd), blk(dk, bwd), blk(dv, bwd), blk(dk, bwd)],
        out_specs=[blk(dv, fwd), blk(dv, bwd)],
        out_shape=[jax.ShapeDtypeStruct((nb, t, dv), BF16)] * 2,
        scratch_shapes=[pltpu.VMEM((2, heads, dk // heads, dv // heads), F32)],
        compiler_params=_cparams(("parallel", "arbitrary")),
        name="gla_core",
    )(q, k, v, gf, q, k, v, gb)


def _gla_out_kernel(of_ref, ob_ref, r_ref, c_ref, x_ref, gate_ref, sh2_ref, sc2_ref, ng_ref, w_ref, g2_ref,
                    xo_ref, ho_ref, *, heads):
    o = of_ref[0].astype(F32) + ob_ref[0].astype(F32)
    r = r_ref[0].astype(F32)
    dvh = o.shape[1] // heads
    parts = []
    for hd in range(heads):
        oh = o[:, hd * dvh:(hd + 1) * dvh]
        ms = jnp.mean(oh * oh, axis=-1, keepdims=True)
        parts.append(oh * lax.rsqrt(ms + EPS) * ng_ref[...])
    y = jnp.concatenate(parts, axis=1) * _silu(r)
    out = jnp.dot(y.astype(BF16), w_ref[...], preferred_element_type=F32)
    x_mid = _stream_tile(c_ref, x_ref) + gate_ref[0] * out
    xo_ref[0] = x_mid
    ho_ref[0] = _norm_mod(x_mid, g2_ref[...], sh2_ref[0], sc2_ref[0]).astype(BF16)


def _gla_out(o_f, o_b, r, ctx, x, mod, norm_g, w_out, norm2_g, heads):
    nb, t, dv = o_f.shape
    d = x.shape[-1]
    nt = t // TM
    row = lambda w: pl.BlockSpec((1, TM, w), lambda b, i: (b, i, 0))
    out = pl.BlockSpec((1, TM, d), lambda b, i: (b, jnp.where(i == 0, nt - 1, i - 1), 0))
    return pl.pallas_call(
        functools.partial(_gla_out_kernel, heads=heads),
        grid=(nb, nt),
        in_specs=[row(dv), row(dv), row(dv)] + _stream_specs(d) + [
            _mod_spec(d, 2, nb, 0), _mod_spec(d, 3, nb, 0), _mod_spec(d, 4, nb, 0),
            _resident(norm_g.shape), _resident(w_out.shape), _resident((1, d))],
        out_specs=[out, out],
        out_shape=[jax.ShapeDtypeStruct((nb, t, d), F32), jax.ShapeDtypeStruct((nb, t, d), BF16)],
        compiler_params=_cparams(("parallel", "parallel")),
        name="gla_out",
    )(o_f, o_b, r, ctx, x, mod, mod, mod, norm_g, w_out, norm2_g)


def _ffn_kernel(h_ref, hp_ref, hn_ref, x_ref, gate_ref, wi_ref, cw_ref, cb_ref, wo_ref, fg_ref,
                *rest, chunks, final_norm, n_cast):
    o_ref = rest[n_cast]
    hbuf_ref, act_ref = rest[-2:]
    _cast_blocks(rest[:n_cast], rest[n_cast + 1:-2])
    i = pl.program_id(1)
    tm = h_ref.shape[1]
    dff = act_ref.shape[1]
    hs = SUBLANES
    has_prev = i > 0
    has_next = i < pl.num_programs(1) - 1
    hbuf_ref[0:tm, :] = h_ref[0]
    nxt = jnp.where(has_next, hn_ref[0, 0:hs, :].astype(F32), 0.0)
    prv = jnp.where(has_prev, hp_ref[0, hs:2 * hs, :].astype(F32), 0.0)
    hbuf_ref[tm:, :] = jnp.concatenate([nxt, prv], axis=0).astype(BF16)
    n = tm + 2 * hs

    for (c0, cw) in chunks:
        a = jnp.dot(hbuf_ref[...], wi_ref[:, c0:c0 + cw], preferred_element_type=F32)
        val = jnp.dot(hbuf_ref[0:tm, :], wi_ref[:, dff + c0:dff + c0 + cw], preferred_element_type=F32)
        w3 = cw_ref[:, c0:c0 + cw]
        conv = (pltpu.roll(a, 1, 0)[0:tm] * w3[0:1] + a[0:tm] * w3[1:2]
                + pltpu.roll(a, n - 1, 0)[0:tm] * w3[2:3] + cb_ref[:, c0:c0 + cw])
        act_ref[:, c0:c0 + cw] = (_silu(conv) * val).astype(BF16)
    y = x_ref[0] + gate_ref[0] * jnp.dot(act_ref[...], wo_ref[...], preferred_element_type=F32)
    if final_norm:
        ms = jnp.mean(y * y, axis=-1, keepdims=True)
        y = y * lax.rsqrt(ms + EPS) * fg_ref[...]
    o_ref[0] = y


def _ffn(h, x, mod, w_in, conv_w, conv_b, w_out, final_g, *, row0, rows, is_ctx, final_norm, cast=()):
    nb, _, d = x.shape
    t = rows
    dff = w_out.shape[0]
    tm = min(FFN_TM, t)
    hb = 2 * SUBLANES
    bpt = tm // hb
    assert row0 % tm == 0 and rows % tm == 0
    t0 = row0 // tm
    b0 = row0 // hb
    nblk = t // hb
    mod_row = (lambda b: nb) if is_ctx else (lambda b: b)
    chunks = []
    c0 = 0
    while c0 < dff:
        cw = min(FFN_CHUNK, dff - c0)
        chunks.append((c0, cw))
        c0 += cw
    kern = functools.partial(_ffn_kernel, chunks=tuple(chunks), final_norm=final_norm, n_cast=len(cast))
    row = pl.BlockSpec((1, tm, d), lambda b, i: (b, t0 + i, 0))
    cast_in, cast_out = _cast_specs(cast, nb, t // tm)
    out = pl.pallas_call(
        kern,
        grid=(nb, t // tm),
        in_specs=[
            row,
            pl.BlockSpec((1, hb, d), lambda b, i: (b, b0 + jnp.maximum(i * bpt - 1, 0), 0)),
            pl.BlockSpec((1, hb, d), lambda b, i: (b, b0 + jnp.minimum((i + 1) * bpt, nblk - 1), 0)),
            row,
            pl.BlockSpec((1, 1, d), lambda b, i: (mod_row(b), 0, 5)),
            _resident(w_in.shape),
            _resident(conv_w.shape),
            _resident(conv_b.shape),
            _resident(w_out.shape),
            _resident((1, d)),
        ] + cast_in,
        out_specs=[pl.BlockSpec((1, tm, d), lambda b, i: (b, i, 0))] + cast_out,
        out_shape=[jax.ShapeDtypeStruct((nb, t, d), F32)]
        + [jax.ShapeDtypeStruct(a.shape[1:], BF16) for a, _ in cast],
        scratch_shapes=[pltpu.VMEM((tm + 2 * SUBLANES, d), BF16), pltpu.VMEM((tm, dff), BF16)],
        compiler_params=_cparams(("arbitrary", "arbitrary") if cast else ("parallel", "parallel")),
        name=("ffn_final" if final_norm else "ffn") + ("_ctx" if is_ctx else ""),
    )(h, h, h, x, mod, w_in, conv_w, conv_b, w_out, final_g, *[a for a, _ in cast])
    return out if cast else out[0]


def _na_qkv_kernel(c_ref, x_ref, sh_ref, sc_ref, g_ref, w_ref, q_ref, k_ref, v_ref, *, d, qscale):
    h = _norm_mod(_stream_tile(c_ref, x_ref), g_ref[...], sh_ref[0], sc_ref[0]).astype(BF16)
    proj = jnp.dot(h, w_ref[...], preferred_element_type=F32)
    q_ref[0] = (proj[:, :d] * qscale).astype(BF16)
    k_ref[0] = proj[:, d:2 * d].astype(BF16)
    v_ref[0] = proj[:, 2 * d:].astype(BF16)


def _na_qkv(ctx, x, mod, norm_g, w_qkv, head_dim):
    nb, n, d = x.shape
    t = ctx.shape[1] + n
    row = lambda w: pl.BlockSpec((1, TM, w), lambda b, i: (b, i, 0))
    return pl.pallas_call(
        functools.partial(_na_qkv_kernel, d=d, qscale=LOG2E * float(head_dim) ** -0.5),
        grid=(nb, t // TM),
        in_specs=_stream_specs(d) + [_mod_spec(d, 0, nb, 0), _mod_spec(d, 1, nb, 0),
                                     _resident((1, d)), _resident(w_qkv.shape)],
        out_specs=[row(d)] * 3,
        out_shape=[jax.ShapeDtypeStruct((nb, t, d), BF16)] * 3,
        compiler_params=_cparams(("parallel", "parallel")),
        name="na_qkv",
    )(ctx, x, mod, mod, norm_g, w_qkv)


def _na_attn_kernel(q_ref, k_ref, v_ref, bias_ref, o_ref, *, n_ctx, rows_n, head_dim):
    rb = pl.program_id(2)
    hw = q_ref.shape[-1]
    nwin = NA_KH * GRID_W
    nt = (((1,), (1,)), ((), ()))
    lane = lax.broadcasted_iota(jnp.int32, (GRID_W, hw), 1)
    hmask = [(lane // head_dim) == hd for hd in range(NA_HG)]

    def window(i):
        r = rb * NA_RB + i
        rs = jnp.clip(r - NA_KH // 2, 0, rows_n - NA_KH)
        lo = rs - r + (NA_KH - 1)
        return r, lo, pl.multiple_of(n_ctx + rs * GRID_W, GRID_W)

    def scores(i):
        r, lo, start = window(i)
        q = q_ref[0, pl.ds(pl.multiple_of(n_ctx + r * GRID_W, GRID_W), GRID_W), :]
        qs = jnp.concatenate([jnp.where(hmask[hd], q, jnp.zeros_like(q)) for hd in range(NA_HG)], axis=0)
        bias = jnp.concatenate(
            [jnp.concatenate([bias_ref[0, hd, lo + 2 * m] for m in range(NA_KH // 2)], axis=1)
             for hd in range(NA_HG)], axis=0)
        s_nb = lax.dot_general(qs, k_ref[0, pl.ds(start, nwin), :], nt, preferred_element_type=F32) + bias
        s_cx = lax.dot_general(qs, k_ref[0, 0:n_ctx, :], nt, preferred_element_type=F32)
        s = jnp.concatenate([s_nb, s_cx], axis=1)
        return s, jnp.max(s, axis=-1, keepdims=True)

    def finish(i, s, m):
        r, lo, start = window(i)
        p = jnp.exp2(s - m)
        l = jnp.sum(p, axis=-1, keepdims=True)
        p = p.astype(BF16)
        o = (jnp.dot(p[:, :nwin], v_ref[0, pl.ds(start, nwin), :], preferred_element_type=F32)
             + jnp.dot(p[:, nwin:], v_ref[0, 0:n_ctx, :], preferred_element_type=F32))
        o = o * (1.0 / l)
        out = o[0:GRID_W]
        for hd in range(1, NA_HG):
            out = jnp.where(hmask[hd], o[hd * GRID_W:(hd + 1) * GRID_W], out)
        o_ref[0, pl.ds(pl.multiple_of(i * GRID_W, GRID_W), GRID_W), :] = out.astype(BF16)

    def body(it, carry):
        nxt = scores(it * NA_U)
        for u in range(NA_U):
            cur = nxt
            if u + 1 < NA_U:
                nxt = scores(it * NA_U + u + 1)
            finish(it * NA_U + u, *cur)
        return carry

    lax.fori_loop(0, NA_RB // NA_U, body, 0)


def _na_bias(rpb):
    heads, ndr, ndc = rpb.shape
    c = np.arange(GRID_W)
    cs = np.clip(c - NA_KW // 2, 0, GRID_W - NA_KW)
    onehot = np.zeros((2 * ndc, GRID_W, 2 * GRID_W), np.float32)
    mask = np.full((GRID_W, 2 * GRID_W), NEG, np.float32)
    for half in range(2):
        for cq in range(GRID_W):
            for kc in range(cs[cq], cs[cq] + NA_KW):
                onehot[half * ndc + kc - cq + NA_KW - 1, cq, half * GRID_W + kc] = 1.0
                mask[cq, half * GRID_W + kc] = 0.0
    pair = jnp.concatenate([rpb[:, :ndr - 1, :], rpb[:, 1:, :]], axis=-1)
    tiles = jnp.einsum('hdk,kcl->hdcl', pair * LOG2E, onehot, precision=lax.Precision.HIGHEST) + mask
    return tiles.reshape(heads // NA_HG, NA_HG, ndr - 1, GRID_W, 2 * GRID_W)


def _na_attn(q, k, v, bias, n_ctx, head_dim):
    nb, t, d = q.shape
    rows_n = (t - n_ctx) // GRID_W
    nrb = rows_n // NA_RB
    nq = NA_RB * GRID_W
    hw = NA_HG * head_dim
    ngrp = d // hw
    assert n_ctx % nq == 0 or nq % n_ctx == 0
    kern = functools.partial(_na_attn_kernel, n_ctx=n_ctx, rows_n=rows_n, head_dim=head_dim)
    return pl.pallas_call(
        kern,
        grid=(nb, ngrp, nrb),
        in_specs=[
            pl.BlockSpec((1, t, hw), lambda b, g, r: (b, 0, g)),
            pl.BlockSpec((1, t, hw), lambda b, g, r: (b, 0, g)),
            pl.BlockSpec((1, t, hw), lambda b, g, r: (b, 0, g)),
            pl.BlockSpec((1,) + bias.shape[1:], lambda b, g, r: (g, 0, 0, 0, 0)),
        ],
        out_specs=pl.BlockSpec((1, nq, hw), lambda b, g, r: (b, r, g)),
        out_shape=jax.ShapeDtypeStruct((nb, t - n_ctx, d), BF16),
        compiler_params=_cparams(("parallel", "parallel", "arbitrary")),
        name="na_attn",
    )(q, k, v, bias)


def _na_out_kernel(a_ref, x_ref, gate_ref, sh2_ref, sc2_ref, w_ref, g2_ref, o_ref, h_ref):
    out = jnp.dot(a_ref[0], w_ref[...], preferred_element_type=F32)
    x_mid = x_ref[0] + gate_ref[0] * out
    o_ref[0] = x_mid
    h_ref[0] = _norm_mod(x_mid, g2_ref[...], sh2_ref[0], sc2_ref[0]).astype(BF16)


def _na_out(attn, x, mod, w_out, norm2_g):
    nb, n, d = x.shape
    tm = FFN_TM
    row = pl.BlockSpec((1, tm, d), lambda b, i: (b, i, 0))
    seg = lambda s: pl.BlockSpec((1, 1, d), lambda b, i: (b, 0, s))
    return pl.pallas_call(
        _na_out_kernel,
        grid=(nb, n // tm),
        in_specs=[row, row, seg(2), seg(3), seg(4), _resident(w_out.shape), _resident((1, d))],
        out_specs=[row, row],
        out_shape=[jax.ShapeDtypeStruct((nb, n, d), F32), jax.ShapeDtypeStruct((nb, n, d), BF16)],
        compiler_params=_cparams(("parallel", "parallel")),
        name="na_out",
    )(attn, x, mod, mod, mod, w_out, norm2_g)


def _rope_tables(n_lat, n_ctx):
    half = LANES // 2
    inv = 1.0 / (ROPE_BASE ** (np.arange(0, half, 2, dtype=np.float64) / half))
    pos = np.arange(n_lat)
    rows = (pos // GRID_W).astype(np.float64)[:, None] * inv[None, :]
    cols = (pos % GRID_W).astype(np.float64)[:, None] * inv[None, :]
    cos = np.concatenate([np.cos(rows)] * 2 + [np.cos(cols)] * 2, axis=1)
    sin = np.concatenate([-np.sin(rows), np.sin(rows), -np.sin(cols), np.sin(cols)], axis=1)
    cos = np.concatenate([np.ones((n_ctx, LANES)), cos], axis=0).astype(np.float32)
    sin = np.concatenate([np.zeros((n_ctx, LANES)), sin], axis=0).astype(np.float32)
    return jnp.asarray(cos), jnp.asarray(sin)


def kernel(x, c, ctx, c_ctx, ada_w, ada_b, norm1_g, norm2_g, ffn_w_in, ffn_conv_w, ffn_conv_b, ffn_w_out,
           gla_w_in, gla_a_w1, gla_a_w2, gla_a_b, gla_norm_g, gla_w_out, na_w_qkv, na_rpb, na_w_out, final_g):
    nb, n, d = x.shape
    n_ctx = ctx.shape[1]
    depth = ada_w.shape[0]
    dff = ffn_conv_b.shape[-1]
    assert n_ctx == TM and n % TM == 0 and depth == 2 and nb + 1 <= SUBLANES

    mod, w_gla_in, w_gla_out = _ada(jnp.concatenate([c, c_ctx[None, :]], axis=0), ada_w, ada_b,
                                    cast=((gla_w_in, 0), (gla_w_out, 0)))

    m0 = mod[0].reshape(SUBLANES, 1, 6 * d)
    dk = gla_a_w2.shape[-1]
    dv = gla_w_out.shape[1]
    assert 4 * 2 * GLA_LOW_RANK == LANES
    aw1 = jnp.concatenate([gla_a_w1[0, 0], gla_a_w1[0, 1]] * 4, axis=1).astype(BF16)
    zeros = jnp.zeros((GLA_LOW_RANK, dk), F32)
    w2 = jnp.concatenate([jnp.concatenate([gla_a_w2[0, 0], zeros], axis=1),
                          jnp.concatenate([zeros, gla_a_w2[0, 1]], axis=1)], axis=0)
    w2_hi = w2.astype(BF16)
    w2_lo = (w2 - w2_hi.astype(F32)).astype(BF16)
    aw2 = jnp.concatenate([w2_hi, w2_hi, w2_lo, w2_lo], axis=0)
    cos_t, sin_t = _rope_tables(n, n_ctx)
    q, k, v, r, gf, gb, ffn_wi0, ffn_wo0 = _gla_in(
        ctx, x, m0, norm1_g[0].reshape(1, d), w_gla_in, aw1, aw2,
        gla_a_b[0].reshape(1, 2 * dk), cos_t, sin_t, dk, dv, cast=((ffn_w_in, 0), (ffn_w_out, 0)))
    o_f, o_b = _gla_core(q, k, v, gf, gb, GLA_HEADS, n_ctx)
    x_mid, h_mid = _gla_out(o_f, o_b, r, ctx, x, m0, gla_norm_g[0].reshape(1, dv // GLA_HEADS),
                            w_gla_out, norm2_g[0].reshape(1, d), GLA_HEADS)
    ffn0 = functools.partial(_ffn, h_mid, x_mid, m0, ffn_wi0, ffn_conv_w[0], ffn_conv_b[0].reshape(1, dff),
                             ffn_wo0, final_g.reshape(1, d), final_norm=False)
    x_lat, ffn_wi1, ffn_wo1, w_qkv, w_na_out = ffn0(
        row0=0, rows=n, is_ctx=False, cast=((ffn_w_in, 1), (ffn_w_out, 1), (na_w_qkv, 0), (na_w_out, 0)))
    x_ctx = ffn0(row0=n, rows=n_ctx, is_ctx=True)

    m1 = mod[1].reshape(SUBLANES, 1, 6 * d)
    head_dim = d // NA_HEADS
    qn, kn, vn = _na_qkv(x_ctx, x_lat, m1, norm1_g[1].reshape(1, d), w_qkv, head_dim)
    bias = _na_bias(na_rpb[0])
    attn = _na_attn(qn, kn, vn, bias, n_ctx, head_dim)
    x_lat, h_lat = _na_out(attn, x_lat, m1, w_na_out, norm2_g[1].reshape(1, d))
    return _ffn(h_lat, x_lat, m1, ffn_wi1, ffn_conv_w[1], ffn_conv_b[1].reshape(1, dff), ffn_wo1,
                final_g.reshape(1, d), row0=0, rows=n, is_ctx=False, final_norm=True)
```

```python
import functools

import numpy as np
import jax
import jax.numpy as jnp
from jax import lax
from jax.experimental import pallas as pl
from jax.experimental.pallas import tpu as pltpu

GRID_W = 64
GLA_HEADS = 4
GLA_GATE_NORM = 16.0
GLA_LOW_RANK = 16
ROPE_BASE = 10000.0
NA_HEADS = 16
NA_KH = 8
NA_KW = 16
EPS = 1e-6
LOG2E = 1.4426950408889634

LANES = 128
SUBLANES = 8
VMEM_LIMIT = 56 * 1024 * 1024

TM = 256
ADA_TK = 256
GLA_C = 128
GLA_BLK = 256
GLA_DIAG = 16
FFN_TM = 512
FFN_CHUNK = 512
NA_RB = 32
NA_U = 32
NA_HG = 4
NEG = -1e30

F32 = jnp.float32
BF16 = jnp.bfloat16


def _cparams(sem):
    return pltpu.CompilerParams(dimension_semantics=sem, vmem_limit_bytes=VMEM_LIMIT)


def _resident(shape):
    nd = len(shape)
    return pl.BlockSpec(shape, lambda *_: (0,) * nd, pipeline_mode=pl.Buffered(1))


def _silu(x):
    return x * (1.0 / (1.0 + jnp.exp(-x)))


def _norm_mod(x, g, shift, scale):
    ms = jnp.mean(x * x, axis=-1, keepdims=True)
    return x * lax.rsqrt(ms + EPS) * (g * (1.0 + scale)) + shift


def _ada_kernel(c_ref, w_ref, b_ref, *rest, n_rows, n_cast):
    o_ref = rest[n_cast]
    _cast_blocks(rest[:n_cast], rest[n_cast + 1:])
    j = pl.program_id(1)
    tk, d6 = w_ref.shape[1:]
    k0 = pl.multiple_of(j * tk, tk)
    rows = []
    for r in range(n_rows):
        sc = _silu(c_ref[r, pl.ds(k0, tk), :])
        parts = [jnp.sum(w_ref[0, :, n * LANES:(n + 1) * LANES] * sc, axis=0, keepdims=True)
                 for n in range(d6 // LANES)]
        rows.append(jnp.concatenate(parts, axis=1))
    rows.append(jnp.zeros((SUBLANES - n_rows, d6), F32))
    part = jnp.concatenate(rows, axis=0)

    @pl.when(j == 0)
    def _():
        o_ref[0] = part + b_ref[0]

    @pl.when(j > 0)
    def _():
        o_ref[0] += part


def _ada(c_rows, ada_w, ada_b, cast):
    depth, d, d6 = ada_w.shape
    n_rows = c_rows.shape[0]
    tk = ADA_TK
    c_b = jnp.broadcast_to(c_rows[:, :, None], (n_rows, d, LANES))
    cast_in, cast_out = _cast_specs(cast, depth, d // tk)
    return pl.pallas_call(
        functools.partial(_ada_kernel, n_rows=n_rows, n_cast=len(cast)),
        grid=(depth, d // tk),
        in_specs=[
            pl.BlockSpec((n_rows, d, LANES), lambda i, j: (0, 0, 0)),
            pl.BlockSpec((1, tk, d6), lambda i, j: (i, j, 0)),
            pl.BlockSpec((1, 1, d6), lambda i, j: (i, 0, 0)),
        ] + cast_in,
        out_specs=[pl.BlockSpec((1, SUBLANES, d6), lambda i, j: (i, 0, 0))] + cast_out,
        out_shape=[jax.ShapeDtypeStruct((depth, SUBLANES, d6), F32)]
        + [jax.ShapeDtypeStruct(a.shape[1:], BF16) for a, _ in cast],
        compiler_params=_cparams(("arbitrary", "arbitrary")),
        name="ada_mod",
    )(c_b, ada_w, ada_b.reshape(depth, 1, d6), *[a for a, _ in cast])


def _mod_spec(d, seg, nb):
    return pl.BlockSpec((1, 1, d), lambda b, i: (jnp.where(i == 0, nb, b), 0, seg))


def _stream_specs(d):
    return [pl.BlockSpec((1, TM, d), lambda b, i: (b, 0, 0)),
            pl.BlockSpec((1, TM, d), lambda b, i: (b, jnp.maximum(i - 1, 0), 0))]


def _stream_tile(ctx_ref, x_ref):
    return jnp.where(pl.program_id(1) == 0, ctx_ref[0], x_ref[0])


def _cast_specs(arrays, nb, steps):
    in_specs, out_specs = [], []
    for a, layer in arrays:
        _, rows, cols = a.shape
        tiles = rows // (2 * SUBLANES)
        nblk = max(k for k in range(1, nb * steps + 1) if tiles % k == 0)
        in_specs.append(pl.BlockSpec(
            (1, rows // nblk, cols),
            lambda b, i, nblk=nblk, layer=layer: (layer, jnp.minimum(b * steps + i, nblk - 1), 0)))
        out_specs.append(pl.BlockSpec(
            (rows // nblk, cols), lambda b, i, nblk=nblk: (jnp.minimum(b * steps + i, nblk - 1), 0)))
    return in_specs, out_specs


def _cast_blocks(src_refs, dst_refs):
    for src, dst in zip(src_refs, dst_refs):
        dst[...] = src[0].astype(BF16)


def _gla_in_kernel(c_ref, x_ref, sh_ref, sc_ref, g_ref, w_ref, aw1_ref, aw2_ref, ab_ref,
                   cos_ref, sin_ref, *rest, dk, dv, n_cast):
    q_ref, k_ref, v_ref, r_ref, gf_ref, gb_ref = rest[n_cast:n_cast + 6]
    _cast_blocks(rest[:n_cast], rest[n_cast + 6:])
    h = _norm_mod(_stream_tile(c_ref, x_ref), g_ref[...], sh_ref[0], sc_ref[0]).astype(BF16)

    z = jnp.dot(h, aw1_ref[...], preferred_element_type=F32)
    z_hi = z.astype(BF16)
    z_lo = (z - z_hi.astype(F32)).astype(BF16)
    grp = lax.broadcasted_iota(jnp.int32, z.shape, 1) // (2 * GLA_LOW_RANK)
    zc = jnp.where((grp & 1) == 0, z_hi, z_lo)
    pre = jnp.dot(zc, aw2_ref[...], preferred_element_type=F32) + ab_ref[...]

    qk = jnp.dot(h, w_ref[:, :2 * dk], preferred_element_type=F32)
    g = (jnp.minimum(pre, 0.0) - jnp.log(1.0 + jnp.exp(-jnp.abs(pre)))) * (LOG2E / GLA_GATE_NORM)
    g_hi = g.astype(BF16)
    g_lo = (g - g_hi.astype(F32)).astype(BF16)
    v_ref[0] = jnp.dot(h, w_ref[:, 2 * dk: 2 * dk + dv], preferred_element_type=F32).astype(BF16)

    cos = cos_ref[...]
    sin = sin_ref[...]
    lane = lax.broadcasted_iota(jnp.int32, cos.shape, 1)
    low = (lane & 32) == 0
    qscale = float(LANES) ** -0.5

    def rope(t):
        partner = jnp.where(low, pltpu.roll(t, LANES - 32, 1), pltpu.roll(t, 32, 1))
        return t * cos + partner * sin

    for hd in range(dk // LANES):
        sl = slice(hd * LANES, (hd + 1) * LANES)
        q_ref[0, :, sl] = rope(qk[:, sl]) * qscale
        k_ref[0, :, sl] = rope(qk[:, dk + hd * LANES: dk + (hd + 1) * LANES])

    ti = lax.broadcasted_iota(jnp.int32, (GLA_C, GLA_C), 0)
    tj = lax.broadcasted_iota(jnp.int32, (GLA_C, GLA_C), 1)
    for o_ref, tri, sl in ((gf_ref, tj <= ti, slice(0, dk)), (gb_ref, tj >= ti, slice(dk, 2 * dk))):
        tri = jnp.where(tri, 1.0, 0.0).astype(BF16)
        tri2 = jnp.concatenate([tri, tri], axis=1)
        for ch in range(TM // GLA_C):
            rows = slice(ch * GLA_C, (ch + 1) * GLA_C)
            o_ref[0, rows, :] = jnp.dot(tri2, jnp.concatenate([g_hi[rows, sl], g_lo[rows, sl]], axis=0),
                                        preferred_element_type=F32)
    r_ref[0] = jnp.dot(h, w_ref[:, 2 * dk + dv:], preferred_element_type=F32).astype(BF16)


def _gla_in(ctx, x, mod, norm_g, w_in, aw1, aw2, ab, cos_t, sin_t, dk, dv, cast):
    nb, n, d = x.shape
    t = ctx.shape[1] + n
    nt = t // TM
    kern = functools.partial(_gla_in_kernel, dk=dk, dv=dv, n_cast=len(cast))
    row = lambda w: pl.BlockSpec((1, TM, w), lambda b, i: (b, i, 0))
    cast_in, cast_out = _cast_specs(cast, nb, nt)
    return pl.pallas_call(
        kern,
        grid=(nb, nt),
        in_specs=_stream_specs(d) + [
            _mod_spec(d, 0, nb),
            _mod_spec(d, 1, nb),
            _resident((1, d)),
            _resident(w_in.shape),
            _resident(aw1.shape),
            _resident(aw2.shape),
            _resident(ab.shape),
            pl.BlockSpec((TM, LANES), lambda b, i: (i, 0)),
            pl.BlockSpec((TM, LANES), lambda b, i: (i, 0)),
        ] + cast_in,
        out_specs=[row(dk), row(dk), row(dv), row(dv), row(dk), row(dk)] + cast_out,
        out_shape=[
            jax.ShapeDtypeStruct((nb, t, dk), F32),
            jax.ShapeDtypeStruct((nb, t, dk), F32),
            jax.ShapeDtypeStruct((nb, t, dv), BF16),
            jax.ShapeDtypeStruct((nb, t, dv), BF16),
            jax.ShapeDtypeStruct((nb, t, dk), F32),
            jax.ShapeDtypeStruct((nb, t, dk), F32),
        ] + [jax.ShapeDtypeStruct(a.shape[1:], BF16) for a, _ in cast],
        compiler_params=_cparams(("arbitrary", "arbitrary")),
        name="gla_in",
    )(ctx, x, mod, mod, norm_g, w_in, aw1, aw2, ab, cos_t, sin_t, *[a for a, _ in cast])


def _row_bcast(b, period, offset):
    c, w = b.shape
    if period == c:
        return jnp.broadcast_to(b[offset:offset + 1, :], (c, w))
    b3 = b.reshape(c // period, period, w)
    return jnp.broadcast_to(b3[:, offset:offset + 1, :], b3.shape).reshape(c, w)


def _pair_scores(lhs, rhs, first):
    zero = jnp.zeros_like(rhs)
    blockdiag = jnp.concatenate([jnp.where(first, rhs, zero), jnp.where(first, zero, rhs)], axis=0)
    return lax.dot_general(lhs, blockdiag, (((1,), (1,)), ((), ())), preferred_element_type=F32)


def _gla_pair(q, k, b, rev, kside, same, causal, first):
    c = q.shape[0]
    b_last = b[0:1, :] if rev else b[c - 1:c, :]
    q_in = (q * jnp.exp2(b)).astype(BF16)
    k_out = (k * jnp.exp2(b_last - b)).astype(BF16)
    b_ref = _row_bcast(b, GLA_DIAG, GLA_DIAG - 1 if rev else 0)
    qd = (q * jnp.exp2(b - b_ref)).astype(BF16)
    kd = (k * jnp.exp2(b_ref - b)).astype(BF16)
    att = _pair_scores(qd, kd, first)
    period = 2 * GLA_DIAG
    while period <= c:
        half = period // 2
        b_ref = _row_bcast(b, period, half if rev else half - 1)
        u = (jnp.where(kside[period], k, q) * jnp.exp2(-jnp.abs(b - b_ref))).astype(BF16)
        att = jnp.where(same[half], att, _pair_scores(u, u, first))
        period *= 2
    att = jnp.where(causal, att, 0.0).astype(BF16)
    return q_in, k_out, att, jnp.exp2(b_last)


def _gla_state_step(q_in, att, k_out, v, decay_row, s):
    lhs = jnp.concatenate([q_in, att], axis=1)
    rhs = jnp.concatenate([s.astype(BF16), v], axis=0)
    o = jnp.dot(lhs, rhs, preferred_element_type=F32)
    decay = jnp.transpose(jnp.broadcast_to(decay_row, (LANES, LANES)))
    decay = jnp.concatenate([decay] * (s.shape[1] // LANES), axis=1)
    kv = lax.dot_general(k_out, v, (((0,), (0,)), ((), ())), preferred_element_type=F32)
    return o, s * decay + kv


def _gla_core_kernel(qf_ref, kf_ref, vf_ref, gf_ref, qb_ref, kb_ref, vb_ref, gb_ref,
                     of_ref, ob_ref, s_ref, *, heads, dvh):
    @pl.when(pl.program_id(1) == 0)
    def _():
        s_ref[...] = jnp.zeros_like(s_ref)

    c = GLA_C
    assert c == LANES and heads % 2 == 0
    row = lax.broadcasted_iota(jnp.int32, (c, 2 * LANES), 0)
    first = lax.broadcasted_iota(jnp.int32, (c, 2 * LANES), 1) < LANES
    ti = lax.broadcasted_iota(jnp.int32, (c, 2 * c), 0)
    tj = lax.broadcasted_iota(jnp.int32, (c, 2 * c), 1) & (c - 1)
    x = ti ^ tj
    same = {}
    half = GLA_DIAG
    while half < c:
        same[half] = x < half
        half *= 2

    n_chunks = qf_ref.shape[1] // c
    dirs = ((qf_ref, kf_ref, vf_ref, gf_ref, of_ref), (qb_ref, kb_ref, vb_ref, gb_ref, ob_ref))
    masks = []
    for d in range(2):
        rev = d == 1
        kside = {}
        period = 2 * GLA_DIAG
        while period <= c:
            off = row & (period - 1)
            kside[period] = (off >= period // 2) if rev else (off < period // 2)
            period *= 2
        masks.append((kside, (tj >= ti) if rev else (tj <= ti)))

    pre = {}
    for step in range(n_chunks):
        for d, (q_ref, k_ref, v_ref, b_ref, o_ref) in enumerate(dirs):
            rev = d == 1
            ch = n_chunks - 1 - step if rev else step
            rows = slice(ch * c, (ch + 1) * c)
            for pr in range(heads // 2):
                psl = slice(2 * pr * LANES, (2 * pr + 2) * LANES)
                pre[step, d, pr] = _gla_pair(q_ref[0, rows, psl], k_ref[0, rows, psl], b_ref[0, rows, psl],
                                             rev, masks[d][0], same, masks[d][1], first)
    for step in range(n_chunks):
        for d, (q_ref, k_ref, v_ref, b_ref, o_ref) in enumerate(dirs):
            ch = n_chunks - 1 - step if d == 1 else step
            rows = slice(ch * c, (ch + 1) * c)
            for pr in range(heads // 2):
                q_in, k_out, att, decay = pre[step, d, pr]
                for j in range(2):
                    hd = 2 * pr + j
                    hsl = slice(j * LANES, (j + 1) * LANES)
                    vsl = slice(hd * dvh, (hd + 1) * dvh)
                    o, s_new = _gla_state_step(q_in[:, hsl], att[:, j * c:(j + 1) * c], k_out[:, hsl],
                                               v_ref[0, rows, vsl], decay[:, hsl], s_ref[d, hd])
                    o_ref[0, rows, vsl] = o.astype(o_ref.dtype)
                    s_ref[d, hd] = s_new


def _gla_core(q, k, v, gf, gb, heads, n_ctx):
    nb, t, dk = q.shape
    dv = v.shape[-1]
    assert n_ctx % GLA_BLK == 0 and t % GLA_BLK == 0
    nc = t // GLA_BLK
    ncx = n_ctx // GLA_BLK
    fwd = lambda b, s: (b, s, 0)
    bwd = lambda b, s: (b, jnp.where(s < ncx, ncx - 1 - s, nc - 1 + ncx - s), 0)
    kern = functools.partial(_gla_core_kernel, heads=heads, dvh=dv // heads)
    blk = lambda w, m: pl.BlockSpec((1, GLA_BLK, w), m)
    return pl.pallas_call(
        kern,
        grid=(nb, nc),
        in_specs=[blk(dk, fwd), blk(dk, fwd), blk(dv, fwd), blk(dk, fwd),
                  blk(dk, bwd), blk(dk, bwd), blk(dv, bwd), blk(dk, bwd)],
        out_specs=[blk(dv, fwd), blk(dv, bwd)],
        out_shape=[jax.ShapeDtypeStruct((nb, t, dv), BF16)] * 2,
        scratch_shapes=[pltpu.VMEM((2, heads, dk // heads, dv // heads), F32)],
        compiler_params=_cparams(("parallel", "arbitrary")),
        name="gla_core",
    )(q, k, v, gf, q, k, v, gb)


def _gla_out_kernel(of_ref, ob_ref, r_ref, c_ref, x_ref, gate_ref, sh2_ref, sc2_ref, ng_ref, w_ref, g2_ref,
                    xo_ref, ho_ref, *, heads):
    o = of_ref[0].astype(F32) + ob_ref[0].astype(F32)
    r = r_ref[0].astype(F32)
    dvh = o.shape[1] // heads
    parts = []
    for hd in range(heads):
        oh = o[:, hd * dvh:(hd + 1) * dvh]
        ms = jnp.mean(oh * oh, axis=-1, keepdims=True)
        parts.append(oh * lax.rsqrt(ms + EPS) * ng_ref[...])
    y = jnp.concatenate(parts, axis=1) * _silu(r)
    out = jnp.dot(y.astype(BF16), w_ref[...], preferred_element_type=F32)
    x_mid = _stream_tile(c_ref, x_ref) + gate_ref[0] * out
    xo_ref[0] = x_mid
    ho_ref[0] = _norm_mod(x_mid, g2_ref[...], sh2_ref[0], sc2_ref[0]).astype(BF16)


def _gla_out(o_f, o_b, r, ctx, x, mod, norm_g, w_out, norm2_g, heads):
    nb, t, dv = o_f.shape
    d = x.shape[-1]
    nt = t // TM
    row = lambda w: pl.BlockSpec((1, TM, w), lambda b, i: (b, i, 0))
    out = pl.BlockSpec((1, TM, d), lambda b, i: (b, jnp.where(i == 0, nt - 1, i - 1), 0))
    return pl.pallas_call(
        functools.partial(_gla_out_kernel, heads=heads),
        grid=(nb, nt),
        in_specs=[row(dv), row(dv), row(dv)] + _stream_specs(d) + [
            _mod_spec(d, 2, nb), _mod_spec(d, 3, nb), _mod_spec(d, 4, nb),
            _resident(norm_g.shape), _resident(w_out.shape), _resident((1, d))],
        out_specs=[out, out],
        out_shape=[jax.ShapeDtypeStruct((nb, t, d), F32), jax.ShapeDtypeStruct((nb, t, d), BF16)],
        compiler_params=_cparams(("parallel", "parallel")),
        name="gla_out",
    )(o_f, o_b, r, ctx, x, mod, mod, mod, norm_g, w_out, norm2_g)


def _ffn_kernel(h_ref, hp_ref, hn_ref, x_ref, gate_ref, wi_ref, cw_ref, cb_ref, wo_ref, fg_ref,
                *rest, chunks, final_norm, n_cast):
    o_ref = rest[n_cast]
    hbuf_ref, act_ref = rest[-2:]
    _cast_blocks(rest[:n_cast], rest[n_cast + 1:-2])
    i = pl.program_id(1)
    tm = h_ref.shape[1]
    dff = act_ref.shape[1]
    hs = SUBLANES
    has_prev = i > 0
    has_next = i < pl.num_programs(1) - 1
    hbuf_ref[0:tm, :] = h_ref[0]
    nxt = jnp.where(has_next, hn_ref[0, 0:hs, :].astype(F32), 0.0)
    prv = jnp.where(has_prev, hp_ref[0, hs:2 * hs, :].astype(F32), 0.0)
    hbuf_ref[tm:, :] = jnp.concatenate([nxt, prv], axis=0).astype(BF16)
    n = tm + 2 * hs

    for (c0, cw) in chunks:
        a = jnp.dot(hbuf_ref[...], wi_ref[:, c0:c0 + cw], preferred_element_type=F32)
        val = jnp.dot(hbuf_ref[0:tm, :], wi_ref[:, dff + c0:dff + c0 + cw], preferred_element_type=F32)
        w3 = cw_ref[:, c0:c0 + cw]
        conv = (pltpu.roll(a, 1, 0)[0:tm] * w3[0:1] + a[0:tm] * w3[1:2]
                + pltpu.roll(a, n - 1, 0)[0:tm] * w3[2:3] + cb_ref[:, c0:c0 + cw])
        act_ref[:, c0:c0 + cw] = (_silu(conv) * val).astype(BF16)
    y = x_ref[0] + gate_ref[0] * jnp.dot(act_ref[...], wo_ref[...], preferred_element_type=F32)
    if final_norm:
        ms = jnp.mean(y * y, axis=-1, keepdims=True)
        y = y * lax.rsqrt(ms + EPS) * fg_ref[...]
    o_ref[0] = y


def _ffn(h, x, mod, w_in, conv_w, conv_b, w_out, final_g, *, row0, rows, is_ctx, final_norm, cast=()):
    nb, _, d = x.shape
    t = rows
    dff = w_out.shape[0]
    tm = min(FFN_TM, t)
    hb = 2 * SUBLANES
    bpt = tm // hb
    assert row0 % tm == 0 and rows % tm == 0
    t0 = row0 // tm
    b0 = row0 // hb
    nblk = t // hb
    mod_row = (lambda b: nb) if is_ctx else (lambda b: b)
    chunks = []
    c0 = 0
    while c0 < dff:
        cw = min(FFN_CHUNK, dff - c0)
        chunks.append((c0, cw))
        c0 += cw
    kern = functools.partial(_ffn_kernel, chunks=tuple(chunks), final_norm=final_norm, n_cast=len(cast))
    row = pl.BlockSpec((1, tm, d), lambda b, i: (b, t0 + i, 0))
    cast_in, cast_out = _cast_specs(cast, nb, t // tm)
    out = pl.pallas_call(
        kern,
        grid=(nb, t // tm),
        in_specs=[
            row,
            pl.BlockSpec((1, hb, d), lambda b, i: (b, b0 + jnp.maximum(i * bpt - 1, 0), 0)),
            pl.BlockSpec((1, hb, d), lambda b, i: (b, b0 + jnp.minimum((i + 1) * bpt, nblk - 1), 0)),
            row,
            pl.BlockSpec((1, 1, d), lambda b, i: (mod_row(b), 0, 5)),
            _resident(w_in.shape),
            _resident(conv_w.shape),
            _resident(conv_b.shape),
            _resident(w_out.shape),
            _resident((1, d)),
        ] + cast_in,
        out_specs=[pl.BlockSpec((1, tm, d), lambda b, i: (b, i, 0))] + cast_out,
        out_shape=[jax.ShapeDtypeStruct((nb, t, d), F32)]
        + [jax.ShapeDtypeStruct(a.shape[1:], BF16) for a, _ in cast],
        scratch_shapes=[pltpu.VMEM((tm + 2 * SUBLANES, d), BF16), pltpu.VMEM((tm, dff), BF16)],
        compiler_params=_cparams(("arbitrary", "arbitrary") if cast else ("parallel", "parallel")),
        name=("ffn_final" if final_norm else "ffn") + ("_ctx" if is_ctx else ""),
    )(h, h, h, x, mod, w_in, conv_w, conv_b, w_out, final_g, *[a for a, _ in cast])
    return out if cast else out[0]


def _na_qkv_kernel(c_ref, x_ref, sh_ref, sc_ref, g_ref, w_ref, q_ref, k_ref, v_ref, *, d, qscale):
    h = _norm_mod(_stream_tile(c_ref, x_ref), g_ref[...], sh_ref[0], sc_ref[0]).astype(BF16)
    proj = jnp.dot(h, w_ref[...], preferred_element_type=F32)
    q_ref[0] = (proj[:, :d] * qscale).astype(BF16)
    k_ref[0] = proj[:, d:2 * d].astype(BF16)
    v_ref[0] = proj[:, 2 * d:].astype(BF16)


def _na_qkv(ctx, x, mod, norm_g, w_qkv, head_dim):
    nb, n, d = x.shape
    t = ctx.shape[1] + n
    row = lambda w: pl.BlockSpec((1, TM, w), lambda b, i: (b, i, 0))
    return pl.pallas_call(
        functools.partial(_na_qkv_kernel, d=d, qscale=LOG2E * float(head_dim) ** -0.5),
        grid=(nb, t // TM),
        in_specs=_stream_specs(d) + [_mod_spec(d, 0, nb), _mod_spec(d, 1, nb),
                                     _resident((1, d)), _resident(w_qkv.shape)],
        out_specs=[row(d)] * 3,
        out_shape=[jax.ShapeDtypeStruct((nb, t, d), BF16)] * 3,
        compiler_params=_cparams(("parallel", "parallel")),
        name="na_qkv",
    )(ctx, x, mod, mod, norm_g, w_qkv)


def _na_attn_kernel(q_ref, k_ref, v_ref, bias_ref, o_ref, *, n_ctx, rows_n, head_dim):
    rb = pl.program_id(2)
    hw = q_ref.shape[-1]
    nwin = NA_KH * GRID_W
    nt = (((1,), (1,)), ((), ()))
    lane = lax.broadcasted_iota(jnp.int32, (GRID_W, hw), 1)
    hmask = [(lane // head_dim) == hd for hd in range(NA_HG)]

    def window(i):
        r = rb * NA_RB + i
        rs = jnp.clip(r - NA_KH // 2, 0, rows_n - NA_KH)
        lo = rs - r + (NA_KH - 1)
        return r, lo, pl.multiple_of(n_ctx + rs * GRID_W, GRID_W)

    def scores(i):
        r, lo, start = window(i)
        q = q_ref[0, pl.ds(pl.multiple_of(n_ctx + r * GRID_W, GRID_W), GRID_W), :]
        qs = jnp.concatenate([jnp.where(hmask[hd], q, jnp.zeros_like(q)) for hd in range(NA_HG)], axis=0)
        bias = jnp.concatenate(
            [jnp.concatenate([bias_ref[0, hd, lo + 2 * m] for m in range(NA_KH // 2)], axis=1)
             for hd in range(NA_HG)], axis=0)
        s_nb = lax.dot_general(qs, k_ref[0, pl.ds(start, nwin), :], nt, preferred_element_type=F32) + bias
        s_cx = lax.dot_general(qs, k_ref[0, 0:n_ctx, :], nt, preferred_element_type=F32)
        s = jnp.concatenate([s_nb, s_cx], axis=1)
        return s, jnp.max(s, axis=-1, keepdims=True)

    def finish(i, s, m):
        r, lo, start = window(i)
        p = jnp.exp2(s - m)
        l = jnp.sum(p, axis=-1, keepdims=True)
        p = p.astype(BF16)
        o = (jnp.dot(p[:, :nwin], v_ref[0, pl.ds(start, nwin), :], preferred_element_type=F32)
             + jnp.dot(p[:, nwin:], v_ref[0, 0:n_ctx, :], preferred_element_type=F32))
        o = o * (1.0 / l)
        out = o[0:GRID_W]
        for hd in range(1, NA_HG):
            out = jnp.where(hmask[hd], o[hd * GRID_W:(hd + 1) * GRID_W], out)
        o_ref[0, pl.ds(pl.multiple_of(i * GRID_W, GRID_W), GRID_W), :] = out.astype(BF16)

    def body(it, carry):
        nxt = scores(it * NA_U)
        for u in range(NA_U):
            cur = nxt
            if u + 1 < NA_U:
                nxt = scores(it * NA_U + u + 1)
            finish(it * NA_U + u, *cur)
        return carry

    lax.fori_loop(0, NA_RB // NA_U, body, 0)


def _na_bias(rpb):
    heads, ndr, ndc = rpb.shape
    c = np.arange(GRID_W)
    cs = np.clip(c - NA_KW // 2, 0, GRID_W - NA_KW)
    onehot = np.zeros((2 * ndc, GRID_W, 2 * GRID_W), np.float32)
    mask = np.full((GRID_W, 2 * GRID_W), NEG, np.float32)
    for half in range(2):
        for cq in range(GRID_W):
            for kc in range(cs[cq], cs[cq] + NA_KW):
                onehot[half * ndc + kc - cq + NA_KW - 1, cq, half * GRID_W + kc] = 1.0
                mask[cq, half * GRID_W + kc] = 0.0
    pair = jnp.concatenate([rpb[:, :ndr - 1, :], rpb[:, 1:, :]], axis=-1)
    tiles = jnp.einsum('hdk,kcl->hdcl', pair * LOG2E, onehot, precision=lax.Precision.HIGHEST) + mask
    return tiles.reshape(heads // NA_HG, NA_HG, ndr - 1, GRID_W, 2 * GRID_W)


def _na_attn(q, k, v, bias, n_ctx, head_dim):
    nb, t, d = q.shape
    rows_n = (t - n_ctx) // GRID_W
    nrb = rows_n // NA_RB
    nq = NA_RB * GRID_W
    hw = NA_HG * head_dim
    ngrp = d // hw
    assert n_ctx % nq == 0 or nq % n_ctx == 0
    kern = functools.partial(_na_attn_kernel, n_ctx=n_ctx, rows_n=rows_n, head_dim=head_dim)
    return pl.pallas_call(
        kern,
        grid=(nb, ngrp, nrb),
        in_specs=[
            pl.BlockSpec((1, t, hw), lambda b, g, r: (b, 0, g)),
            pl.BlockSpec((1, t, hw), lambda b, g, r: (b, 0, g)),
            pl.BlockSpec((1, t, hw), lambda b, g, r: (b, 0, g)),
            pl.BlockSpec((1,) + bias.shape[1:], lambda b, g, r: (g, 0, 0, 0, 0)),
        ],
        out_specs=pl.BlockSpec((1, nq, hw), lambda b, g, r: (b, r, g)),
        out_shape=jax.ShapeDtypeStruct((nb, t - n_ctx, d), BF16),
        compiler_params=_cparams(("parallel", "parallel", "arbitrary")),
        name="na_attn",
    )(q, k, v, bias)


def _na_out_kernel(a_ref, x_ref, gate_ref, sh2_ref, sc2_ref, w_ref, g2_ref, o_ref, h_ref):
    out = jnp.dot(a_ref[0], w_ref[...], preferred_element_type=F32)
    x_mid = x_ref[0] + gate_ref[0] * out
    o_ref[0] = x_mid
    h_ref[0] = _norm_mod(x_mid, g2_ref[...], sh2_ref[0], sc2_ref[0]).astype(BF16)


def _na_out(attn, x, mod, w_out, norm2_g):
    nb, n, d = x.shape
    tm = FFN_TM
    row = pl.BlockSpec((1, tm, d), lambda b, i: (b, i, 0))
    seg = lambda s: pl.BlockSpec((1, 1, d), lambda b, i: (b, 0, s))
    return pl.pallas_call(
        _na_out_kernel,
        grid=(nb, n // tm),
        in_specs=[row, row, seg(2), seg(3), seg(4), _resident(w_out.shape), _resident((1, d))],
        out_specs=[row, row],
        out_shape=[jax.ShapeDtypeStruct((nb, n, d), F32), jax.ShapeDtypeStruct((nb, n, d), BF16)],
        compiler_params=_cparams(("parallel", "parallel")),
        name="na_out",
    )(attn, x, mod, mod, mod, w_out, norm2_g)


def _rope_tables(n_lat, n_ctx):
    half = LANES // 2
    inv = 1.0 / (ROPE_BASE ** (np.arange(0, half, 2, dtype=np.float64) / half))
    pos = np.arange(n_lat)
    rows = (pos // GRID_W).astype(np.float64)[:, None] * inv[None, :]
    cols = (pos % GRID_W).astype(np.float64)[:, None] * inv[None, :]
    cos = np.concatenate([np.cos(rows)] * 2 + [np.cos(cols)] * 2, axis=1)
    sin = np.concatenate([-np.sin(rows), np.sin(rows), -np.sin(cols), np.sin(cols)], axis=1)
    cos = np.concatenate([np.ones((n_ctx, LANES)), cos], axis=0).astype(np.float32)
    sin = np.concatenate([np.zeros((n_ctx, LANES)), sin], axis=0).astype(np.float32)
    return jnp.asarray(cos), jnp.asarray(sin)


def kernel(x, c, ctx, c_ctx, ada_w, ada_b, norm1_g, norm2_g, ffn_w_in, ffn_conv_w, ffn_conv_b, ffn_w_out,
           gla_w_in, gla_a_w1, gla_a_w2, gla_a_b, gla_norm_g, gla_w_out, na_w_qkv, na_rpb, na_w_out, final_g):
    nb, n, d = x.shape
    n_ctx = ctx.shape[1]
    depth = ada_w.shape[0]
    dff = ffn_conv_b.shape[-1]
    assert n_ctx == TM and n % TM == 0 and depth == 2 and nb + 1 <= SUBLANES

    mod, w_gla_in, w_gla_out = _ada(jnp.concatenate([c, c_ctx[None, :]], axis=0), ada_w, ada_b,
                                    cast=((gla_w_in, 0), (gla_w_out, 0)))

    m0 = mod[0].reshape(SUBLANES, 1, 6 * d)
    dk = gla_a_w2.shape[-1]
    dv = gla_w_out.shape[1]
    assert 4 * 2 * GLA_LOW_RANK == LANES
    aw1 = jnp.concatenate([gla_a_w1[0, 0], gla_a_w1[0, 1]] * 4, axis=1).astype(BF16)
    zeros = jnp.zeros((GLA_LOW_RANK, dk), F32)
    w2 = jnp.concatenate([jnp.concatenate([gla_a_w2[0, 0], zeros], axis=1),
                          jnp.concatenate([zeros, gla_a_w2[0, 1]], axis=1)], axis=0)
    w2_hi = w2.astype(BF16)
    w2_lo = (w2 - w2_hi.astype(F32)).astype(BF16)
    aw2 = jnp.concatenate([w2_hi, w2_hi, w2_lo, w2_lo], axis=0)
    cos_t, sin_t = _rope_tables(n, n_ctx)
    q, k, v, r, gf, gb, ffn_wi0, ffn_wo0 = _gla_in(
        ctx, x, m0, norm1_g[0].reshape(1, d), w_gla_in, aw1, aw2,
        gla_a_b[0].reshape(1, 2 * dk), cos_t, sin_t, dk, dv, cast=((ffn_w_in, 0), (ffn_w_out, 0)))
    o_f, o_b = _gla_core(q, k, v, gf, gb, GLA_HEADS, n_ctx)
    x_mid, h_mid = _gla_out(o_f, o_b, r, ctx, x, m0, gla_norm_g[0].reshape(1, dv // GLA_HEADS),
                            w_gla_out, norm2_g[0].reshape(1, d), GLA_HEADS)
    ffn0 = functools.partial(_ffn, h_mid, x_mid, m0, ffn_wi0, ffn_conv_w[0], ffn_conv_b[0].reshape(1, dff),
                             ffn_wo0, final_g.reshape(1, d), final_norm=False)
    x_lat, ffn_wi1, ffn_wo1, w_qkv, w_na_out = ffn0(
        row0=0, rows=n, is_ctx=False, cast=((ffn_w_in, 1), (ffn_w_out, 1), (na_w_qkv, 0), (na_w_out, 0)))
    x_ctx = ffn0(row0=n, rows=n_ctx, is_ctx=True)

    m1 = mod[1].reshape(SUBLANES, 1, 6 * d)
    head_dim = d // NA_HEADS
    qn, kn, vn = _na_qkv(x_ctx, x_lat, m1, norm1_g[1].reshape(1, d), w_qkv, head_dim)
    bias = _na_bias(na_rpb[0])
    attn = _na_attn(qn, kn, vn, bias, n_ctx, head_dim)
    x_lat, h_lat = _na_out(attn, x_lat, m1, w_na_out, norm2_g[1].reshape(1, d))
    return _ffn(h_lat, x_lat, m1, ffn_wi1, ffn_conv_w[1], ffn_conv_b[1].reshape(1, dff), ffn_wo1,
                final_g.reshape(1, d), row0=0, rows=n, is_ctx=False, final_norm=True)
```

```python
import functools

import numpy as np
import jax
import jax.numpy as jnp
from jax import lax
from jax.experimental import pallas as pl
from jax.experimental.pallas import tpu as pltpu

GRID_W = 64
GLA_HEADS = 4
GLA_GATE_NORM = 16.0
GLA_LOW_RANK = 16
ROPE_BASE = 10000.0
NA_HEADS = 16
NA_KH = 8
NA_KW = 16
EPS = 1e-6
LOG2E = 1.4426950408889634

LANES = 128
SUBLANES = 8
VMEM_LIMIT = 56 * 1024 * 1024

TM = 256
ADA_TK = 256
GLA_C = 128
GLA_BLK = 256
GLA_DIAG = 16
FFN_TM = 1024
FFN_CHUNK = 512
NA_RB = 32
NA_U = 32
NA_HG = 4
NEG = -1e30

F32 = jnp.float32
BF16 = jnp.bfloat16


def _cparams(sem):
    return pltpu.CompilerParams(dimension_semantics=sem, vmem_limit_bytes=VMEM_LIMIT)


def _resident(shape):
    nd = len(shape)
    return pl.BlockSpec(shape, lambda *_: (0,) * nd, pipeline_mode=pl.Buffered(1))


def _silu(x):
    return x * (1.0 / (1.0 + jnp.exp(-x)))


def _norm_mod(x, g, shift, scale):
    ms = jnp.mean(x * x, axis=-1, keepdims=True)
    return x * lax.rsqrt(ms + EPS) * (g * (1.0 + scale)) + shift


def _ada_kernel(c_ref, w_ref, b_ref, *rest, n_rows, n_cast):
    o_ref = rest[n_cast]
    _cast_blocks(rest[:n_cast], rest[n_cast + 1:])
    j = pl.program_id(1)
    tk, d6 = w_ref.shape[1:]
    k0 = pl.multiple_of(j * tk, tk)
    rows = []
    for r in range(n_rows):
        sc = _silu(c_ref[r, pl.ds(k0, tk), :])
        parts = [jnp.sum(w_ref[0, :, n * LANES:(n + 1) * LANES] * sc, axis=0, keepdims=True)
                 for n in range(d6 // LANES)]
        rows.append(jnp.concatenate(parts, axis=1))
    rows.append(jnp.zeros((SUBLANES - n_rows, d6), F32))
    part = jnp.concatenate(rows, axis=0)

    @pl.when(j == 0)
    def _():
        o_ref[0] = part + b_ref[0]

    @pl.when(j > 0)
    def _():
        o_ref[0] += part


def _ada(c_rows, ada_w, ada_b, cast):
    depth, d, d6 = ada_w.shape
    n_rows = c_rows.shape[0]
    tk = ADA_TK
    c_b = jnp.broadcast_to(c_rows[:, :, None], (n_rows, d, LANES))
    cast_in, cast_out = _cast_specs(cast, depth, d // tk)
    return pl.pallas_call(
        functools.partial(_ada_kernel, n_rows=n_rows, n_cast=len(cast)),
        grid=(depth, d // tk),
        in_specs=[
            pl.BlockSpec((n_rows, d, LANES), lambda i, j: (0, 0, 0)),
            pl.BlockSpec((1, tk, d6), lambda i, j: (i, j, 0)),
            pl.BlockSpec((1, 1, d6), lambda i, j: (i, 0, 0)),
        ] + cast_in,
        out_specs=[pl.BlockSpec((1, SUBLANES, d6), lambda i, j: (i, 0, 0))] + cast_out,
        out_shape=[jax.ShapeDtypeStruct((depth, SUBLANES, d6), F32)]
        + [jax.ShapeDtypeStruct(a.shape[1:], BF16) for a, _ in cast],
        compiler_params=_cparams(("arbitrary", "arbitrary")),
        name="ada_mod",
    )(c_b, ada_w, ada_b.reshape(depth, 1, d6), *[a for a, _ in cast])


def _mod_spec(d, seg, nb):
    return pl.BlockSpec((1, 1, d), lambda b, i: (jnp.where(i == 0, nb, b), 0, seg))


def _stream_specs(d):
    return [pl.BlockSpec((1, TM, d), lambda b, i: (b, 0, 0)),
            pl.BlockSpec((1, TM, d), lambda b, i: (b, jnp.maximum(i - 1, 0), 0))]


def _stream_tile(ctx_ref, x_ref):
    return jnp.where(pl.program_id(1) == 0, ctx_ref[0], x_ref[0])


def _cast_specs(arrays, nb, steps):
    in_specs, out_specs = [], []
    for a, layer in arrays:
        _, rows, cols = a.shape
        tiles = rows // (2 * SUBLANES)
        nblk = max(k for k in range(1, nb * steps + 1) if tiles % k == 0)
        in_specs.append(pl.BlockSpec(
            (1, rows // nblk, cols),
            lambda b, i, nblk=nblk, layer=layer: (layer, jnp.minimum(b * steps + i, nblk - 1), 0)))
        out_specs.append(pl.BlockSpec(
            (rows // nblk, cols), lambda b, i, nblk=nblk: (jnp.minimum(b * steps + i, nblk - 1), 0)))
    return in_specs, out_specs


def _cast_blocks(src_refs, dst_refs):
    for src, dst in zip(src_refs, dst_refs):
        dst[...] = src[0].astype(BF16)


def _gla_in_kernel(c_ref, x_ref, sh_ref, sc_ref, g_ref, w_ref, aw1_ref, aw2_ref, ab_ref,
                   cos_ref, sin_ref, *rest, dk, dv, n_cast):
    q_ref, k_ref, v_ref, r_ref, gf_ref, gb_ref = rest[n_cast:n_cast + 6]
    _cast_blocks(rest[:n_cast], rest[n_cast + 6:])
    h = _norm_mod(_stream_tile(c_ref, x_ref), g_ref[...], sh_ref[0], sc_ref[0]).astype(BF16)

    z = jnp.dot(h, aw1_ref[...], preferred_element_type=F32)
    z_hi = z.astype(BF16)
    z_lo = (z - z_hi.astype(F32)).astype(BF16)
    grp = lax.broadcasted_iota(jnp.int32, z.shape, 1) // (2 * GLA_LOW_RANK)
    zc = jnp.where((grp & 1) == 0, z_hi, z_lo)
    pre = jnp.dot(zc, aw2_ref[...], preferred_element_type=F32) + ab_ref[...]

    qk = jnp.dot(h, w_ref[:, :2 * dk], preferred_element_type=F32)
    g = (jnp.minimum(pre, 0.0) - jnp.log(1.0 + jnp.exp(-jnp.abs(pre)))) * (LOG2E / GLA_GATE_NORM)
    g_hi = g.astype(BF16)
    g_lo = (g - g_hi.astype(F32)).astype(BF16)
    v_ref[0] = jnp.dot(h, w_ref[:, 2 * dk: 2 * dk + dv], preferred_element_type=F32).astype(BF16)

    cos = cos_ref[...]
    sin = sin_ref[...]
    lane = lax.broadcasted_iota(jnp.int32, cos.shape, 1)
    low = (lane & 32) == 0
    qscale = float(LANES) ** -0.5

    def rope(t):
        partner = jnp.where(low, pltpu.roll(t, LANES - 32, 1), pltpu.roll(t, 32, 1))
        return t * cos + partner * sin

    for hd in range(dk // LANES):
        sl = slice(hd * LANES, (hd + 1) * LANES)
        q_ref[0, :, sl] = rope(qk[:, sl]) * qscale
        k_ref[0, :, sl] = rope(qk[:, dk + hd * LANES: dk + (hd + 1) * LANES])

    ti = lax.broadcasted_iota(jnp.int32, (GLA_C, GLA_C), 0)
    tj = lax.broadcasted_iota(jnp.int32, (GLA_C, GLA_C), 1)
    for o_ref, tri, sl in ((gf_ref, tj <= ti, slice(0, dk)), (gb_ref, tj >= ti, slice(dk, 2 * dk))):
        tri = jnp.where(tri, 1.0, 0.0).astype(BF16)
        tri2 = jnp.concatenate([tri, tri], axis=1)
        for ch in range(TM // GLA_C):
            rows = slice(ch * GLA_C, (ch + 1) * GLA_C)
            o_ref[0, rows, :] = jnp.dot(tri2, jnp.concatenate([g_hi[rows, sl], g_lo[rows, sl]], axis=0),
                                        preferred_element_type=F32)
    r_ref[0] = jnp.dot(h, w_ref[:, 2 * dk + dv:], preferred_element_type=F32).astype(BF16)


def _gla_in(ctx, x, mod, norm_g, w_in, aw1, aw2, ab, cos_t, sin_t, dk, dv, cast):
    nb, n, d = x.shape
    t = ctx.shape[1] + n
    nt = t // TM
    kern = functools.partial(_gla_in_kernel, dk=dk, dv=dv, n_cast=len(cast))
    row = lambda w: pl.BlockSpec((1, TM, w), lambda b, i: (b, i, 0))
    cast_in, cast_out = _cast_specs(cast, nb, nt)
    return pl.pallas_call(
        kern,
        grid=(nb, nt),
        in_specs=_stream_specs(d) + [
            _mod_spec(d, 0, nb),
            _mod_spec(d, 1, nb),
            _resident((1, d)),
            _resident(w_in.shape),
            _resident(aw1.shape),
            _resident(aw2.shape),
            _resident(ab.shape),
            pl.BlockSpec((TM, LANES), lambda b, i: (i, 0)),
            pl.BlockSpec((TM, LANES), lambda b, i: (i, 0)),
        ] + cast_in,
        out_specs=[row(dk), row(dk), row(dv), row(dv), row(dk), row(dk)] + cast_out,
        out_shape=[
            jax.ShapeDtypeStruct((nb, t, dk), F32),
            jax.ShapeDtypeStruct((nb, t, dk), F32),
            jax.ShapeDtypeStruct((nb, t, dv), BF16),
            jax.ShapeDtypeStruct((nb, t, dv), BF16),
            jax.ShapeDtypeStruct((nb, t, dk), F32),
            jax.ShapeDtypeStruct((nb, t, dk), F32),
        ] + [jax.ShapeDtypeStruct(a.shape[1:], BF16) for a, _ in cast],
        compiler_params=_cparams(("arbitrary", "arbitrary")),
        name="gla_in",
    )(ctx, x, mod, mod, norm_g, w_in, aw1, aw2, ab, cos_t, sin_t, *[a for a, _ in cast])


def _row_bcast(b, period, offset):
    c, w = b.shape
    if period == c:
        return jnp.broadcast_to(b[offset:offset + 1, :], (c, w))
    b3 = b.reshape(c // period, period, w)
    return jnp.broadcast_to(b3[:, offset:offset + 1, :], b3.shape).reshape(c, w)


def _pair_scores(lhs, rhs, first):
    zero = jnp.zeros_like(rhs)
    blockdiag = jnp.concatenate([jnp.where(first, rhs, zero), jnp.where(first, zero, rhs)], axis=0)
    return lax.dot_general(lhs, blockdiag, (((1,), (1,)), ((), ())), preferred_element_type=F32)


def _gla_pair(q, k, b, rev, kside, same, causal, first):
    c = q.shape[0]
    b_last = b[0:1, :] if rev else b[c - 1:c, :]
    q_in = (q * jnp.exp2(b)).astype(BF16)
    k_out = (k * jnp.exp2(b_last - b)).astype(BF16)
    b_ref = _row_bcast(b, GLA_DIAG, GLA_DIAG - 1 if rev else 0)
    qd = (q * jnp.exp2(b - b_ref)).astype(BF16)
    kd = (k * jnp.exp2(b_ref - b)).astype(BF16)
    att = _pair_scores(qd, kd, first)
    period = 2 * GLA_DIAG
    while period <= c:
        half = period // 2
        b_ref = _row_bcast(b, period, half if rev else half - 1)
        u = (jnp.where(kside[period], k, q) * jnp.exp2(-jnp.abs(b - b_ref))).astype(BF16)
        att = jnp.where(same[half], att, _pair_scores(u, u, first))
        period *= 2
    att = jnp.where(causal, att, 0.0).astype(BF16)
    return q_in, k_out, att, jnp.exp2(b_last)


def _gla_state_step(q_in, att, k_out, v, decay_row, s):
    lhs = jnp.concatenate([q_in, att], axis=1)
    rhs = jnp.concatenate([s.astype(BF16), v], axis=0)
    o = jnp.dot(lhs, rhs, preferred_element_type=F32)
    decay = jnp.transpose(jnp.broadcast_to(decay_row, (LANES, LANES)))
    decay = jnp.concatenate([decay] * (s.shape[1] // LANES), axis=1)
    kv = lax.dot_general(k_out, v, (((0,), (0,)), ((), ())), preferred_element_type=F32)
    return o, s * decay + kv


def _gla_core_kernel(qf_ref, kf_ref, vf_ref, gf_ref, qb_ref, kb_ref, vb_ref, gb_ref,
                     of_ref, ob_ref, s_ref, *, heads, dvh):
    @pl.when(pl.program_id(1) == 0)
    def _():
        s_ref[...] = jnp.zeros_like(s_ref)

    c = GLA_C
    assert c == LANES and heads % 2 == 0
    row = lax.broadcasted_iota(jnp.int32, (c, 2 * LANES), 0)
    first = lax.broadcasted_iota(jnp.int32, (c, 2 * LANES), 1) < LANES
    ti = lax.broadcasted_iota(jnp.int32, (c, 2 * c), 0)
    tj = lax.broadcasted_iota(jnp.int32, (c, 2 * c), 1) & (c - 1)
    x = ti ^ tj
    same = {}
    half = GLA_DIAG
    while half < c:
        same[half] = x < half
        half *= 2

    n_chunks = qf_ref.shape[1] // c
    dirs = ((qf_ref, kf_ref, vf_ref, gf_ref, of_ref), (qb_ref, kb_ref, vb_ref, gb_ref, ob_ref))
    masks = []
    for d in range(2):
        rev = d == 1
        kside = {}
        period = 2 * GLA_DIAG
        while period <= c:
            off = row & (period - 1)
            kside[period] = (off >= period // 2) if rev else (off < period // 2)
            period *= 2
        masks.append((kside, (tj >= ti) if rev else (tj <= ti)))

    pre = {}
    for step in range(n_chunks):
        for d, (q_ref, k_ref, v_ref, b_ref, o_ref) in enumerate(dirs):
            rev = d == 1
            ch = n_chunks - 1 - step if rev else step
            rows = slice(ch * c, (ch + 1) * c)
            for pr in range(heads // 2):
                psl = slice(2 * pr * LANES, (2 * pr + 2) * LANES)
                pre[step, d, pr] = _gla_pair(q_ref[0, rows, psl], k_ref[0, rows, psl], b_ref[0, rows, psl],
                                             rev, masks[d][0], same, masks[d][1], first)
    for step in range(n_chunks):
        for d, (q_ref, k_ref, v_ref, b_ref, o_ref) in enumerate(dirs):
            ch = n_chunks - 1 - step if d == 1 else step
            rows = slice(ch * c, (ch + 1) * c)
            for pr in range(heads // 2):
                q_in, k_out, att, decay = pre[step, d, pr]
                for j in range(2):
                    hd = 2 * pr + j
                    hsl = slice(j * LANES, (j + 1) * LANES)
                    vsl = slice(hd * dvh, (hd + 1) * dvh)
                    o, s_new = _gla_state_step(q_in[:, hsl], att[:, j * c:(j + 1) * c], k_out[:, hsl],
                                               v_ref[0, rows, vsl], decay[:, hsl], s_ref[d, hd])
                    o_ref[0, rows, vsl] = o.astype(o_ref.dtype)
                    s_ref[d, hd] = s_new


def _gla_core(q, k, v, gf, gb, heads, n_ctx):
    nb, t, dk = q.shape
    dv = v.shape[-1]
    assert n_ctx % GLA_BLK == 0 and t % GLA_BLK == 0
    nc = t // GLA_BLK
    ncx = n_ctx // GLA_BLK
    fwd = lambda b, s: (b, s, 0)
    bwd = lambda b, s: (b, jnp.where(s < ncx, ncx - 1 - s, nc - 1 + ncx - s), 0)
    kern = functools.partial(_gla_core_kernel, heads=heads, dvh=dv // heads)
    blk = lambda w, m: pl.BlockSpec((1, GLA_BLK, w), m)
    return pl.pallas_call(
        kern,
        grid=(nb, nc),
        in_specs=[blk(dk, fwd), blk(dk, fwd), blk(dv, fwd), blk(dk, fwd),
                  blk(dk, bwd), blk(dk, bwd), blk(dv, bwd), blk(dk, bwd)],
        out_specs=[blk(dv, fwd), blk(dv, bwd)],
        out_shape=[jax.ShapeDtypeStruct((nb, t, dv), BF16)] * 2,
        scratch_shapes=[pltpu.VMEM((2, heads, dk // heads, dv // heads), F32)],
        compiler_params=_cparams(("parallel", "arbitrary")),
        name="gla_core",
    )(q, k, v, gf, q, k, v, gb)


def _gla_out_kernel(of_ref, ob_ref, r_ref, c_ref, x_ref, gate_ref, sh2_ref, sc2_ref, ng_ref, w_ref, g2_ref,
                    xo_ref, ho_ref, *, heads):
    o = of_ref[0].astype(F32) + ob_ref[0].astype(F32)
    r = r_ref[0].astype(F32)
    dvh = o.shape[1] // heads
    parts = []
    for hd in range(heads):
        oh = o[:, hd * dvh:(hd + 1) * dvh]
        ms = jnp.mean(oh * oh, axis=-1, keepdims=True)
        parts.append(oh * lax.rsqrt(ms + EPS) * ng_ref[...])
    y = jnp.concatenate(parts, axis=1) * _silu(r)
    out = jnp.dot(y.astype(BF16), w_ref[...], preferred_element_type=F32)
    x_mid = _stream_tile(c_ref, x_ref) + gate_ref[0] * out
    xo_ref[0] = x_mid
    ho_ref[0] = _norm_mod(x_mid, g2_ref[...], sh2_ref[0], sc2_ref[0]).astype(BF16)


def _gla_out(o_f, o_b, r, ctx, x, mod, norm_g, w_out, norm2_g, heads):
    nb, t, dv = o_f.shape
    d = x.shape[-1]
    nt = t // TM
    row = lambda w: pl.BlockSpec((1, TM, w), lambda b, i: (b, i, 0))
    out = pl.BlockSpec((1, TM, d), lambda b, i: (b, jnp.where(i == 0, nt - 1, i - 1), 0))
    return pl.pallas_call(
        functools.partial(_gla_out_kernel, heads=heads),
        grid=(nb, nt),
        in_specs=[row(dv), row(dv), row(dv)] + _stream_specs(d) + [
            _mod_spec(d, 2, nb), _mod_spec(d, 3, nb), _mod_spec(d, 4, nb),
            _resident(norm_g.shape), _resident(w_out.shape), _resident((1, d))],
        out_specs=[out, out],
        out_shape=[jax.ShapeDtypeStruct((nb, t, d), F32), jax.ShapeDtypeStruct((nb, t, d), BF16)],
        compiler_params=_cparams(("parallel", "parallel")),
        name="gla_out",
    )(o_f, o_b, r, ctx, x, mod, mod, mod, norm_g, w_out, norm2_g)


def _ffn_kernel(h_ref, hp_ref, hn_ref, x_ref, gate_ref, wi_ref, cw_ref, cb_ref, wo_ref, fg_ref,
                *rest, chunks, final_norm, n_cast):
    o_ref = rest[n_cast]
    hbuf_ref, act_ref = rest[-2:]
    _cast_blocks(rest[:n_cast], rest[n_cast + 1:-2])
    i = pl.program_id(1)
    tm = h_ref.shape[1]
    dff = act_ref.shape[1]
    hs = SUBLANES
    has_prev = i > 0
    has_next = i < pl.num_programs(1) - 1
    hbuf_ref[0:tm, :] = h_ref[0]
    nxt = jnp.where(has_next, hn_ref[0, 0:hs, :].astype(F32), 0.0)
    prv = jnp.where(has_prev, hp_ref[0, hs:2 * hs, :].astype(F32), 0.0)
    hbuf_ref[tm:, :] = jnp.concatenate([nxt, prv], axis=0).astype(BF16)
    n = tm + 2 * hs

    for (c0, cw) in chunks:
        a = jnp.dot(hbuf_ref[...], wi_ref[:, c0:c0 + cw], preferred_element_type=F32)
        val = jnp.dot(hbuf_ref[0:tm, :], wi_ref[:, dff + c0:dff + c0 + cw], preferred_element_type=F32)
        w3 = cw_ref[:, c0:c0 + cw]
        conv = (pltpu.roll(a, 1, 0)[0:tm] * w3[0:1] + a[0:tm] * w3[1:2]
                + pltpu.roll(a, n - 1, 0)[0:tm] * w3[2:3] + cb_ref[:, c0:c0 + cw])
        act_ref[:, c0:c0 + cw] = (_silu(conv) * val).astype(BF16)
    y = x_ref[0] + gate_ref[0] * jnp.dot(act_ref[...], wo_ref[...], preferred_element_type=F32)
    if final_norm:
        ms = jnp.mean(y * y, axis=-1, keepdims=True)
        y = y * lax.rsqrt(ms + EPS) * fg_ref[...]
    o_ref[0] = y


def _ffn(h, x, mod, w_in, conv_w, conv_b, w_out, final_g, *, row0, rows, is_ctx, final_norm, cast=()):
    nb, _, d = x.shape
    t = rows
    dff = w_out.shape[0]
    tm = min(FFN_TM, t)
    hb = 2 * SUBLANES
    bpt = tm // hb
    assert row0 % tm == 0 and rows % tm == 0
    t0 = row0 // tm
    b0 = row0 // hb
    nblk = t // hb
    mod_row = (lambda b: nb) if is_ctx else (lambda b: b)
    chunks = []
    c0 = 0
    while c0 < dff:
        cw = min(FFN_CHUNK, dff - c0)
        chunks.append((c0, cw))
        c0 += cw
    kern = functools.partial(_ffn_kernel, chunks=tuple(chunks), final_norm=final_norm, n_cast=len(cast))
    row = pl.BlockSpec((1, tm, d), lambda b, i: (b, t0 + i, 0))
    cast_in, cast_out = _cast_specs(cast, nb, t // tm)
    out = pl.pallas_call(
        kern,
        grid=(nb, t // tm),
        in_specs=[
            row,
            pl.BlockSpec((1, hb, d), lambda b, i: (b, b0 + jnp.maximum(i * bpt - 1, 0), 0)),
            pl.BlockSpec((1, hb, d), lambda b, i: (b, b0 + jnp.minimum((i + 1) * bpt, nblk - 1), 0)),
            row,
            pl.BlockSpec((1, 1, d), lambda b, i: (mod_row(b), 0, 5)),
            _resident(w_in.shape),
            _resident(conv_w.shape),
            _resident(conv_b.shape),
            _resident(w_out.shape),
            _resident((1, d)),
        ] + cast_in,
        out_specs=[pl.BlockSpec((1, tm, d), lambda b, i: (b, i, 0))] + cast_out,
        out_shape=[jax.ShapeDtypeStruct((nb, t, d), F32)]
        + [jax.ShapeDtypeStruct(a.shape[1:], BF16) for a, _ in cast],
        scratch_shapes=[pltpu.VMEM((tm + 2 * SUBLANES, d), BF16), pltpu.VMEM((tm, dff), BF16)],
        compiler_params=_cparams(("arbitrary", "arbitrary") if cast else ("parallel", "parallel")),
        name=("ffn_final" if final_norm else "ffn") + ("_ctx" if is_ctx else ""),
    )(h, h, h, x, mod, w_in, conv_w, conv_b, w_out, final_g, *[a for a, _ in cast])
    return out if cast else out[0]


def _na_qkv_kernel(c_ref, x_ref, sh_ref, sc_ref, g_ref, w_ref, q_ref, k_ref, v_ref, *, d, qscale):
    h = _norm_mod(_stream_tile(c_ref, x_ref), g_ref[...], sh_ref[0], sc_ref[0]).astype(BF16)
    proj = jnp.dot(h, w_ref[...], preferred_element_type=F32)
    q_ref[0] = (proj[:, :d] * qscale).astype(BF16)
    k_ref[0] = proj[:, d:2 * d].astype(BF16)
    v_ref[0] = proj[:, 2 * d:].astype(BF16)


def _na_qkv(ctx, x, mod, norm_g, w_qkv, head_dim):
    nb, n, d = x.shape
    t = ctx.shape[1] + n
    row = lambda w: pl.BlockSpec((1, TM, w), lambda b, i: (b, i, 0))
    return pl.pallas_call(
        functools.partial(_na_qkv_kernel, d=d, qscale=LOG2E * float(head_dim) ** -0.5),
        grid=(nb, t // TM),
        in_specs=_stream_specs(d) + [_mod_spec(d, 0, nb), _mod_spec(d, 1, nb),
                                     _resident((1, d)), _resident(w_qkv.shape)],
        out_specs=[row(d)] * 3,
        out_shape=[jax.ShapeDtypeStruct((nb, t, d), BF16)] * 3,
        compiler_params=_cparams(("parallel", "parallel")),
        name="na_qkv",
    )(ctx, x, mod, mod, norm_g, w_qkv)


def _na_attn_kernel(q_ref, k_ref, v_ref, bias_ref, o_ref, *, n_ctx, rows_n, head_dim):
    rb = pl.program_id(2)
    hw = q_ref.shape[-1]
    nwin = NA_KH * GRID_W
    nt = (((1,), (1,)), ((), ()))
    lane = lax.broadcasted_iota(jnp.int32, (GRID_W, hw), 1)
    hmask = [(lane // head_dim) == hd for hd in range(NA_HG)]

    def window(i):
        r = rb * NA_RB + i
        rs = jnp.clip(r - NA_KH // 2, 0, rows_n - NA_KH)
        lo = rs - r + (NA_KH - 1)
        return r, lo, pl.multiple_of(n_ctx + rs * GRID_W, GRID_W)

    def scores(i):
        r, lo, start = window(i)
        q = q_ref[0, pl.ds(pl.multiple_of(n_ctx + r * GRID_W, GRID_W), GRID_W), :]
        qs = jnp.concatenate([jnp.where(hmask[hd], q, jnp.zeros_like(q)) for hd in range(NA_HG)], axis=0)
        bias = jnp.concatenate(
            [jnp.concatenate([bias_ref[0, hd, lo + 2 * m] for m in range(NA_KH // 2)], axis=1)
             for hd in range(NA_HG)], axis=0)
        s_nb = lax.dot_general(qs, k_ref[0, pl.ds(start, nwin), :], nt, preferred_element_type=F32) + bias
        s_cx = lax.dot_general(qs, k_ref[0, 0:n_ctx, :], nt, preferred_element_type=F32)
        s = jnp.concatenate([s_nb, s_cx], axis=1)
        return s, jnp.max(s, axis=-1, keepdims=True)

    def finish(i, s, m):
        r, lo, start = window(i)
        p = jnp.exp2(s - m)
        l = jnp.sum(p, axis=-1, keepdims=True)
        p = p.astype(BF16)
        o = (jnp.dot(p[:, :nwin], v_ref[0, pl.ds(start, nwin), :], preferred_element_type=F32)
             + jnp.dot(p[:, nwin:], v_ref[0, 0:n_ctx, :], preferred_element_type=F32))
        o = o * (1.0 / l)
        out = o[0:GRID_W]
        for hd in range(1, NA_HG):
            out = jnp.where(hmask[hd], o[hd * GRID_W:(hd + 1) * GRID_W], out)
        o_ref[0, pl.ds(pl.multiple_of(i * GRID_W, GRID_W), GRID_W), :] = out.astype(BF16)

    def body(it, carry):
        nxt = scores(it * NA_U)
        for u in range(NA_U):
            cur = nxt
            if u + 1 < NA_U:
                nxt = scores(it * NA_U + u + 1)
            finish(it * NA_U + u, *cur)
        return carry

    lax.fori_loop(0, NA_RB // NA_U, body, 0)


def _na_bias(rpb):
    heads, ndr, ndc = rpb.shape
    c = np.arange(GRID_W)
    cs = np.clip(c - NA_KW // 2, 0, GRID_W - NA_KW)
    onehot = np.zeros((2 * ndc, GRID_W, 2 * GRID_W), np.float32)
    mask = np.full((GRID_W, 2 * GRID_W), NEG, np.float32)
    for half in range(2):
        for cq in range(GRID_W):
            for kc in range(cs[cq], cs[cq] + NA_KW):
                onehot[half * ndc + kc - cq + NA_KW - 1, cq, half * GRID_W + kc] = 1.0
                mask[cq, half * GRID_W + kc] = 0.0
    pair = jnp.concatenate([rpb[:, :ndr - 1, :], rpb[:, 1:, :]], axis=-1)
    tiles = jnp.einsum('hdk,kcl->hdcl', pair * LOG2E, onehot, precision=lax.Precision.HIGHEST) + mask
    return tiles.reshape(heads // NA_HG, NA_HG, ndr - 1, GRID_W, 2 * GRID_W)


def _na_attn(q, k, v, bias, n_ctx, head_dim):
    nb, t, d = q.shape
    rows_n = (t - n_ctx) // GRID_W
    nrb = rows_n // NA_RB
    nq = NA_RB * GRID_W
    hw = NA_HG * head_dim
    ngrp = d // hw
    assert n_ctx % nq == 0 or nq % n_ctx == 0
    kern = functools.partial(_na_attn_kernel, n_ctx=n_ctx, rows_n=rows_n, head_dim=head_dim)
    return pl.pallas_call(
        kern,
        grid=(nb, ngrp, nrb),
        in_specs=[
            pl.BlockSpec((1, t, hw), lambda b, g, r: (b, 0, g)),
            pl.BlockSpec((1, t, hw), lambda b, g, r: (b, 0, g)),
            pl.BlockSpec((1, t, hw), lambda b, g, r: (b, 0, g)),
            pl.BlockSpec((1,) + bias.shape[1:], lambda b, g, r: (g, 0, 0, 0, 0)),
        ],
        out_specs=pl.BlockSpec((1, nq, hw), lambda b, g, r: (b, r, g)),
        out_shape=jax.ShapeDtypeStruct((nb, t - n_ctx, d), BF16),
        compiler_params=_cparams(("parallel", "parallel", "arbitrary")),
        name="na_attn",
    )(q, k, v, bias)


def _na_out_kernel(a_ref, x_ref, gate_ref, sh2_ref, sc2_ref, w_ref, g2_ref, o_ref, h_ref):
    out = jnp.dot(a_ref[0], w_ref[...], preferred_element_type=F32)
    x_mid = x_ref[0] + gate_ref[0] * out
    o_ref[0] = x_mid
    h_ref[0] = _norm_mod(x_mid, g2_ref[...], sh2_ref[0], sc2_ref[0]).astype(BF16)


def _na_out(attn, x, mod, w_out, norm2_g):
    nb, n, d = x.shape
    tm = FFN_TM
    row = pl.BlockSpec((1, tm, d), lambda b, i: (b, i, 0))
    seg = lambda s: pl.BlockSpec((1, 1, d), lambda b, i: (b, 0, s))
    return pl.pallas_call(
        _na_out_kernel,
        grid=(nb, n // tm),
        in_specs=[row, row, seg(2), seg(3), seg(4), _resident(w_out.shape), _resident((1, d))],
        out_specs=[row, row],
        out_shape=[jax.ShapeDtypeStruct((nb, n, d), F32), jax.ShapeDtypeStruct((nb, n, d), BF16)],
        compiler_params=_cparams(("parallel", "parallel")),
        name="na_out",
    )(attn, x, mod, mod, mod, w_out, norm2_g)


def _rope_tables(n_lat, n_ctx):
    half = LANES // 2
    inv = 1.0 / (ROPE_BASE ** (np.arange(0, half, 2, dtype=np.float64) / half))
    pos = np.arange(n_lat)
    rows = (pos // GRID_W).astype(np.float64)[:, None] * inv[None, :]
    cols = (pos % GRID_W).astype(np.float64)[:, None] * inv[None, :]
    cos = np.concatenate([np.cos(rows)] * 2 + [np.cos(cols)] * 2, axis=1)
    sin = np.concatenate([-np.sin(rows), np.sin(rows), -np.sin(cols), np.sin(cols)], axis=1)
    cos = np.concatenate([np.ones((n_ctx, LANES)), cos], axis=0).astype(np.float32)
    sin = np.concatenate([np.zeros((n_ctx, LANES)), sin], axis=0).astype(np.float32)
    return jnp.asarray(cos), jnp.asarray(sin)


def kernel(x, c, ctx, c_ctx, ada_w, ada_b, norm1_g, norm2_g, ffn_w_in, ffn_conv_w, ffn_conv_b, ffn_w_out,
           gla_w_in, gla_a_w1, gla_a_w2, gla_a_b, gla_norm_g, gla_w_out, na_w_qkv, na_rpb, na_w_out, final_g):
    nb, n, d = x.shape
    n_ctx = ctx.shape[1]
    depth = ada_w.shape[0]
    dff = ffn_conv_b.shape[-1]
    assert n_ctx == TM and n % TM == 0 and depth == 2 and nb + 1 <= SUBLANES

    mod, w_gla_in, w_gla_out = _ada(jnp.concatenate([c, c_ctx[None, :]], axis=0), ada_w, ada_b,
                                    cast=((gla_w_in, 0), (gla_w_out, 0)))

    m0 = mod[0].reshape(SUBLANES, 1, 6 * d)
    dk = gla_a_w2.shape[-1]
    dv = gla_w_out.shape[1]
    assert 4 * 2 * GLA_LOW_RANK == LANES
    aw1 = jnp.concatenate([gla_a_w1[0, 0], gla_a_w1[0, 1]] * 4, axis=1).astype(BF16)
    zeros = jnp.zeros((GLA_LOW_RANK, dk), F32)
    w2 = jnp.concatenate([jnp.concatenate([gla_a_w2[0, 0], zeros], axis=1),
                          jnp.concatenate([zeros, gla_a_w2[0, 1]], axis=1)], axis=0)
    w2_hi = w2.astype(BF16)
    w2_lo = (w2 - w2_hi.astype(F32)).astype(BF16)
    aw2 = jnp.concatenate([w2_hi, w2_hi, w2_lo, w2_lo], axis=0)
    cos_t, sin_t = _rope_tables(n, n_ctx)
    q, k, v, r, gf, gb, ffn_wi0, ffn_wo0 = _gla_in(
        ctx, x, m0, norm1_g[0].reshape(1, d), w_gla_in, aw1, aw2,
        gla_a_b[0].reshape(1, 2 * dk), cos_t, sin_t, dk, dv, cast=((ffn_w_in, 0), (ffn_w_out, 0)))
    o_f, o_b = _gla_core(q, k, v, gf, gb, GLA_HEADS, n_ctx)
    x_mid, h_mid = _gla_out(o_f, o_b, r, ctx, x, m0, gla_norm_g[0].reshape(1, dv // GLA_HEADS),
                            w_gla_out, norm2_g[0].reshape(1, d), GLA_HEADS)
    ffn0 = functools.partial(_ffn, h_mid, x_mid, m0, ffn_wi0, ffn_conv_w[0], ffn_conv_b[0].reshape(1, dff),
                             ffn_wo0, final_g.reshape(1, d), final_norm=False)
    x_lat, ffn_wi1, ffn_wo1, w_qkv, w_na_out = ffn0(
        row0=0, rows=n, is_ctx=False, cast=((ffn_w_in, 1), (ffn_w_out, 1), (na_w_qkv, 0), (na_w_out, 0)))
    x_ctx = ffn0(row0=n, rows=n_ctx, is_ctx=True)

    m1 = mod[1].reshape(SUBLANES, 1, 6 * d)
    head_dim = d // NA_HEADS
    qn, kn, vn = _na_qkv(x_ctx, x_lat, m1, norm1_g[1].reshape(1, d), w_qkv, head_dim)
    bias = _na_bias(na_rpb[0])
    attn = _na_attn(qn, kn, vn, bias, n_ctx, head_dim)
    x_lat, h_lat = _na_out(attn, x_lat, m1, w_na_out, norm2_g[1].reshape(1, d))
    return _ffn(h_lat, x_lat, m1, ffn_wi1, ffn_conv_w[1], ffn_conv_b[1].reshape(1, dff), ffn_wo1,
                final_g.reshape(1, d), row0=0, rows=n, is_ctx=False, final_norm=True)
```

```python
import functools

import numpy as np
import jax
import jax.numpy as jnp
from jax import lax
from jax.experimental import pallas as pl
from jax.experimental.pallas import tpu as pltpu

GRID_W = 64
GLA_HEADS = 4
GLA_GATE_NORM = 16.0
GLA_LOW_RANK = 16
ROPE_BASE = 10000.0
NA_HEADS = 16
NA_KH = 8
NA_KW = 16
EPS = 1e-6
LOG2E = 1.4426950408889634

LANES = 128
SUBLANES = 8
VMEM_LIMIT = 56 * 1024 * 1024

TM = 256
ADA_TK = 256
GLA_C = 128
GLA_BLK = 256
GLA_DIAG = 16
FFN_TM = 1024
FFN_CHUNK = 512
NA_RB = 64
NA_U = 32
NA_HG = 4
NEG = -1e30

F32 = jnp.float32
BF16 = jnp.bfloat16


def _cparams(sem):
    return pltpu.CompilerParams(dimension_semantics=sem, vmem_limit_bytes=VMEM_LIMIT)


def _resident(shape):
    nd = len(shape)
    return pl.BlockSpec(shape, lambda *_: (0,) * nd, pipeline_mode=pl.Buffered(1))


def _silu(x):
    return x * (1.0 / (1.0 + jnp.exp(-x)))


def _norm_mod(x, g, shift, scale):
    ms = jnp.mean(x * x, axis=-1, keepdims=True)
    return x * lax.rsqrt(ms + EPS) * (g * (1.0 + scale)) + shift


def _ada_kernel(c_ref, w_ref, b_ref, *rest, n_rows, n_cast):
    o_ref = rest[n_cast]
    _cast_blocks(rest[:n_cast], rest[n_cast + 1:])
    j = pl.program_id(1)
    tk, d6 = w_ref.shape[1:]
    k0 = pl.multiple_of(j * tk, tk)
    rows = []
    for r in range(n_rows):
        sc = _silu(c_ref[r, pl.ds(k0, tk), :])
        parts = [jnp.sum(w_ref[0, :, n * LANES:(n + 1) * LANES] * sc, axis=0, keepdims=True)
                 for n in range(d6 // LANES)]
        rows.append(jnp.concatenate(parts, axis=1))
    rows.append(jnp.zeros((SUBLANES - n_rows, d6), F32))
    part = jnp.concatenate(rows, axis=0)

    @pl.when(j == 0)
    def _():
        o_ref[0] = part + b_ref[0]

    @pl.when(j > 0)
    def _():
        o_ref[0] += part


def _ada(c_rows, ada_w, ada_b, cast):
    depth, d, d6 = ada_w.shape
    n_rows = c_rows.shape[0]
    tk = ADA_TK
    c_b = jnp.broadcast_to(c_rows[:, :, None], (n_rows, d, LANES))
    cast_in, cast_out = _cast_specs(cast, depth, d // tk)
    return pl.pallas_call(
        functools.partial(_ada_kernel, n_rows=n_rows, n_cast=len(cast)),
        grid=(depth, d // tk),
        in_specs=[
            pl.BlockSpec((n_rows, d, LANES), lambda i, j: (0, 0, 0)),
            pl.BlockSpec((1, tk, d6), lambda i, j: (i, j, 0)),
            pl.BlockSpec((1, 1, d6), lambda i, j: (i, 0, 0)),
        ] + cast_in,
        out_specs=[pl.BlockSpec((1, SUBLANES, d6), lambda i, j: (i, 0, 0))] + cast_out,
        out_shape=[jax.ShapeDtypeStruct((depth, SUBLANES, d6), F32)]
        + [jax.ShapeDtypeStruct(a.shape[1:], BF16) for a, _ in cast],
        compiler_params=_cparams(("arbitrary", "arbitrary")),
        name="ada_mod",
    )(c_b, ada_w, ada_b.reshape(depth, 1, d6), *[a for a, _ in cast])


def _mod_spec(d, seg, nb):
    return pl.BlockSpec((1, 1, d), lambda b, i: (jnp.where(i == 0, nb, b), 0, seg))


def _stream_specs(d):
    return [pl.BlockSpec((1, TM, d), lambda b, i: (b, 0, 0)),
            pl.BlockSpec((1, TM, d), lambda b, i: (b, jnp.maximum(i - 1, 0), 0))]


def _stream_tile(ctx_ref, x_ref):
    return jnp.where(pl.program_id(1) == 0, ctx_ref[0], x_ref[0])


def _cast_specs(arrays, nb, steps):
    in_specs, out_specs = [], []
    for a, layer in arrays:
        _, rows, cols = a.shape
        tiles = rows // (2 * SUBLANES)
        nblk = max(k for k in range(1, nb * steps + 1) if tiles % k == 0)
        in_specs.append(pl.BlockSpec(
            (1, rows // nblk, cols),
            lambda b, i, nblk=nblk, layer=layer: (layer, jnp.minimum(b * steps + i, nblk - 1), 0)))
        out_specs.append(pl.BlockSpec(
            (rows // nblk, cols), lambda b, i, nblk=nblk: (jnp.minimum(b * steps + i, nblk - 1), 0)))
    return in_specs, out_specs


def _cast_blocks(src_refs, dst_refs):
    for src, dst in zip(src_refs, dst_refs):
        dst[...] = src[0].astype(BF16)


def _gla_in_kernel(c_ref, x_ref, sh_ref, sc_ref, g_ref, w_ref, aw1_ref, aw2_ref, ab_ref,
                   cos_ref, sin_ref, *rest, dk, dv, n_cast):
    q_ref, k_ref, v_ref, r_ref, gf_ref, gb_ref = rest[n_cast:n_cast + 6]
    _cast_blocks(rest[:n_cast], rest[n_cast + 6:])
    h = _norm_mod(_stream_tile(c_ref, x_ref), g_ref[...], sh_ref[0], sc_ref[0]).astype(BF16)

    z = jnp.dot(h, aw1_ref[...], preferred_element_type=F32)
    z_hi = z.astype(BF16)
    z_lo = (z - z_hi.astype(F32)).astype(BF16)
    grp = lax.broadcasted_iota(jnp.int32, z.shape, 1) // (2 * GLA_LOW_RANK)
    zc = jnp.where((grp & 1) == 0, z_hi, z_lo)
    pre = jnp.dot(zc, aw2_ref[...], preferred_element_type=F32) + ab_ref[...]

    qk = jnp.dot(h, w_ref[:, :2 * dk], preferred_element_type=F32)
    g = (jnp.minimum(pre, 0.0) - jnp.log(1.0 + jnp.exp(-jnp.abs(pre)))) * (LOG2E / GLA_GATE_NORM)
    g_hi = g.astype(BF16)
    g_lo = (g - g_hi.astype(F32)).astype(BF16)
    v_ref[0] = jnp.dot(h, w_ref[:, 2 * dk: 2 * dk + dv], preferred_element_type=F32).astype(BF16)

    cos = cos_ref[...]
    sin = sin_ref[...]
    lane = lax.broadcasted_iota(jnp.int32, cos.shape, 1)
    low = (lane & 32) == 0
    qscale = float(LANES) ** -0.5

    def rope(t):
        partner = jnp.where(low, pltpu.roll(t, LANES - 32, 1), pltpu.roll(t, 32, 1))
        return t * cos + partner * sin

    for hd in range(dk // LANES):
        sl = slice(hd * LANES, (hd + 1) * LANES)
        q_ref[0, :, sl] = rope(qk[:, sl]) * qscale
        k_ref[0, :, sl] = rope(qk[:, dk + hd * LANES: dk + (hd + 1) * LANES])

    ti = lax.broadcasted_iota(jnp.int32, (GLA_C, GLA_C), 0)
    tj = lax.broadcasted_iota(jnp.int32, (GLA_C, GLA_C), 1)
    for o_ref, tri, sl in ((gf_ref, tj <= ti, slice(0, dk)), (gb_ref, tj >= ti, slice(dk, 2 * dk))):
        tri = jnp.where(tri, 1.0, 0.0).astype(BF16)
        tri2 = jnp.concatenate([tri, tri], axis=1)
        for ch in range(TM // GLA_C):
            rows = slice(ch * GLA_C, (ch + 1) * GLA_C)
            o_ref[0, rows, :] = jnp.dot(tri2, jnp.concatenate([g_hi[rows, sl], g_lo[rows, sl]], axis=0),
                                        preferred_element_type=F32)
    r_ref[0] = jnp.dot(h, w_ref[:, 2 * dk + dv:], preferred_element_type=F32).astype(BF16)


def _gla_in(ctx, x, mod, norm_g, w_in, aw1, aw2, ab, cos_t, sin_t, dk, dv, cast):
    nb, n, d = x.shape
    t = ctx.shape[1] + n
    nt = t // TM
    kern = functools.partial(_gla_in_kernel, dk=dk, dv=dv, n_cast=len(cast))
    row = lambda w: pl.BlockSpec((1, TM, w), lambda b, i: (b, i, 0))
    cast_in, cast_out = _cast_specs(cast, nb, nt)
    return pl.pallas_call(
        kern,
        grid=(nb, nt),
        in_specs=_stream_specs(d) + [
            _mod_spec(d, 0, nb),
            _mod_spec(d, 1, nb),
            _resident((1, d)),
            _resident(w_in.shape),
            _resident(aw1.shape),
            _resident(aw2.shape),
            _resident(ab.shape),
            pl.BlockSpec((TM, LANES), lambda b, i: (i, 0)),
            pl.BlockSpec((TM, LANES), lambda b, i: (i, 0)),
        ] + cast_in,
        out_specs=[row(dk), row(dk), row(dv), row(dv), row(dk), row(dk)] + cast_out,
        out_shape=[
            jax.ShapeDtypeStruct((nb, t, dk), F32),
            jax.ShapeDtypeStruct((nb, t, dk), F32),
            jax.ShapeDtypeStruct((nb, t, dv), BF16),
            jax.ShapeDtypeStruct((nb, t, dv), BF16),
            jax.ShapeDtypeStruct((nb, t, dk), F32),
            jax.ShapeDtypeStruct((nb, t, dk), F32),
        ] + [jax.ShapeDtypeStruct(a.shape[1:], BF16) for a, _ in cast],
        compiler_params=_cparams(("arbitrary", "arbitrary")),
        name="gla_in",
    )(ctx, x, mod, mod, norm_g, w_in, aw1, aw2, ab, cos_t, sin_t, *[a for a, _ in cast])


def _row_bcast(b, period, offset):
    c, w = b.shape
    if period == c:
        return jnp.broadcast_to(b[offset:offset + 1, :], (c, w))
    b3 = b.reshape(c // period, period, w)
    return jnp.broadcast_to(b3[:, offset:offset + 1, :], b3.shape).reshape(c, w)


def _pair_scores(lhs, rhs, first):
    zero = jnp.zeros_like(rhs)
    blockdiag = jnp.concatenate([jnp.where(first, rhs, zero), jnp.where(first, zero, rhs)], axis=0)
    return lax.dot_general(lhs, blockdiag, (((1,), (1,)), ((), ())), preferred_element_type=F32)


def _gla_pair(q, k, b, rev, kside, same, causal, first):
    c = q.shape[0]
    b_last = b[0:1, :] if rev else b[c - 1:c, :]
    q_in = (q * jnp.exp2(b)).astype(BF16)
    k_out = (k * jnp.exp2(b_last - b)).astype(BF16)
    b_ref = _row_bcast(b, GLA_DIAG, GLA_DIAG - 1 if rev else 0)
    qd = (q * jnp.exp2(b - b_ref)).astype(BF16)
    kd = (k * jnp.exp2(b_ref - b)).astype(BF16)
    att = _pair_scores(qd, kd, first)
    period = 2 * GLA_DIAG
    while period <= c:
        half = period // 2
        b_ref = _row_bcast(b, period, half if rev else half - 1)
        u = (jnp.where(kside[period], k, q) * jnp.exp2(-jnp.abs(b - b_ref))).astype(BF16)
        att = jnp.where(same[half], att, _pair_scores(u, u, first))
        period *= 2
    att = jnp.where(causal, att, 0.0).astype(BF16)
    return q_in, k_out, att, jnp.exp2(b_last)


def _gla_state_step(q_in, att, k_out, v, decay_row, s):
    lhs = jnp.concatenate([q_in, att], axis=1)
    rhs = jnp.concatenate([s.astype(BF16), v], axis=0)
    o = jnp.dot(lhs, rhs, preferred_element_type=F32)
    decay = jnp.transpose(jnp.broadcast_to(decay_row, (LANES, LANES)))
    decay = jnp.concatenate([decay] * (s.shape[1] // LANES), axis=1)
    kv = lax.dot_general(k_out, v, (((0,), (0,)), ((), ())), preferred_element_type=F32)
    return o, s * decay + kv


def _gla_core_kernel(qf_ref, kf_ref, vf_ref, gf_ref, qb_ref, kb_ref, vb_ref, gb_ref,
                     of_ref, ob_ref, s_ref, *, heads, dvh):
    @pl.when(pl.program_id(1) == 0)
    def _():
        s_ref[...] = jnp.zeros_like(s_ref)

    c = GLA_C
    assert c == LANES and heads % 2 == 0
    row = lax.broadcasted_iota(jnp.int32, (c, 2 * LANES), 0)
    first = lax.broadcasted_iota(jnp.int32, (c, 2 * LANES), 1) < LANES
    ti = lax.broadcasted_iota(jnp.int32, (c, 2 * c), 0)
    tj = lax.broadcasted_iota(jnp.int32, (c, 2 * c), 1) & (c - 1)
    x = ti ^ tj
    same = {}
    half = GLA_DIAG
    while half < c:
        same[half] = x < half
        half *= 2

    n_chunks = qf_ref.shape[1] // c
    dirs = ((qf_ref, kf_ref, vf_ref, gf_ref, of_ref), (qb_ref, kb_ref, vb_ref, gb_ref, ob_ref))
    masks = []
    for d in range(2):
        rev = d == 1
        kside = {}
        period = 2 * GLA_DIAG
        while period <= c:
            off = row & (period - 1)
            kside[period] = (off >= period // 2) if rev else (off < period // 2)
            period *= 2
        masks.append((kside, (tj >= ti) if rev else (tj <= ti)))

    pre = {}
    for step in range(n_chunks):
        for d, (q_ref, k_ref, v_ref, b_ref, o_ref) in enumerate(dirs):
            rev = d == 1
            ch = n_chunks - 1 - step if rev else step
            rows = slice(ch * c, (ch + 1) * c)
            for pr in range(heads // 2):
                psl = slice(2 * pr * LANES, (2 * pr + 2) * LANES)
                pre[step, d, pr] = _gla_pair(q_ref[0, rows, psl], k_ref[0, rows, psl], b_ref[0, rows, psl],
                                             rev, masks[d][0], same, masks[d][1], first)
    for step in range(n_chunks):
        for d, (q_ref, k_ref, v_ref, b_ref, o_ref) in enumerate(dirs):
            ch = n_chunks - 1 - step if d == 1 else step
            rows = slice(ch * c, (ch + 1) * c)
            for pr in range(heads // 2):
                q_in, k_out, att, decay = pre[step, d, pr]
                for j in range(2):
                    hd = 2 * pr + j
                    hsl = slice(j * LANES, (j + 1) * LANES)
                    vsl = slice(hd * dvh, (hd + 1) * dvh)
                    o, s_new = _gla_state_step(q_in[:, hsl], att[:, j * c:(j + 1) * c], k_out[:, hsl],
                                               v_ref[0, rows, vsl], decay[:, hsl], s_ref[d, hd])
                    o_ref[0, rows, vsl] = o.astype(o_ref.dtype)
                    s_ref[d, hd] = s_new


def _gla_core(q, k, v, gf, gb, heads, n_ctx):
    nb, t, dk = q.shape
    dv = v.shape[-1]
    assert n_ctx % GLA_BLK == 0 and t % GLA_BLK == 0
    nc = t // GLA_BLK
    ncx = n_ctx // GLA_BLK
    fwd = lambda b, s: (b, s, 0)
    bwd = lambda b, s: (b, jnp.where(s < ncx, ncx - 1 - s, nc - 1 + ncx - s), 0)
    kern = functools.partial(_gla_core_kernel, heads=heads, dvh=dv // heads)
    blk = lambda w, m: pl.BlockSpec((1, GLA_BLK, w), m)
    return pl.pallas_call(
        kern,
        grid=(nb, nc),
        in_specs=[blk(dk, fwd), blk(dk, fwd), blk(dv, fwd), blk(dk, fwd),
                  blk(dk, bwd), blk(dk, bwd), blk(dv, bwd), blk(dk, bwd)],
        out_specs=[blk(dv, fwd), blk(dv, bwd)],
        out_shape=[jax.ShapeDtypeStruct((nb, t, dv), BF16)] * 2,
        scratch_shapes=[pltpu.VMEM((2, heads, dk // heads, dv // heads), F32)],
        compiler_params=_cparams(("parallel", "arbitrary")),
        name="gla_core",
    )(q, k, v, gf, q, k, v, gb)


def _gla_out_kernel(of_ref, ob_ref, r_ref, c_ref, x_ref, gate_ref, sh2_ref, sc2_ref, ng_ref, w_ref, g2_ref,
                    xo_ref, ho_ref, *, heads):
    o = of_ref[0].astype(F32) + ob_ref[0].astype(F32)
    r = r_ref[0].astype(F32)
    dvh = o.shape[1] // heads
    parts = []
    for hd in range(heads):
        oh = o[:, hd * dvh:(hd + 1) * dvh]
        ms = jnp.mean(oh * oh, axis=-1, keepdims=True)
        parts.append(oh * lax.rsqrt(ms + EPS) * ng_ref[...])
    y = jnp.concatenate(parts, axis=1) * _silu(r)
    out = jnp.dot(y.astype(BF16), w_ref[...], preferred_element_type=F32)
    x_mid = _stream_tile(c_ref, x_ref) + gate_ref[0] * out
    xo_ref[0] = x_mid
    ho_ref[0] = _norm_mod(x_mid, g2_ref[...], sh2_ref[0], sc2_ref[0]).astype(BF16)


def _gla_out(o_f, o_b, r, ctx, x, mod, norm_g, w_out, norm2_g, heads):
    nb, t, dv = o_f.shape
    d = x.shape[-1]
    nt = t // TM
    row = lambda w: pl.BlockSpec((1, TM, w), lambda b, i: (b, i, 0))
    out = pl.BlockSpec((1, TM, d), lambda b, i: (b, jnp.where(i == 0, nt - 1, i - 1), 0))
    return pl.pallas_call(
        functools.partial(_gla_out_kernel, heads=heads),
        grid=(nb, nt),
        in_specs=[row(dv), row(dv), row(dv)] + _stream_specs(d) + [
            _mod_spec(d, 2, nb), _mod_spec(d, 3, nb), _mod_spec(d, 4, nb),
            _resident(norm_g.shape), _resident(w_out.shape), _resident((1, d))],
        out_specs=[out, out],
        out_shape=[jax.ShapeDtypeStruct((nb, t, d), F32), jax.ShapeDtypeStruct((nb, t, d), BF16)],
        compiler_params=_cparams(("parallel", "parallel")),
        name="gla_out",
    )(o_f, o_b, r, ctx, x, mod, mod, mod, norm_g, w_out, norm2_g)


def _ffn_kernel(h_ref, hp_ref, hn_ref, x_ref, gate_ref, wi_ref, cw_ref, cb_ref, wo_ref, fg_ref,
                *rest, chunks, final_norm, n_cast):
    o_ref = rest[n_cast]
    hbuf_ref, act_ref = rest[-2:]
    _cast_blocks(rest[:n_cast], rest[n_cast + 1:-2])
    i = pl.program_id(1)
    tm = h_ref.shape[1]
    dff = act_ref.shape[1]
    hs = SUBLANES
    has_prev = i > 0
    has_next = i < pl.num_programs(1) - 1
    hbuf_ref[0:tm, :] = h_ref[0]
    nxt = jnp.where(has_next, hn_ref[0, 0:hs, :].astype(F32), 0.0)
    prv = jnp.where(has_prev, hp_ref[0, hs:2 * hs, :].astype(F32), 0.0)
    hbuf_ref[tm:, :] = jnp.concatenate([nxt, prv], axis=0).astype(BF16)
    n = tm + 2 * hs

    for (c0, cw) in chunks:
        a = jnp.dot(hbuf_ref[...], wi_ref[:, c0:c0 + cw], preferred_element_type=F32)
        val = jnp.dot(hbuf_ref[0:tm, :], wi_ref[:, dff + c0:dff + c0 + cw], preferred_element_type=F32)
        w3 = cw_ref[:, c0:c0 + cw]
        conv = (pltpu.roll(a, 1, 0)[0:tm] * w3[0:1] + a[0:tm] * w3[1:2]
                + pltpu.roll(a, n - 1, 0)[0:tm] * w3[2:3] + cb_ref[:, c0:c0 + cw])
        act_ref[:, c0:c0 + cw] = (_silu(conv) * val).astype(BF16)
    y = x_ref[0] + gate_ref[0] * jnp.dot(act_ref[...], wo_ref[...], preferred_element_type=F32)
    if final_norm:
        ms = jnp.mean(y * y, axis=-1, keepdims=True)
        y = y * lax.rsqrt(ms + EPS) * fg_ref[...]
    o_ref[0] = y


def _ffn(h, x, mod, w_in, conv_w, conv_b, w_out, final_g, *, row0, rows, is_ctx, final_norm, cast=()):
    nb, _, d = x.shape
    t = rows
    dff = w_out.shape[0]
    tm = min(FFN_TM, t)
    hb = 2 * SUBLANES
    bpt = tm // hb
    assert row0 % tm == 0 and rows % tm == 0
    t0 = row0 // tm
    b0 = row0 // hb
    nblk = t // hb
    mod_row = (lambda b: nb) if is_ctx else (lambda b: b)
    chunks = []
    c0 = 0
    while c0 < dff:
        cw = min(FFN_CHUNK, dff - c0)
        chunks.append((c0, cw))
        c0 += cw
    kern = functools.partial(_ffn_kernel, chunks=tuple(chunks), final_norm=final_norm, n_cast=len(cast))
    row = pl.BlockSpec((1, tm, d), lambda b, i: (b, t0 + i, 0))
    cast_in, cast_out = _cast_specs(cast, nb, t // tm)
    out = pl.pallas_call(
        kern,
        grid=(nb, t // tm),
        in_specs=[
            row,
            pl.BlockSpec((1, hb, d), lambda b, i: (b, b0 + jnp.maximum(i * bpt - 1, 0), 0)),
            pl.BlockSpec((1, hb, d), lambda b, i: (b, b0 + jnp.minimum((i + 1) * bpt, nblk - 1), 0)),
            row,
            pl.BlockSpec((1, 1, d), lambda b, i: (mod_row(b), 0, 5)),
            _resident(w_in.shape),
            _resident(conv_w.shape),
            _resident(conv_b.shape),
            _resident(w_out.shape),
            _resident((1, d)),
        ] + cast_in,
        out_specs=[pl.BlockSpec((1, tm, d), lambda b, i: (b, i, 0))] + cast_out,
        out_shape=[jax.ShapeDtypeStruct((nb, t, d), F32)]
        + [jax.ShapeDtypeStruct(a.shape[1:], BF16) for a, _ in cast],
        scratch_shapes=[pltpu.VMEM((tm + 2 * SUBLANES, d), BF16), pltpu.VMEM((tm, dff), BF16)],
        compiler_params=_cparams(("arbitrary", "arbitrary") if cast else ("parallel", "parallel")),
        name=("ffn_final" if final_norm else "ffn") + ("_ctx" if is_ctx else ""),
    )(h, h, h, x, mod, w_in, conv_w, conv_b, w_out, final_g, *[a for a, _ in cast])
    return out if cast else out[0]


def _na_qkv_kernel(c_ref, x_ref, sh_ref, sc_ref, g_ref, w_ref, q_ref, k_ref, v_ref, *, d, qscale):
    h = _norm_mod(_stream_tile(c_ref, x_ref), g_ref[...], sh_ref[0], sc_ref[0]).astype(BF16)
    proj = jnp.dot(h, w_ref[...], preferred_element_type=F32)
    q_ref[0] = (proj[:, :d] * qscale).astype(BF16)
    k_ref[0] = proj[:, d:2 * d].astype(BF16)
    v_ref[0] = proj[:, 2 * d:].astype(BF16)


def _na_qkv(ctx, x, mod, norm_g, w_qkv, head_dim):
    nb, n, d = x.shape
    t = ctx.shape[1] + n
    row = lambda w: pl.BlockSpec((1, TM, w), lambda b, i: (b, i, 0))
    return pl.pallas_call(
        functools.partial(_na_qkv_kernel, d=d, qscale=LOG2E * float(head_dim) ** -0.5),
        grid=(nb, t // TM),
        in_specs=_stream_specs(d) + [_mod_spec(d, 0, nb), _mod_spec(d, 1, nb),
                                     _resident((1, d)), _resident(w_qkv.shape)],
        out_specs=[row(d)] * 3,
        out_shape=[jax.ShapeDtypeStruct((nb, t, d), BF16)] * 3,
        compiler_params=_cparams(("parallel", "parallel")),
        name="na_qkv",
    )(ctx, x, mod, mod, norm_g, w_qkv)


def _na_attn_kernel(q_ref, k_ref, v_ref, bias_ref, o_ref, *, n_ctx, rows_n, head_dim):
    rb = pl.program_id(2)
    hw = q_ref.shape[-1]
    nwin = NA_KH * GRID_W
    nt = (((1,), (1,)), ((), ()))
    lane = lax.broadcasted_iota(jnp.int32, (GRID_W, hw), 1)
    hmask = [(lane // head_dim) == hd for hd in range(NA_HG)]

    def window(i):
        r = rb * NA_RB + i
        rs = jnp.clip(r - NA_KH // 2, 0, rows_n - NA_KH)
        lo = rs - r + (NA_KH - 1)
        return r, lo, pl.multiple_of(n_ctx + rs * GRID_W, GRID_W)

    def scores(i):
        r, lo, start = window(i)
        q = q_ref[0, pl.ds(pl.multiple_of(n_ctx + r * GRID_W, GRID_W), GRID_W), :]
        qs = jnp.concatenate([jnp.where(hmask[hd], q, jnp.zeros_like(q)) for hd in range(NA_HG)], axis=0)
        bias = jnp.concatenate(
            [jnp.concatenate([bias_ref[0, hd, lo + 2 * m] for m in range(NA_KH // 2)], axis=1)
             for hd in range(NA_HG)], axis=0)
        s_nb = lax.dot_general(qs, k_ref[0, pl.ds(start, nwin), :], nt, preferred_element_type=F32) + bias
        s_cx = lax.dot_general(qs, k_ref[0, 0:n_ctx, :], nt, preferred_element_type=F32)
        s = jnp.concatenate([s_nb, s_cx], axis=1)
        return s, jnp.max(s, axis=-1, keepdims=True)

    def finish(i, s, m):
        r, lo, start = window(i)
        p = jnp.exp2(s - m)
        l = jnp.sum(p, axis=-1, keepdims=True)
        p = p.astype(BF16)
        o = (jnp.dot(p[:, :nwin], v_ref[0, pl.ds(start, nwin), :], preferred_element_type=F32)
             + jnp.dot(p[:, nwin:], v_ref[0, 0:n_ctx, :], preferred_element_type=F32))
        o = o * (1.0 / l)
        out = o[0:GRID_W]
        for hd in range(1, NA_HG):
            out = jnp.where(hmask[hd], o[hd * GRID_W:(hd + 1) * GRID_W], out)
        o_ref[0, pl.ds(pl.multiple_of(i * GRID_W, GRID_W), GRID_W), :] = out.astype(BF16)

    def body(it, carry):
        nxt = scores(it * NA_U)
        for u in range(NA_U):
            cur = nxt
            if u + 1 < NA_U:
                nxt = scores(it * NA_U + u + 1)
            finish(it * NA_U + u, *cur)
        return carry

    lax.fori_loop(0, NA_RB // NA_U, body, 0)


def _na_bias(rpb):
    heads, ndr, ndc = rpb.shape
    c = np.arange(GRID_W)
    cs = np.clip(c - NA_KW // 2, 0, GRID_W - NA_KW)
    onehot = np.zeros((2 * ndc, GRID_W, 2 * GRID_W), np.float32)
    mask = np.full((GRID_W, 2 * GRID_W), NEG, np.float32)
    for half in range(2):
        for cq in range(GRID_W):
            for kc in range(cs[cq], cs[cq] + NA_KW):
                onehot[half * ndc + kc - cq + NA_KW - 1, cq, half * GRID_W + kc] = 1.0
                mask[cq, half * GRID_W + kc] = 0.0
    pair = jnp.concatenate([rpb[:, :ndr - 1, :], rpb[:, 1:, :]], axis=-1)
    tiles = jnp.einsum('hdk,kcl->hdcl', pair * LOG2E, onehot, precision=lax.Precision.HIGHEST) + mask
    return tiles.reshape(heads // NA_HG, NA_HG, ndr - 1, GRID_W, 2 * GRID_W)


def _na_attn(q, k, v, bias, n_ctx, head_dim):
    nb, t, d = q.shape
    rows_n = (t - n_ctx) // GRID_W
    nrb = rows_n // NA_RB
    nq = NA_RB * GRID_W
    hw = NA_HG * head_dim
    ngrp = d // hw
    assert n_ctx % nq == 0 or nq % n_ctx == 0
    kern = functools.partial(_na_attn_kernel, n_ctx=n_ctx, rows_n=rows_n, head_dim=head_dim)
    return pl.pallas_call(
        kern,
        grid=(nb, ngrp, nrb),
        in_specs=[
            pl.BlockSpec((1, t, hw), lambda b, g, r: (b, 0, g)),
            pl.BlockSpec((1, t, hw), lambda b, g, r: (b, 0, g)),
            pl.BlockSpec((1, t, hw), lambda b, g, r: (b, 0, g)),
            pl.BlockSpec((1,) + bias.shape[1:], lambda b, g, r: (g, 0, 0, 0, 0)),
        ],
        out_specs=pl.BlockSpec((1, nq, hw), lambda b, g, r: (b, r, g)),
        out_shape=jax.ShapeDtypeStruct((nb, t - n_ctx, d), BF16),
        compiler_params=_cparams(("parallel", "parallel", "arbitrary")),
        name="na_attn",
    )(q, k, v, bias)


def _na_out_kernel(a_ref, x_ref, gate_ref, sh2_ref, sc2_ref, w_ref, g2_ref, o_ref, h_ref):
    out = jnp.dot(a_ref[0], w_ref[...], preferred_element_type=F32)
    x_mid = x_ref[0] + gate_ref[0] * out
    o_ref[0] = x_mid
    h_ref[0] = _norm_mod(x_mid, g2_ref[...], sh2_ref[0], sc2_ref[0]).astype(BF16)


def _na_out(attn, x, mod, w_out, norm2_g):
    nb, n, d = x.shape
    tm = FFN_TM
    row = pl.BlockSpec((1, tm, d), lambda b, i: (b, i, 0))
    seg = lambda s: pl.BlockSpec((1, 1, d), lambda b, i: (b, 0, s))
    return pl.pallas_call(
        _na_out_kernel,
        grid=(nb, n // tm),
        in_specs=[row, row, seg(2), seg(3), seg(4), _resident(w_out.shape), _resident((1, d))],
        out_specs=[row, row],
        out_shape=[jax.ShapeDtypeStruct((nb, n, d), F32), jax.ShapeDtypeStruct((nb, n, d), BF16)],
        compiler_params=_cparams(("parallel", "parallel")),
        name="na_out",
    )(attn, x, mod, mod, mod, w_out, norm2_g)


def _rope_tables(n_lat, n_ctx):
    half = LANES // 2
    inv = 1.0 / (ROPE_BASE ** (np.arange(0, half, 2, dtype=np.float64) / half))
    pos = np.arange(n_lat)
    rows = (pos // GRID_W).astype(np.float64)[:, None] * inv[None, :]
    cols = (pos % GRID_W).astype(np.float64)[:, None] * inv[None, :]
    cos = np.concatenate([np.cos(rows)] * 2 + [np.cos(cols)] * 2, axis=1)
    sin = np.concatenate([-np.sin(rows), np.sin(rows), -np.sin(cols), np.sin(cols)], axis=1)
    cos = np.concatenate([np.ones((n_ctx, LANES)), cos], axis=0).astype(np.float32)
    sin = np.concatenate([np.zeros((n_ctx, LANES)), sin], axis=0).astype(np.float32)
    return jnp.asarray(cos), jnp.asarray(sin)


def kernel(x, c, ctx, c_ctx, ada_w, ada_b, norm1_g, norm2_g, ffn_w_in, ffn_conv_w, ffn_conv_b, ffn_w_out,
           gla_w_in, gla_a_w1, gla_a_w2, gla_a_b, gla_norm_g, gla_w_out, na_w_qkv, na_rpb, na_w_out, final_g):
    nb, n, d = x.shape
    n_ctx = ctx.shape[1]
    depth = ada_w.shape[0]
    dff = ffn_conv_b.shape[-1]
    assert n_ctx == TM and n % TM == 0 and depth == 2 and nb + 1 <= SUBLANES

    mod, w_gla_in, w_gla_out = _ada(jnp.concatenate([c, c_ctx[None, :]], axis=0), ada_w, ada_b,
                                    cast=((gla_w_in, 0), (gla_w_out, 0)))

    m0 = mod[0].reshape(SUBLANES, 1, 6 * d)
    dk = gla_a_w2.shape[-1]
    dv = gla_w_out.shape[1]
    assert 4 * 2 * GLA_LOW_RANK == LANES
    aw1 = jnp.concatenate([gla_a_w1[0, 0], gla_a_w1[0, 1]] * 4, axis=1).astype(BF16)
    zeros = jnp.zeros((GLA_LOW_RANK, dk), F32)
    w2 = jnp.concatenate([jnp.concatenate([gla_a_w2[0, 0], zeros], axis=1),
                          jnp.concatenate([zeros, gla_a_w2[0, 1]], axis=1)], axis=0)
    w2_hi = w2.astype(BF16)
    w2_lo = (w2 - w2_hi.astype(F32)).astype(BF16)
    aw2 = jnp.concatenate([w2_hi, w2_hi, w2_lo, w2_lo], axis=0)
    cos_t, sin_t = _rope_tables(n, n_ctx)
    q, k, v, r, gf, gb, ffn_wi0, ffn_wo0 = _gla_in(
        ctx, x, m0, norm1_g[0].reshape(1, d), w_gla_in, aw1, aw2,
        gla_a_b[0].reshape(1, 2 * dk), cos_t, sin_t, dk, dv, cast=((ffn_w_in, 0), (ffn_w_out, 0)))
    o_f, o_b = _gla_core(q, k, v, gf, gb, GLA_HEADS, n_ctx)
    x_mid, h_mid = _gla_out(o_f, o_b, r, ctx, x, m0, gla_norm_g[0].reshape(1, dv // GLA_HEADS),
                            w_gla_out, norm2_g[0].reshape(1, d), GLA_HEADS)
    ffn0 = functools.partial(_ffn, h_mid, x_mid, m0, ffn_wi0, ffn_conv_w[0], ffn_conv_b[0].reshape(1, dff),
                             ffn_wo0, final_g.reshape(1, d), final_norm=False)
    x_lat, ffn_wi1, ffn_wo1, w_qkv, w_na_out = ffn0(
        row0=0, rows=n, is_ctx=False, cast=((ffn_w_in, 1), (ffn_w_out, 1), (na_w_qkv, 0), (na_w_out, 0)))
    x_ctx = ffn0(row0=n, rows=n_ctx, is_ctx=True)

    m1 = mod[1].reshape(SUBLANES, 1, 6 * d)
    head_dim = d // NA_HEADS
    qn, kn, vn = _na_qkv(x_ctx, x_lat, m1, norm1_g[1].reshape(1, d), w_qkv, head_dim)
    bias = _na_bias(na_rpb[0])
    attn = _na_attn(qn, kn, vn, bias, n_ctx, head_dim)
    x_lat, h_lat = _na_out(attn, x_lat, m1, w_na_out, norm2_g[1].reshape(1, d))
    return _ffn(h_lat, x_lat, m1, ffn_wi1, ffn_conv_w[1], ffn_conv_b[1].reshape(1, dff), ffn_wo1,
                final_g.reshape(1, d), row0=0, rows=n, is_ctx=False, final_norm=True)
```

```python
import functools

import numpy as np
import jax
import jax.numpy as jnp
from jax import lax
from jax.experimental import pallas as pl
from jax.experimental.pallas import tpu as pltpu

GRID_W = 64
GLA_HEADS = 4
GLA_GATE_NORM = 16.0
GLA_LOW_RANK = 16
ROPE_BASE = 10000.0
NA_HEADS = 16
NA_KH = 8
NA_KW = 16
EPS = 1e-6
LOG2E = 1.4426950408889634

LANES = 128
SUBLANES = 8
VMEM_LIMIT = 56 * 1024 * 1024

TM = 256
ADA_TK = 256
GLA_C = 128
GLA_BLK = 256
GLA_DIAG = 16
FFN_TM = 1024
FFN_CHUNK = 512
NA_RB = 64
NA_U = 32
NA_HG = 4
NEG = -1e30

F32 = jnp.float32
BF16 = jnp.bfloat16


def _cparams(sem):
    return pltpu.CompilerParams(dimension_semantics=sem, vmem_limit_bytes=VMEM_LIMIT)


def _resident(shape):
    nd = len(shape)
    return pl.BlockSpec(shape, lambda *_: (0,) * nd, pipeline_mode=pl.Buffered(1))


def _silu(x):
    return x * (1.0 / (1.0 + jnp.exp(-x)))


def _norm_mod(x, g, shift, scale):
    ms = jnp.mean(x * x, axis=-1, keepdims=True)
    return x * lax.rsqrt(ms + EPS) * (g * (1.0 + scale)) + shift


def _ada_kernel(c_ref, w_ref, b_ref, *rest, n_rows, n_cast):
    o_ref = rest[n_cast]
    _cast_blocks(rest[:n_cast], rest[n_cast + 1:])
    j = pl.program_id(1)
    tk, d6 = w_ref.shape[1:]
    k0 = pl.multiple_of(j * tk, tk)
    rows = []
    for r in range(n_rows):
        sc = _silu(c_ref[r, pl.ds(k0, tk), :])
        parts = [jnp.sum(w_ref[0, :, n * LANES:(n + 1) * LANES] * sc, axis=0, keepdims=True)
                 for n in range(d6 // LANES)]
        rows.append(jnp.concatenate(parts, axis=1))
    rows.append(jnp.zeros((SUBLANES - n_rows, d6), F32))
    part = jnp.concatenate(rows, axis=0)

    @pl.when(j == 0)
    def _():
        o_ref[0] = part + b_ref[0]

    @pl.when(j > 0)
    def _():
        o_ref[0] += part


def _ada(c_rows, ada_w, ada_b, cast):
    depth, d, d6 = ada_w.shape
    n_rows = c_rows.shape[0]
    tk = ADA_TK
    c_b = jnp.broadcast_to(c_rows[:, :, None], (n_rows, d, LANES))
    cast_in, cast_out = _cast_specs(cast, depth, d // tk)
    return pl.pallas_call(
        functools.partial(_ada_kernel, n_rows=n_rows, n_cast=len(cast)),
        grid=(depth, d // tk),
        in_specs=[
            pl.BlockSpec((n_rows, d, LANES), lambda i, j: (0, 0, 0)),
            pl.BlockSpec((1, tk, d6), lambda i, j: (i, j, 0)),
            pl.BlockSpec((1, 1, d6), lambda i, j: (i, 0, 0)),
        ] + cast_in,
        out_specs=[pl.BlockSpec((1, SUBLANES, d6), lambda i, j: (i, 0, 0))] + cast_out,
        out_shape=[jax.ShapeDtypeStruct((depth, SUBLANES, d6), F32)]
        + [jax.ShapeDtypeStruct(a.shape[1:], BF16) for a, _ in cast],
        compiler_params=_cparams(("arbitrary", "arbitrary")),
        name="ada_mod",
    )(c_b, ada_w, ada_b.reshape(depth, 1, d6), *[a for a, _ in cast])


def _mod_spec(d, seg, nb):
    return pl.BlockSpec((1, 1, d), lambda b, i: (jnp.where(i == 0, nb, b), 0, seg))


def _stream_specs(d):
    return [pl.BlockSpec((1, TM, d), lambda b, i: (b, 0, 0)),
            pl.BlockSpec((1, TM, d), lambda b, i: (b, jnp.maximum(i - 1, 0), 0))]


def _stream_tile(ctx_ref, x_ref):
    return jnp.where(pl.program_id(1) == 0, ctx_ref[0], x_ref[0])


def _cast_specs(arrays, nb, steps):
    in_specs, out_specs = [], []
    for a, layer in arrays:
        _, rows, cols = a.shape
        tiles = rows // (2 * SUBLANES)
        nblk = max(k for k in range(1, nb * steps + 1) if tiles % k == 0)
        in_specs.append(pl.BlockSpec(
            (1, rows // nblk, cols),
            lambda b, i, nblk=nblk, layer=layer: (layer, jnp.minimum(b * steps + i, nblk - 1), 0)))
        out_specs.append(pl.BlockSpec(
            (rows // nblk, cols), lambda b, i, nblk=nblk: (jnp.minimum(b * steps + i, nblk - 1), 0)))
    return in_specs, out_specs


def _cast_blocks(src_refs, dst_refs):
    for src, dst in zip(src_refs, dst_refs):
        dst[...] = src[0].astype(BF16)


def _gla_in_kernel(c_ref, x_ref, sh_ref, sc_ref, g_ref, w_ref, aw1_ref, aw2_ref, ab_ref,
                   cos_ref, sin_ref, *rest, dk, dv, n_cast):
    q_ref, k_ref, v_ref, gf_ref, gb_ref = rest[n_cast:n_cast + 5]
    _cast_blocks(rest[:n_cast], rest[n_cast + 5:])
    h = _norm_mod(_stream_tile(c_ref, x_ref), g_ref[...], sh_ref[0], sc_ref[0]).astype(BF16)

    z = jnp.dot(h, aw1_ref[...], preferred_element_type=F32)
    z_hi = z.astype(BF16)
    z_lo = (z - z_hi.astype(F32)).astype(BF16)
    grp = lax.broadcasted_iota(jnp.int32, z.shape, 1) // (2 * GLA_LOW_RANK)
    zc = jnp.where((grp & 1) == 0, z_hi, z_lo)
    pre = jnp.dot(zc, aw2_ref[...], preferred_element_type=F32) + ab_ref[...]

    qk = jnp.dot(h, w_ref[:, :2 * dk], preferred_element_type=F32)
    g = (jnp.minimum(pre, 0.0) - jnp.log(1.0 + jnp.exp(-jnp.abs(pre)))) * (LOG2E / GLA_GATE_NORM)
    g_hi = g.astype(BF16)
    g_lo = (g - g_hi.astype(F32)).astype(BF16)
    v_ref[0] = jnp.dot(h, w_ref[:, 2 * dk: 2 * dk + dv], preferred_element_type=F32).astype(BF16)

    cos = cos_ref[...]
    sin = sin_ref[...]
    lane = lax.broadcasted_iota(jnp.int32, cos.shape, 1)
    low = (lane & 32) == 0
    qscale = float(LANES) ** -0.5

    def rope(t):
        partner = jnp.where(low, pltpu.roll(t, LANES - 32, 1), pltpu.roll(t, 32, 1))
        return t * cos + partner * sin

    for hd in range(dk // LANES):
        sl = slice(hd * LANES, (hd + 1) * LANES)
        q_ref[0, :, sl] = rope(qk[:, sl]) * qscale
        k_ref[0, :, sl] = rope(qk[:, dk + hd * LANES: dk + (hd + 1) * LANES])

    ti = lax.broadcasted_iota(jnp.int32, (GLA_C, GLA_C), 0)
    tj = lax.broadcasted_iota(jnp.int32, (GLA_C, GLA_C), 1)
    for o_ref, tri, sl in ((gf_ref, tj <= ti, slice(0, dk)), (gb_ref, tj >= ti, slice(dk, 2 * dk))):
        tri = jnp.where(tri, 1.0, 0.0).astype(BF16)
        tri2 = jnp.concatenate([tri, tri], axis=1)
        for ch in range(TM // GLA_C):
            rows = slice(ch * GLA_C, (ch + 1) * GLA_C)
            o_ref[0, rows, :] = jnp.dot(tri2, jnp.concatenate([g_hi[rows, sl], g_lo[rows, sl]], axis=0),
                                        preferred_element_type=F32)


def _gla_in(ctx, x, mod, norm_g, w_in, aw1, aw2, ab, cos_t, sin_t, dk, dv, cast):
    nb, n, d = x.shape
    t = ctx.shape[1] + n
    nt = t // TM
    kern = functools.partial(_gla_in_kernel, dk=dk, dv=dv, n_cast=len(cast))
    row = lambda w: pl.BlockSpec((1, TM, w), lambda b, i: (b, i, 0))
    cast_in, cast_out = _cast_specs(cast, nb, nt)
    return pl.pallas_call(
        kern,
        grid=(nb, nt),
        in_specs=_stream_specs(d) + [
            _mod_spec(d, 0, nb),
            _mod_spec(d, 1, nb),
            _resident((1, d)),
            _resident(w_in.shape),
            _resident(aw1.shape),
            _resident(aw2.shape),
            _resident(ab.shape),
            pl.BlockSpec((TM, LANES), lambda b, i: (i, 0)),
            pl.BlockSpec((TM, LANES), lambda b, i: (i, 0)),
        ] + cast_in,
        out_specs=[row(dk), row(dk), row(dv), row(dk), row(dk)] + cast_out,
        out_shape=[
            jax.ShapeDtypeStruct((nb, t, dk), F32),
            jax.ShapeDtypeStruct((nb, t, dk), F32),
            jax.ShapeDtypeStruct((nb, t, dv), BF16),
            jax.ShapeDtypeStruct((nb, t, dk), F32),
            jax.ShapeDtypeStruct((nb, t, dk), F32),
        ] + [jax.ShapeDtypeStruct(a.shape[1:], BF16) for a, _ in cast],
        compiler_params=_cparams(("arbitrary", "arbitrary")),
        name="gla_in",
    )(ctx, x, mod, mod, norm_g, w_in, aw1, aw2, ab, cos_t, sin_t, *[a for a, _ in cast])


def _row_bcast(b, period, offset):
    c, w = b.shape
    if period == c:
        return jnp.broadcast_to(b[offset:offset + 1, :], (c, w))
    b3 = b.reshape(c // period, period, w)
    return jnp.broadcast_to(b3[:, offset:offset + 1, :], b3.shape).reshape(c, w)


def _pair_scores(lhs, rhs, first):
    zero = jnp.zeros_like(rhs)
    blockdiag = jnp.concatenate([jnp.where(first, rhs, zero), jnp.where(first, zero, rhs)], axis=0)
    return lax.dot_general(lhs, blockdiag, (((1,), (1,)), ((), ())), preferred_element_type=F32)


def _gla_pair(q, k, b, rev, kside, same, causal, first):
    c = q.shape[0]
    b_last = b[0:1, :] if rev else b[c - 1:c, :]
    q_in = (q * jnp.exp2(b)).astype(BF16)
    k_out = (k * jnp.exp2(b_last - b)).astype(BF16)
    b_ref = _row_bcast(b, GLA_DIAG, GLA_DIAG - 1 if rev else 0)
    qd = (q * jnp.exp2(b - b_ref)).astype(BF16)
    kd = (k * jnp.exp2(b_ref - b)).astype(BF16)
    att = _pair_scores(qd, kd, first)
    period = 2 * GLA_DIAG
    while period <= c:
        half = period // 2
        b_ref = _row_bcast(b, period, half if rev else half - 1)
        u = (jnp.where(kside[period], k, q) * jnp.exp2(-jnp.abs(b - b_ref))).astype(BF16)
        att = jnp.where(same[half], att, _pair_scores(u, u, first))
        period *= 2
    att = jnp.where(causal, att, 0.0).astype(BF16)
    return q_in, k_out, att, jnp.exp2(b_last)


def _gla_state_step(q_in, att, k_out, v, decay_row, s):
    lhs = jnp.concatenate([q_in, att], axis=1)
    rhs = jnp.concatenate([s.astype(BF16), v], axis=0)
    o = jnp.dot(lhs, rhs, preferred_element_type=F32)
    decay = jnp.transpose(jnp.broadcast_to(decay_row, (LANES, LANES)))
    decay = jnp.concatenate([decay] * (s.shape[1] // LANES), axis=1)
    kv = lax.dot_general(k_out, v, (((0,), (0,)), ((), ())), preferred_element_type=F32)
    return o, s * decay + kv


def _gla_core_kernel(qf_ref, kf_ref, vf_ref, gf_ref, qb_ref, kb_ref, vb_ref, gb_ref,
                     of_ref, ob_ref, s_ref, *, heads, dvh):
    @pl.when(pl.program_id(1) == 0)
    def _():
        s_ref[...] = jnp.zeros_like(s_ref)

    c = GLA_C
    assert c == LANES and heads % 2 == 0
    row = lax.broadcasted_iota(jnp.int32, (c, 2 * LANES), 0)
    first = lax.broadcasted_iota(jnp.int32, (c, 2 * LANES), 1) < LANES
    ti = lax.broadcasted_iota(jnp.int32, (c, 2 * c), 0)
    tj = lax.broadcasted_iota(jnp.int32, (c, 2 * c), 1) & (c - 1)
    x = ti ^ tj
    same = {}
    half = GLA_DIAG
    while half < c:
        same[half] = x < half
        half *= 2

    n_chunks = qf_ref.shape[1] // c
    dirs = ((qf_ref, kf_ref, vf_ref, gf_ref, of_ref), (qb_ref, kb_ref, vb_ref, gb_ref, ob_ref))
    masks = []
    for d in range(2):
        rev = d == 1
        kside = {}
        period = 2 * GLA_DIAG
        while period <= c:
            off = row & (period - 1)
            kside[period] = (off >= period // 2) if rev else (off < period // 2)
            period *= 2
        masks.append((kside, (tj >= ti) if rev else (tj <= ti)))

    pre = {}
    for step in range(n_chunks):
        for d, (q_ref, k_ref, v_ref, b_ref, o_ref) in enumerate(dirs):
            rev = d == 1
            ch = n_chunks - 1 - step if rev else step
            rows = slice(ch * c, (ch + 1) * c)
            for pr in range(heads // 2):
                psl = slice(2 * pr * LANES, (2 * pr + 2) * LANES)
                pre[step, d, pr] = _gla_pair(q_ref[0, rows, psl], k_ref[0, rows, psl], b_ref[0, rows, psl],
                                             rev, masks[d][0], same, masks[d][1], first)
    for step in range(n_chunks):
        for d, (q_ref, k_ref, v_ref, b_ref, o_ref) in enumerate(dirs):
            ch = n_chunks - 1 - step if d == 1 else step
            rows = slice(ch * c, (ch + 1) * c)
            for pr in range(heads // 2):
                q_in, k_out, att, decay = pre[step, d, pr]
                for j in range(2):
                    hd = 2 * pr + j
                    hsl = slice(j * LANES, (j + 1) * LANES)
                    vsl = slice(hd * dvh, (hd + 1) * dvh)
                    o, s_new = _gla_state_step(q_in[:, hsl], att[:, j * c:(j + 1) * c], k_out[:, hsl],
                                               v_ref[0, rows, vsl], decay[:, hsl], s_ref[d, hd])
                    o_ref[0, rows, vsl] = o.astype(o_ref.dtype)
                    s_ref[d, hd] = s_new


def _gla_core(q, k, v, gf, gb, heads, n_ctx):
    nb, t, dk = q.shape
    dv = v.shape[-1]
    assert n_ctx % GLA_BLK == 0 and t % GLA_BLK == 0
    nc = t // GLA_BLK
    ncx = n_ctx // GLA_BLK
    fwd = lambda b, s: (b, s, 0)
    bwd = lambda b, s: (b, jnp.where(s < ncx, ncx - 1 - s, nc - 1 + ncx - s), 0)
    kern = functools.partial(_gla_core_kernel, heads=heads, dvh=dv // heads)
    blk = lambda w, m: pl.BlockSpec((1, GLA_BLK, w), m)
    return pl.pallas_call(
        kern,
        grid=(nb, nc),
        in_specs=[blk(dk, fwd), blk(dk, fwd), blk(dv, fwd), blk(dk, fwd),
                  blk(dk, bwd), blk(dk, bwd), blk(dv, bwd), blk(dk, bwd)],
        out_specs=[blk(dv, fwd), blk(dv, bwd)],
        out_shape=[jax.ShapeDtypeStruct((nb, t, dv), BF16)] * 2,
        scratch_shapes=[pltpu.VMEM((2, heads, dk // heads, dv // heads), F32)],
        compiler_params=_cparams(("parallel", "arbitrary")),
        name="gla_core",
    )(q, k, v, gf, q, k, v, gb)


def _gla_out_kernel(of_ref, ob_ref, c_ref, x_ref, sh1_ref, sc1_ref, gate_ref, sh2_ref, sc2_ref, g1_ref, wi_ref,
                    ng_ref, w_ref, g2_ref, xo_ref, ho_ref, *, heads):
    xt = _stream_tile(c_ref, x_ref)
    h = _norm_mod(xt, g1_ref[...], sh1_ref[0], sc1_ref[0]).astype(BF16)
    r = jnp.dot(h, wi_ref[:, wi_ref.shape[1] - of_ref.shape[2]:], preferred_element_type=F32)
    o = of_ref[0].astype(F32) + ob_ref[0].astype(F32)
    dvh = o.shape[1] // heads
    parts = []
    for hd in range(heads):
        oh = o[:, hd * dvh:(hd + 1) * dvh]
        ms = jnp.mean(oh * oh, axis=-1, keepdims=True)
        parts.append(oh * lax.rsqrt(ms + EPS) * ng_ref[...])
    y = jnp.concatenate(parts, axis=1) * _silu(r)
    out = jnp.dot(y.astype(BF16), w_ref[...], preferred_element_type=F32)
    x_mid = xt + gate_ref[0] * out
    xo_ref[0] = x_mid
    ho_ref[0] = _norm_mod(x_mid, g2_ref[...], sh2_ref[0], sc2_ref[0]).astype(BF16)


def _gla_out(o_f, o_b, ctx, x, mod, norm1_g, w_in, norm_g, w_out, norm2_g, heads):
    nb, t, dv = o_f.shape
    d = x.shape[-1]
    nt = t // TM
    row = lambda w: pl.BlockSpec((1, TM, w), lambda b, i: (b, i, 0))
    out = pl.BlockSpec((1, TM, d), lambda b, i: (b, jnp.where(i == 0, nt - 1, i - 1), 0))
    return pl.pallas_call(
        functools.partial(_gla_out_kernel, heads=heads),
        grid=(nb, nt),
        in_specs=[row(dv), row(dv)] + _stream_specs(d) + [
            _mod_spec(d, 0, nb), _mod_spec(d, 1, nb), _mod_spec(d, 2, nb), _mod_spec(d, 3, nb), _mod_spec(d, 4, nb),
            _resident((1, d)), _resident(w_in.shape),
            _resident(norm_g.shape), _resident(w_out.shape), _resident((1, d))],
        out_specs=[out, out],
        out_shape=[jax.ShapeDtypeStruct((nb, t, d), F32), jax.ShapeDtypeStruct((nb, t, d), BF16)],
        compiler_params=_cparams(("parallel", "parallel")),
        name="gla_out",
    )(o_f, o_b, ctx, x, mod, mod, mod, mod, mod, norm1_g, w_in, norm_g, w_out, norm2_g)


def _ffn_kernel(h_ref, hp_ref, hn_ref, x_ref, gate_ref, wi_ref, cw_ref, cb_ref, wo_ref, fg_ref,
                *rest, chunks, final_norm, n_cast):
    o_ref = rest[n_cast]
    hbuf_ref, act_ref = rest[-2:]
    _cast_blocks(rest[:n_cast], rest[n_cast + 1:-2])
    i = pl.program_id(1)
    tm = h_ref.shape[1]
    dff = act_ref.shape[1]
    hs = SUBLANES
    has_prev = i > 0
    has_next = i < pl.num_programs(1) - 1
    hbuf_ref[0:tm, :] = h_ref[0]
    nxt = jnp.where(has_next, hn_ref[0, 0:hs, :].astype(F32), 0.0)
    prv = jnp.where(has_prev, hp_ref[0, hs:2 * hs, :].astype(F32), 0.0)
    hbuf_ref[tm:, :] = jnp.concatenate([nxt, prv], axis=0).astype(BF16)
    n = tm + 2 * hs

    for (c0, cw) in chunks:
        a = jnp.dot(hbuf_ref[...], wi_ref[:, c0:c0 + cw], preferred_element_type=F32)
        val = jnp.dot(hbuf_ref[0:tm, :], wi_ref[:, dff + c0:dff + c0 + cw], preferred_element_type=F32)
        w3 = cw_ref[:, c0:c0 + cw]
        conv = (pltpu.roll(a, 1, 0)[0:tm] * w3[0:1] + a[0:tm] * w3[1:2]
                + pltpu.roll(a, n - 1, 0)[0:tm] * w3[2:3] + cb_ref[:, c0:c0 + cw])
        act_ref[:, c0:c0 + cw] = (_silu(conv) * val).astype(BF16)
    y = x_ref[0] + gate_ref[0] * jnp.dot(act_ref[...], wo_ref[...], preferred_element_type=F32)
    if final_norm:
        ms = jnp.mean(y * y, axis=-1, keepdims=True)
        y = y * lax.rsqrt(ms + EPS) * fg_ref[...]
    o_ref[0] = y


def _ffn(h, x, mod, w_in, conv_w, conv_b, w_out, final_g, *, row0, rows, is_ctx, final_norm, cast=()):
    nb, _, d = x.shape
    t = rows
    dff = w_out.shape[0]
    tm = min(FFN_TM, t)
    hb = 2 * SUBLANES
    bpt = tm // hb
    assert row0 % tm == 0 and rows % tm == 0
    t0 = row0 // tm
    b0 = row0 // hb
    nblk = t // hb
    mod_row = (lambda b: nb) if is_ctx else (lambda b: b)
    chunks = []
    c0 = 0
    while c0 < dff:
        cw = min(FFN_CHUNK, dff - c0)
        chunks.append((c0, cw))
        c0 += cw
    kern = functools.partial(_ffn_kernel, chunks=tuple(chunks), final_norm=final_norm, n_cast=len(cast))
    row = pl.BlockSpec((1, tm, d), lambda b, i: (b, t0 + i, 0))
    cast_in, cast_out = _cast_specs(cast, nb, t // tm)
    out = pl.pallas_call(
        kern,
        grid=(nb, t // tm),
        in_specs=[
            row,
            pl.BlockSpec((1, hb, d), lambda b, i: (b, b0 + jnp.maximum(i * bpt - 1, 0), 0)),
            pl.BlockSpec((1, hb, d), lambda b, i: (b, b0 + jnp.minimum((i + 1) * bpt, nblk - 1), 0)),
            row,
            pl.BlockSpec((1, 1, d), lambda b, i: (mod_row(b), 0, 5)),
            _resident(w_in.shape),
            _resident(conv_w.shape),
            _resident(conv_b.shape),
            _resident(w_out.shape),
            _resident((1, d)),
        ] + cast_in,
        out_specs=[pl.BlockSpec((1, tm, d), lambda b, i: (b, i, 0))] + cast_out,
        out_shape=[jax.ShapeDtypeStruct((nb, t, d), F32)]
        + [jax.ShapeDtypeStruct(a.shape[1:], BF16) for a, _ in cast],
        scratch_shapes=[pltpu.VMEM((tm + 2 * SUBLANES, d), BF16), pltpu.VMEM((tm, dff), BF16)],
        compiler_params=_cparams(("arbitrary", "arbitrary") if cast else ("parallel", "parallel")),
        name=("ffn_final" if final_norm else "ffn") + ("_ctx" if is_ctx else ""),
    )(h, h, h, x, mod, w_in, conv_w, conv_b, w_out, final_g, *[a for a, _ in cast])
    return out if cast else out[0]


def _na_qkv_kernel(c_ref, x_ref, sh_ref, sc_ref, g_ref, w_ref, q_ref, k_ref, v_ref, *, d, qscale):
    h = _norm_mod(_stream_tile(c_ref, x_ref), g_ref[...], sh_ref[0], sc_ref[0]).astype(BF16)
    proj = jnp.dot(h, w_ref[...], preferred_element_type=F32)
    q_ref[0] = (proj[:, :d] * qscale).astype(BF16)
    k_ref[0] = proj[:, d:2 * d].astype(BF16)
    v_ref[0] = proj[:, 2 * d:].astype(BF16)


def _na_qkv(ctx, x, mod, norm_g, w_qkv, head_dim):
    nb, n, d = x.shape
    t = ctx.shape[1] + n
    row = lambda w: pl.BlockSpec((1, TM, w), lambda b, i: (b, i, 0))
    return pl.pallas_call(
        functools.partial(_na_qkv_kernel, d=d, qscale=LOG2E * float(head_dim) ** -0.5),
        grid=(nb, t // TM),
        in_specs=_stream_specs(d) + [_mod_spec(d, 0, nb), _mod_spec(d, 1, nb),
                                     _resident((1, d)), _resident(w_qkv.shape)],
        out_specs=[row(d)] * 3,
        out_shape=[jax.ShapeDtypeStruct((nb, t, d), BF16)] * 3,
        compiler_params=_cparams(("parallel", "parallel")),
        name="na_qkv",
    )(ctx, x, mod, mod, norm_g, w_qkv)


def _na_attn_kernel(q_ref, k_ref, v_ref, bias_ref, o_ref, *, n_ctx, rows_n, head_dim):
    rb = pl.program_id(2)
    hw = q_ref.shape[-1]
    nwin = NA_KH * GRID_W
    nt = (((1,), (1,)), ((), ()))
    lane = lax.broadcasted_iota(jnp.int32, (GRID_W, hw), 1)
    hmask = [(lane // head_dim) == hd for hd in range(NA_HG)]

    def window(i):
        r = rb * NA_RB + i
        rs = jnp.clip(r - NA_KH // 2, 0, rows_n - NA_KH)
        lo = rs - r + (NA_KH - 1)
        return r, lo, pl.multiple_of(n_ctx + rs * GRID_W, GRID_W)

    def scores(i):
        r, lo, start = window(i)
        q = q_ref[0, pl.ds(pl.multiple_of(n_ctx + r * GRID_W, GRID_W), GRID_W), :]
        qs = jnp.concatenate([jnp.where(hmask[hd], q, jnp.zeros_like(q)) for hd in range(NA_HG)], axis=0)
        bias = jnp.concatenate(
            [jnp.concatenate([bias_ref[0, hd, lo + 2 * m] for m in range(NA_KH // 2)], axis=1)
             for hd in range(NA_HG)], axis=0)
        s_nb = lax.dot_general(qs, k_ref[0, pl.ds(start, nwin), :], nt, preferred_element_type=F32) + bias
        s_cx = lax.dot_general(qs, k_ref[0, 0:n_ctx, :], nt, preferred_element_type=F32)
        s = jnp.concatenate([s_nb, s_cx], axis=1)
        return s, jnp.max(s, axis=-1, keepdims=True)

    def finish(i, s, m):
        r, lo, start = window(i)
        p = jnp.exp2(s - m)
        l = jnp.sum(p, axis=-1, keepdims=True)
        p = p.astype(BF16)
        o = (jnp.dot(p[:, :nwin], v_ref[0, pl.ds(start, nwin), :], preferred_element_type=F32)
             + jnp.dot(p[:, nwin:], v_ref[0, 0:n_ctx, :], preferred_element_type=F32))
        o = o * (1.0 / l)
        out = o[0:GRID_W]
        for hd in range(1, NA_HG):
            out = jnp.where(hmask[hd], o[hd * GRID_W:(hd + 1) * GRID_W], out)
        o_ref[0, pl.ds(pl.multiple_of(i * GRID_W, GRID_W), GRID_W), :] = out.astype(BF16)

    def body(it, carry):
        nxt = scores(it * NA_U)
        for u in range(NA_U):
            cur = nxt
            if u + 1 < NA_U:
                nxt = scores(it * NA_U + u + 1)
            finish(it * NA_U + u, *cur)
        return carry

    lax.fori_loop(0, NA_RB // NA_U, body, 0)


def _na_bias(rpb):
    heads, ndr, ndc = rpb.shape
    c = np.arange(GRID_W)
    cs = np.clip(c - NA_KW // 2, 0, GRID_W - NA_KW)
    onehot = np.zeros((2 * ndc, GRID_W, 2 * GRID_W), np.float32)
    mask = np.full((GRID_W, 2 * GRID_W), NEG, np.float32)
    for half in range(2):
        for cq in range(GRID_W):
            for kc in range(cs[cq], cs[cq] + NA_KW):
                onehot[half * ndc + kc - cq + NA_KW - 1, cq, half * GRID_W + kc] = 1.0
                mask[cq, half * GRID_W + kc] = 0.0
    pair = jnp.concatenate([rpb[:, :ndr - 1, :], rpb[:, 1:, :]], axis=-1)
    tiles = jnp.einsum('hdk,kcl->hdcl', pair * LOG2E, onehot, precision=lax.Precision.HIGHEST) + mask
    return tiles.reshape(heads // NA_HG, NA_HG, ndr - 1, GRID_W, 2 * GRID_W)


def _na_attn(q, k, v, bias, n_ctx, head_dim):
    nb, t, d = q.shape
    rows_n = (t - n_ctx) // GRID_W
    nrb = rows_n // NA_RB
    nq = NA_RB * GRID_W
    hw = NA_HG * head_dim
    ngrp = d // hw
    assert n_ctx % nq == 0 or nq % n_ctx == 0
    kern = functools.partial(_na_attn_kernel, n_ctx=n_ctx, rows_n=rows_n, head_dim=head_dim)
    return pl.pallas_call(
        kern,
        grid=(nb, ngrp, nrb),
        in_specs=[
            pl.BlockSpec((1, t, hw), lambda b, g, r: (b, 0, g)),
            pl.BlockSpec((1, t, hw), lambda b, g, r: (b, 0, g)),
            pl.BlockSpec((1, t, hw), lambda b, g, r: (b, 0, g)),
            pl.BlockSpec((1,) + bias.shape[1:], lambda b, g, r: (g, 0, 0, 0, 0)),
        ],
        out_specs=pl.BlockSpec((1, nq, hw), lambda b, g, r: (b, r, g)),
        out_shape=jax.ShapeDtypeStruct((nb, t - n_ctx, d), BF16),
        compiler_params=_cparams(("parallel", "parallel", "arbitrary")),
        name="na_attn",
    )(q, k, v, bias)


def _na_out_kernel(a_ref, x_ref, gate_ref, sh2_ref, sc2_ref, w_ref, g2_ref, o_ref, h_ref):
    out = jnp.dot(a_ref[0], w_ref[...], preferred_element_type=F32)
    x_mid = x_ref[0] + gate_ref[0] * out
    o_ref[0] = x_mid
    h_ref[0] = _norm_mod(x_mid, g2_ref[...], sh2_ref[0], sc2_ref[0]).astype(BF16)


def _na_out(attn, x, mod, w_out, norm2_g):
    nb, n, d = x.shape
    tm = FFN_TM
    row = pl.BlockSpec((1, tm, d), lambda b, i: (b, i, 0))
    seg = lambda s: pl.BlockSpec((1, 1, d), lambda b, i: (b, 0, s))
    return pl.pallas_call(
        _na_out_kernel,
        grid=(nb, n // tm),
        in_specs=[row, row, seg(2), seg(3), seg(4), _resident(w_out.shape), _resident((1, d))],
        out_specs=[row, row],
        out_shape=[jax.ShapeDtypeStruct((nb, n, d), F32), jax.ShapeDtypeStruct((nb, n, d), BF16)],
        compiler_params=_cparams(("parallel", "parallel")),
        name="na_out",
    )(attn, x, mod, mod, mod, w_out, norm2_g)


def _rope_tables(n_lat, n_ctx):
    half = LANES // 2
    inv = 1.0 / (ROPE_BASE ** (np.arange(0, half, 2, dtype=np.float64) / half))
    pos = np.arange(n_lat)
    rows = (pos // GRID_W).astype(np.float64)[:, None] * inv[None, :]
    cols = (pos % GRID_W).astype(np.float64)[:, None] * inv[None, :]
    cos = np.concatenate([np.cos(rows)] * 2 + [np.cos(cols)] * 2, axis=1)
    sin = np.concatenate([-np.sin(rows), np.sin(rows), -np.sin(cols), np.sin(cols)], axis=1)
    cos = np.concatenate([np.ones((n_ctx, LANES)), cos], axis=0).astype(np.float32)
    sin = np.concatenate([np.zeros((n_ctx, LANES)), sin], axis=0).astype(np.float32)
    return jnp.asarray(cos), jnp.asarray(sin)


def kernel(x, c, ctx, c_ctx, ada_w, ada_b, norm1_g, norm2_g, ffn_w_in, ffn_conv_w, ffn_conv_b, ffn_w_out,
           gla_w_in, gla_a_w1, gla_a_w2, gla_a_b, gla_norm_g, gla_w_out, na_w_qkv, na_rpb, na_w_out, final_g):
    nb, n, d = x.shape
    n_ctx = ctx.shape[1]
    depth = ada_w.shape[0]
    dff = ffn_conv_b.shape[-1]
    assert n_ctx == TM and n % TM == 0 and depth == 2 and nb + 1 <= SUBLANES

    mod, w_gla_in, w_gla_out = _ada(jnp.concatenate([c, c_ctx[None, :]], axis=0), ada_w, ada_b,
                                    cast=((gla_w_in, 0), (gla_w_out, 0)))

    m0 = mod[0].reshape(SUBLANES, 1, 6 * d)
    dk = gla_a_w2.shape[-1]
    dv = gla_w_out.shape[1]
    assert 4 * 2 * GLA_LOW_RANK == LANES
    aw1 = jnp.concatenate([gla_a_w1[0, 0], gla_a_w1[0, 1]] * 4, axis=1).astype(BF16)
    zeros = jnp.zeros((GLA_LOW_RANK, dk), F32)
    w2 = jnp.concatenate([jnp.concatenate([gla_a_w2[0, 0], zeros], axis=1),
                          jnp.concatenate([zeros, gla_a_w2[0, 1]], axis=1)], axis=0)
    w2_hi = w2.astype(BF16)
    w2_lo = (w2 - w2_hi.astype(F32)).astype(BF16)
    aw2 = jnp.concatenate([w2_hi, w2_hi, w2_lo, w2_lo], axis=0)
    cos_t, sin_t = _rope_tables(n, n_ctx)
    q, k, v, gf, gb, ffn_wi0, ffn_wo0 = _gla_in(
        ctx, x, m0, norm1_g[0].reshape(1, d), w_gla_in, aw1, aw2,
        gla_a_b[0].reshape(1, 2 * dk), cos_t, sin_t, dk, dv, cast=((ffn_w_in, 0), (ffn_w_out, 0)))
    o_f, o_b = _gla_core(q, k, v, gf, gb, GLA_HEADS, n_ctx)
    x_mid, h_mid = _gla_out(o_f, o_b, ctx, x, m0, norm1_g[0].reshape(1, d), w_gla_in,
                            gla_norm_g[0].reshape(1, dv // GLA_HEADS),
                            w_gla_out, norm2_g[0].reshape(1, d), GLA_HEADS)
    ffn0 = functools.partial(_ffn, h_mid, x_mid, m0, ffn_wi0, ffn_conv_w[0], ffn_conv_b[0].reshape(1, dff),
                             ffn_wo0, final_g.reshape(1, d), final_norm=False)
    x_lat, ffn_wi1, ffn_wo1, w_qkv, w_na_out = ffn0(
        row0=0, rows=n, is_ctx=False, cast=((ffn_w_in, 1), (ffn_w_out, 1), (na_w_qkv, 0), (na_w_out, 0)))
    x_ctx = ffn0(row0=n, rows=n_ctx, is_ctx=True)

    m1 = mod[1].reshape(SUBLANES, 1, 6 * d)
    head_dim = d // NA_HEADS
    qn, kn, vn = _na_qkv(x_ctx, x_lat, m1, norm1_g[1].reshape(1, d), w_qkv, head_dim)
    bias = _na_bias(na_rpb[0])
    attn = _na_attn(qn, kn, vn, bias, n_ctx, head_dim)
    x_lat, h_lat = _na_out(attn, x_lat, m1, w_na_out, norm2_g[1].reshape(1, d))
    return _ffn(h_lat, x_lat, m1, ffn_wi1, ffn_conv_w[1], ffn_conv_b[1].reshape(1, dff), ffn_wo1,
                final_g.reshape(1, d), row0=0, rows=n, is_ctx=False, final_norm=True)
```
